```python
import jax, jax.numpy as jnp
from jax import lax
import numpy as np

D_MODEL = 1024
BATCH = 8
SEQ = 2048
DEPTH = 2
DEC_BATCH = 128
DEC_SEQ = 4
PAST_LEN = 16384
PAGE_SIZE = 128

N_EVEN = (DEPTH + 1) // 2
N_ODD = DEPTH // 2
NORM_EPS = 1e-6

POOL_WINDOWS = (2, 4, 8, 16)
POOL_GROUPS = len(POOL_WINDOWS)
POOL_GROUP_DIM = D_MODEL // 16
POOL_DIM = POOL_GROUPS * POOL_GROUP_DIM
POOL_BUF = max(POOL_WINDOWS) - 1

RWKV_HEAD_DIM = 64
RWKV_DIM = D_MODEL - POOL_DIM
RWKV_HEADS = RWKV_DIM // RWKV_HEAD_DIM
DECAY_LORA = 64
AAA_LORA = 64
GATE_LORA = 128
RWKV_PROJ = 3 * RWKV_DIM + DECAY_LORA + AAA_LORA + GATE_LORA
RWKV_GN_EPS = 64e-5
EVEN_PROJ = POOL_DIM + RWKV_PROJ
EVEN_MIX = POOL_DIM + RWKV_DIM

CHUNK = 128
GMLP_DIM = D_MODEL // 2
GMLP_HEADS = 4
GMLP_HEAD_DIM = GMLP_DIM // GMLP_HEADS
LN_EPS = 1e-5

LRU_DIM = D_MODEL // 2
LRU_BLOCKS = 8
LRU_BLOCK_DIM = LRU_DIM // LRU_BLOCKS
CONV_WIDTH = 4
LRU_C = 8.0
ODD_PROJ = 2 * GMLP_DIM + 2 * LRU_DIM
ODD_MIX = GMLP_DIM + LRU_DIM

D_FF = 4 * D_MODEL

kernel_name = 'pool_rwkv7_gmlp_rglru_hybrid_step'


def rmsnorm(x, g):
    xf = x.astype(jnp.float32)
    y = xf * lax.rsqrt(jnp.mean(xf * xf, -1, keepdims=True) + NORM_EPS) * g.astype(jnp.float32)
    return y.astype(x.dtype)


def pool_mixer(u, buf, start, w_grp, scale):
    B, T, _ = u.shape
    full = jnp.concatenate([buf.astype(jnp.float32), u.astype(jnp.float32)], 1)
    cs = jnp.concatenate([jnp.zeros((B, 1, POOL_DIM), jnp.float32), jnp.cumsum(full, 1)], 1)
    end = POOL_BUF + 1
    pos = start + jnp.arange(T)
    means = []
    for gi, w in enumerate(POOL_WINDOWS):
        ch = slice(gi * POOL_GROUP_DIM, (gi + 1) * POOL_GROUP_DIM)
        s = cs[:, end:end + T, ch] - cs[:, end - w:end - w + T, ch]
        cnt = jnp.minimum(w, pos + 1).astype(jnp.float32)
        means.append(s / cnt[None, :, None])
    d = (jnp.concatenate(means, -1) - full[:, POOL_BUF:]).reshape(B, T, POOL_GROUPS, POOL_GROUP_DIM)
    y = jnp.einsum('btgc,gcd->btgd', d, w_grp.astype(jnp.float32)).reshape(B, T, POOL_DIM)
    return y * scale.astype(jnp.float32), full[:, -POOL_BUF:]


def rwkv7_mixer(p, shift_prev, wkv0, mu, w0, w_w2, a0, a_w2, g_w2, k_k, k_a, r_k, gn_g, gn_b):
    f32 = jnp.float32
    B, T, _ = p.shape
    p = p.astype(f32)
    p_prev = jnp.concatenate([shift_prev.astype(f32)[:, None], p[:, :-1]], 1)
    xs = p + (p_prev - p) * mu
    splits = [RWKV_DIM, 2 * RWKV_DIM, 3 * RWKV_DIM, 3 * RWKV_DIM + DECAY_LORA, 3 * RWKV_DIM + DECAY_LORA + AAA_LORA]
    r, k, v, cw, ca, cg = jnp.split(xs, splits, axis=-1)
    w = -jax.nn.softplus(-(w0 + jnp.tanh(cw) @ w_w2)) - 0.5
    decay = jnp.exp(-jnp.exp(w))
    a = jax.nn.sigmoid(a0 + ca @ a_w2)
    g = jax.nn.sigmoid(cg) @ g_w2
    hs = lambda t: t.reshape(B, T, RWKV_HEADS, RWKV_HEAD_DIM)
    kk = hs(k * k_k)
    kk = kk / jnp.maximum(jnp.sqrt(jnp.sum(kk * kk, -1, keepdims=True)), 1e-12)
    k = k * (1.0 + (a - 1.0) * k_a)
    r, k, v, decay, a = hs(r), hs(k), hs(v), hs(decay), hs(a)

    def step(S, inp):
        r_t, k_t, v_t, w_t, kk_t, a_t = inp
        Sk = jnp.einsum('bhvk,bhk->bhv', S, kk_t)
        S = (S * w_t[:, :, None, :] - Sk[..., None] * (kk_t * a_t)[:, :, None, :]
             + v_t[..., None] * k_t[:, :, None, :])
        return S, jnp.einsum('bhvk,bhk->bhv', S, r_t)

    tm = lambda t: jnp.moveaxis(t, 1, 0)
    S_T, o = lax.scan(step, wkv0.astype(f32), (tm(r), tm(k), tm(v), tm(decay), tm(kk), tm(a)))
    o = jnp.moveaxis(o, 0, 1)
    m = jnp.mean(o, -1, keepdims=True)
    var = jnp.mean(jnp.square(o - m), -1, keepdims=True)
    o = ((o - m) * lax.rsqrt(var + RWKV_GN_EPS)).reshape(B, T, RWKV_DIM) * gn_g + gn_b
    bonus = jnp.sum(r * k * r_k, -1, keepdims=True) * v
    o = o + bonus.reshape(B, T, RWKV_DIM)
    return o * g, p[:, -1], S_T


def gmlp_mixer(q, ln_g, ln_b, ws, bs):
    f32 = jnp.float32
    B, T, _ = q.shape
    z = jax.nn.gelu(q.astype(f32))
    u, v = z[..., :GMLP_DIM], z[..., GMLP_DIM:]
    m = jnp.mean(v, -1, keepdims=True)
    var = jnp.mean(jnp.square(v - m), -1, keepdims=True)
    v = (v - m) * lax.rsqrt(var + LN_EPS) * ln_g + ln_b
    pad = (-T) % CHUNK
    nc = (T + pad) // CHUNK
    vh = jnp.pad(v, ((0, 0), (0, pad), (0, 0))).reshape(B, nc, CHUNK, GMLP_HEADS, GMLP_HEAD_DIM)
    causal = jnp.tril(jnp.ones((CHUNK, CHUNK), bool))
    wm = jnp.where(causal[None], ws.astype(f32), 0.0)
    mix = jnp.einsum('hij,bcjhd->bcihd', wm, vh) + bs.astype(f32).T[None, None, :, :, None]
    mix = mix.reshape(B, nc * CHUNK, GMLP_DIM)[:, :T]
    return u * mix, v


def rglru_mixer(q, conv_buf, h0, conv_w, conv_b, wx, bx, wa, ba, lam):
    f32 = jnp.float32
    B, T, _ = q.shape
    q = q.astype(f32)
    gate_in, xr = q[..., :LRU_DIM], q[..., LRU_DIM:]
    full = jnp.concatenate([conv_buf.astype(f32), xr], 1)
    xc = full[:, CONV_WIDTH - 1:] * conv_w[CONV_WIDTH - 1]
    for j in range(CONV_WIDTH - 1):
        xc = xc + full[:, j:j + T] * conv_w[j]
    xc = xc + conv_b
    xb = xc.reshape(B, T, LRU_BLOCKS, LRU_BLOCK_DIM)
    gx = jax.nn.sigmoid(jnp.einsum('btnc,ncd->btnd', xb, wx).reshape(B, T, LRU_DIM) + bx)
    ga = jax.nn.sigmoid(jnp.einsum('btnc,ncd->btnd', xb, wa).reshape(B, T, LRU_DIM) + ba)
    log_a = -LRU_C * ga * jax.nn.softplus(-lam)
    a = jnp.exp(log_a)
    b = jnp.sqrt(-jnp.expm1(2.0 * log_a)) * gx * xc
    b = b.at[:, 0].add(a[:, 0] * h0.astype(f32))
    comb = lambda l, r: (l[0] * r[0], r[0] * l[1] + r[1])
    _, h = lax.associative_scan(comb, (a, b), axis=1)
    y = h * jax.nn.gelu(gate_in)
    return y, full[:, -(CONV_WIDTH - 1):], h[:, -1]


def trunk(x, start, st_pool, st_shift, st_wkv, st_conv, st_lru, prm):
    n_pool, n_shift, n_wkv, n_conv, n_lru, n_v = [], [], [], [], [], []
    for layer in range(DEPTH):
        i = layer // 2
        if layer % 2 == 0:
            h = rmsnorm(x, prm['ev_norm_g'][i])
            p = h @ prm['ev_w_in'][i]
            yA, nb = pool_mixer(p[..., :POOL_DIM], st_pool[i], start, prm['pool_w'][i], prm['pool_scale'][i])
            yB, ns, nS = rwkv7_mixer(p[..., POOL_DIM:], st_shift[i], st_wkv[i], prm['rwkv_mu'][i], prm['rwkv_w0'][i],
                                     prm['rwkv_w_w2'][i], prm['rwkv_a0'][i], prm['rwkv_a_w2'][i], prm['rwkv_g_w2'][i],
                                     prm['rwkv_k_k'][i], prm['rwkv_k_a'][i], prm['rwkv_r_k'][i],
                                     prm['rwkv_gn_g'][i], prm['rwkv_gn_b'][i])
            y = jnp.concatenate([yA, yB], -1).astype(x.dtype)
            x = x + y @ prm['ev_w_out'][i]
            n_pool.append(nb); n_shift.append(ns); n_wkv.append(nS)
        else:
            h = rmsnorm(x, prm['od_norm_g'][i])
            q = h @ prm['od_w_in'][i]
            yC, vrows = gmlp_mixer(q[..., :2 * GMLP_DIM], prm['gmlp_ln_g'][i], prm['gmlp_ln_b'][i],
                                   prm['gmlp_ws'][i], prm['gmlp_bs'][i])
            yD, nc, nh = rglru_mixer(q[..., 2 * GMLP_DIM:], st_conv[i], st_lru[i], prm['lru_conv_w'][i],
                                     prm['lru_conv_b'][i], prm['lru_wx'][i], prm['lru_bx'][i], prm['lru_wa'][i],
                                     prm['lru_ba'][i], prm['lru_lam'][i])
            y = jnp.concatenate([yC, yD], -1).astype(x.dtype)
            x = x + y @ prm['od_w_out'][i]
            n_conv.append(nc); n_lru.append(nh); n_v.append(vrows)
        hf = rmsnorm(x, prm['ff_norm_g'][layer])
        x = x + jnp.square(jax.nn.relu(hf @ prm['ff_w1'][layer])) @ prm['ff_w2'][layer]
    y = rmsnorm(x, prm['final_norm_g'])
    return y, jnp.stack(n_pool), jnp.stack(n_shift), jnp.stack(n_wkv), jnp.stack(n_conv), jnp.stack(n_lru), jnp.stack(n_v)


def setup_inputs(seed: int = 0) -> dict:
    key = jax.random.key(seed)
    ks = iter(jax.random.split(key, 64))
    f32 = jnp.float32
    nrm = lambda shape, s: s * jax.random.normal(next(ks), shape, f32)
    gain = lambda shape: 1.0 + 0.02 * jax.random.normal(next(ks), shape, f32)
    unif = lambda shape, lo, hi: jax.random.uniform(next(ks), shape, f32, lo, hi)
    a_target = unif((N_ODD, LRU_DIM), 0.9, 0.999)
    s_root = a_target ** (1.0 / LRU_C)
    lru_lam = jnp.log(s_root) - jnp.log1p(-s_root)
    return {
        'x_prompt': nrm((BATCH, SEQ, D_MODEL), 1.0),
        'x_sample': nrm((DEC_BATCH, DEC_SEQ, D_MODEL), 1.0),
        'state_pool': nrm((N_EVEN, DEC_BATCH, POOL_BUF, POOL_DIM), 1.0),
        'state_shift': nrm((N_EVEN, DEC_BATCH, RWKV_PROJ), 1.0),
        'state_wkv': nrm((N_EVEN, DEC_BATCH, RWKV_HEADS, RWKV_HEAD_DIM, RWKV_HEAD_DIM), 1.0),
        'state_conv': nrm((N_ODD, DEC_BATCH, CONV_WIDTH - 1, LRU_DIM), 1.0),
        'state_lru': nrm((N_ODD, DEC_BATCH, LRU_DIM), 0.5),
        'ev_norm_g': gain((N_EVEN, D_MODEL)),
        'ev_w_in': nrm((N_EVEN, D_MODEL, EVEN_PROJ), D_MODEL ** -0.5),
        'pool_w': nrm((N_EVEN, POOL_GROUPS, POOL_GROUP_DIM, POOL_GROUP_DIM), POOL_GROUP_DIM ** -0.5),
        'pool_scale': unif((N_EVEN, POOL_DIM), 0.5, 1.0),
        'rwkv_mu': unif((N_EVEN, RWKV_PROJ), 0.0, 1.0),
        'rwkv_w0': unif((N_EVEN, RWKV_DIM), -6.0, 1.0),
        'rwkv_w_w2': nrm((N_EVEN, DECAY_LORA, RWKV_DIM), 0.5 * DECAY_LORA ** -0.5),
        'rwkv_a0': nrm((N_EVEN, RWKV_DIM), 0.5),
        'rwkv_a_w2': nrm((N_EVEN, AAA_LORA, RWKV_DIM), AAA_LORA ** -0.5),
        'rwkv_g_w2': nrm((N_EVEN, GATE_LORA, RWKV_DIM), GATE_LORA ** -0.5),
        'rwkv_k_k': unif((N_EVEN, RWKV_DIM), 0.7, 1.0),
        'rwkv_k_a': unif((N_EVEN, RWKV_DIM), 0.8, 1.2),
        'rwkv_r_k': nrm((N_EVEN, RWKV_HEADS, RWKV_HEAD_DIM), 0.1),
        'rwkv_gn_g': gain((N_EVEN, RWKV_DIM)),
        'rwkv_gn_b': nrm((N_EVEN, RWKV_DIM), 0.02),
        'ev_w_out': nrm((N_EVEN, EVEN_MIX, D_MODEL), EVEN_MIX ** -0.5),
        'od_norm_g': gain((N_ODD, D_MODEL)),
        'od_w_in': nrm((N_ODD, D_MODEL, ODD_PROJ), D_MODEL ** -0.5),
        'gmlp_ln_g': gain((N_ODD, GMLP_DIM)),
        'gmlp_ln_b': nrm((N_ODD, GMLP_DIM), 0.02),
        'gmlp_ws': nrm((N_ODD, GMLP_HEADS, CHUNK, CHUNK), CHUNK ** -0.5),
        'gmlp_bs': 1.0 + nrm((N_ODD, GMLP_HEADS, CHUNK), 0.01),
        'lru_conv_w': nrm((N_ODD, CONV_WIDTH, LRU_DIM), CONV_WIDTH ** -0.5),
        'lru_conv_b': nrm((N_ODD, LRU_DIM), 0.02),
        'lru_wx': nrm((N_ODD, LRU_BLOCKS, LRU_BLOCK_DIM, LRU_BLOCK_DIM), LRU_BLOCK_DIM ** -0.5),
        'lru_bx': nrm((N_ODD, LRU_DIM), 0.02),
        'lru_wa': nrm((N_ODD, LRU_BLOCKS, LRU_BLOCK_DIM, LRU_BLOCK_DIM), LRU_BLOCK_DIM ** -0.5),
        'lru_ba': nrm((N_ODD, LRU_DIM), 0.02),
        'lru_lam': lru_lam,
        'od_w_out': nrm((N_ODD, ODD_MIX, D_MODEL), ODD_MIX ** -0.5),
        'ff_norm_g': gain((DEPTH, D_MODEL)),
        'ff_w1': nrm((DEPTH, D_MODEL, D_FF), D_MODEL ** -0.5),
        'ff_w2': nrm((DEPTH, D_FF, D_MODEL), D_FF ** -0.5),
        'final_norm_g': gain((D_MODEL,)),
    }


def reference(x_prompt, x_sample, state_pool, state_shift, state_wkv, state_conv, state_lru,
              ev_norm_g, ev_w_in, pool_w, pool_scale, rwkv_mu, rwkv_w0, rwkv_w_w2, rwkv_a0, rwkv_a_w2,
              rwkv_g_w2, rwkv_k_k, rwkv_k_a, rwkv_r_k, rwkv_gn_g, rwkv_gn_b, ev_w_out,
              od_norm_g, od_w_in, gmlp_ln_g, gmlp_ln_b, gmlp_ws, gmlp_bs, lru_conv_w, lru_conv_b,
              lru_wx, lru_bx, lru_wa, lru_ba, lru_lam, od_w_out,
              ff_norm_g, ff_w1, ff_w2, final_norm_g):
    prm = dict(ev_norm_g=ev_norm_g, ev_w_in=ev_w_in, pool_w=pool_w, pool_scale=pool_scale, rwkv_mu=rwkv_mu,
               rwkv_w0=rwkv_w0, rwkv_w_w2=rwkv_w_w2, rwkv_a0=rwkv_a0, rwkv_a_w2=rwkv_a_w2, rwkv_g_w2=rwkv_g_w2,
               rwkv_k_k=rwkv_k_k, rwkv_k_a=rwkv_k_a, rwkv_r_k=rwkv_r_k, rwkv_gn_g=rwkv_gn_g, rwkv_gn_b=rwkv_gn_b,
               ev_w_out=ev_w_out, od_norm_g=od_norm_g, od_w_in=od_w_in, gmlp_ln_g=gmlp_ln_g, gmlp_ln_b=gmlp_ln_b,
               gmlp_ws=gmlp_ws, gmlp_bs=gmlp_bs, lru_conv_w=lru_conv_w, lru_conv_b=lru_conv_b, lru_wx=lru_wx,
               lru_bx=lru_bx, lru_wa=lru_wa, lru_ba=lru_ba, lru_lam=lru_lam, od_w_out=od_w_out,
               ff_norm_g=ff_norm_g, ff_w1=ff_w1, ff_w2=ff_w2, final_norm_g=final_norm_g)
    dt = x_prompt.dtype
    B = x_prompt.shape[0]
    z_pool = jnp.zeros((N_EVEN, B, POOL_BUF, POOL_DIM), dt)
    z_shift = jnp.zeros((N_EVEN, B, RWKV_PROJ), dt)
    z_wkv = jnp.zeros((N_EVEN, B, RWKV_HEADS, RWKV_HEAD_DIM, RWKV_HEAD_DIM), dt)
    z_conv = jnp.zeros((N_ODD, B, CONV_WIDTH - 1, LRU_DIM), dt)
    z_lru = jnp.zeros((N_ODD, B, LRU_DIM), dt)
    y_prompt, p_pool, p_shift, p_wkv, p_conv, p_lru, _ = trunk(
        x_prompt, 0, z_pool, z_shift, z_wkv, z_conv, z_lru, prm)
    y_sample, s_pool, s_shift, s_wkv, s_conv, s_lru, s_gmlp_v = trunk(
        x_sample, PAST_LEN, state_pool, state_shift, state_wkv, state_conv, state_lru, prm)
    return (y_prompt, y_sample, p_pool, p_shift, p_wkv, p_conv, p_lru,
            s_pool, s_shift, s_wkv, s_conv, s_lru, s_gmlp_v)
```

```python
import functools

import jax
import jax.numpy as jnp
from jax import lax
from jax.experimental import pallas as pl
from jax.experimental.pallas import tpu as pltpu

F32 = jnp.float32
BF16 = jnp.bfloat16

D_MODEL = 1024
NORM_EPS = 1e-6
D_FF = 4 * D_MODEL

POOL_WINDOWS = (2, 4, 8, 16)
POOL_GROUP_DIM = 64
POOL_DIM = 256
POOL_BUF = 15

HEAD_DIM = 64
RWKV_DIM = 768
RWKV_HEADS = 12
HEAD_PAIRS = RWKV_HEADS // 2
PAIR_DIM = 2 * HEAD_DIM
RWKV_PROJ = 2560
RWKV_GN_EPS = 64e-5
EVEN_PROJ = POOL_DIM + RWKV_PROJ
LORA_OFF = 3 * RWKV_DIM
GATE_OFF = LORA_OFF + 128

CHUNK = 128
GMLP_DIM = 512
GMLP_HEADS = 4
LN_EPS = 1e-5
LRU_DIM = 512
CONV_WIDTH = 4
LRU_C = 8.0
ODD_PROJ = 2048

WKV_CHUNK = 64
SEG_TILE = 256

VMEM_LIMIT = 48 * 1024 * 1024


def _bdot(a, b):
    return jnp.dot(a.astype(BF16), b.astype(BF16), preferred_element_type=F32)


def _bdot_nt(a, b):
    return lax.dot_general(a.astype(BF16), b.astype(BF16), (((1,), (1,)), ((), ())),
                           preferred_element_type=F32)


def _split3(x):
    hi = x.astype(BF16)
    r1 = x - hi.astype(F32)
    mid = r1.astype(BF16)
    lo = (r1 - mid.astype(F32)).astype(BF16)
    return hi, mid, lo


def _exact_dot_rhs01(x, e):
    hi, mid, lo = _split3(x)
    d = lambda t: jnp.dot(t, e, preferred_element_type=F32)
    return d(hi) + d(mid) + d(lo)


def _exact_dot_lhs01(e, x):
    hi, mid, lo = _split3(x)
    d = lambda t: jnp.dot(e, t, preferred_element_type=F32)
    return d(hi) + d(mid) + d(lo)


def _segsum(x, e_seg):
    parts = [_exact_dot_rhs01(x[:, g * SEG_TILE:(g + 1) * SEG_TILE], e_seg)
             for g in range(RWKV_DIM // SEG_TILE)]
    return jnp.concatenate(parts, axis=1)


def _softplus(z):
    return jnp.maximum(z, 0.0) + jnp.log1p(jnp.exp(-jnp.abs(z)))


def _sigmoid(z):
    return 1.0 / (1.0 + jnp.exp(-z))


def _gelu(z):
    return 0.5 * z * (1.0 + jnp.tanh(0.7978845608028654 * (z + 0.044715 * (z * z * z))))


def _rmsnorm(x, g):
    ms = jnp.mean(x * x, axis=-1, keepdims=True)
    return x * lax.rsqrt(ms + NORM_EPS) * g


def _rwkv_pointwise(P, Pprev, prm):
    (mu, w0, wdec, a0, wa, gw2, k_k, k_a, r_k, e_seg) = prm
    xs = P + (Pprev - P) * mu
    r = xs[:, 0:RWKV_DIM]
    k = xs[:, RWKV_DIM:2 * RWKV_DIM]
    v = xs[:, 2 * RWKV_DIM:3 * RWKV_DIM]
    c_wa = xs[:, LORA_OFF:GATE_OFF]
    cg = xs[:, GATE_OFF:RWKV_PROJ]
    w = -_softplus(-(w0 + _bdot(jnp.tanh(c_wa), wdec))) - 0.5
    ld = -jnp.exp(w)
    a = _sigmoid(a0 + _bdot(c_wa, wa))
    g = _bdot(_sigmoid(cg), gw2)
    kk = k * k_k
    kk = kk / jnp.maximum(jnp.sqrt(_segsum(kk * kk, e_seg)), 1e-12)
    kp = k * (1.0 + (a - 1.0) * k_a)
    bonus = _segsum(r * kp * r_k, e_seg) * v
    return r, kp, v, ld, kk, a, g, bonus


def _rwkv_post(o, bonus, g, gn_g, gn_b, e_seg):
    m = _segsum(o, e_seg) * (1.0 / HEAD_DIM)
    d = o - m
    var = _segsum(d * d, e_seg) * (1.0 / HEAD_DIM)
    on = d * lax.rsqrt(var + RWKV_GN_EPS) * gn_g + gn_b
    return (on + bonus) * g


def _pool_lane_select(s2, s4, s8, s16):
    lane = lax.broadcasted_iota(jnp.int32, (1, POOL_DIM), 1)
    return jnp.where(lane < 64, s2, jnp.where(lane < 128, s4, jnp.where(lane < 192, s8, s16)))


def _pool_window_lanes():
    lane = lax.broadcasted_iota(jnp.int32, (1, POOL_DIM), 1)
    return jnp.where(lane < 64, 2, jnp.where(lane < 128, 4, jnp.where(lane < 192, 8, 16)))


def _lru_gates(xc, wx, bx, wa, ba, lam):
    gx = _sigmoid(_bdot(xc, wx) + bx)
    ga = _sigmoid(_bdot(xc, wa) + ba)
    log_a = -LRU_C * ga * _softplus(-lam)
    a = jnp.exp(log_a)
    b = jnp.sqrt(-jnp.tanh(log_a) * (a * a + 1.0)) * gx * xc
    return a, b


def _gmlp_pre(zq, ln_g, ln_b):
    z = _gelu(zq)
    u = z[:, :GMLP_DIM]
    v = z[:, GMLP_DIM:]
    m = jnp.mean(v, axis=-1, keepdims=True)
    d = v - m
    var = jnp.mean(d * d, axis=-1, keepdims=True)
    return u, d * lax.rsqrt(var + LN_EPS) * ln_g + ln_b


def _inproj_kernel(x_ref, g_ref, w_ref, o_ref):
    h = _rmsnorm(x_ref[...], g_ref[...])
    o_ref[...] = jnp.dot(h.astype(BF16), w_ref[...], preferred_element_type=F32)


def _inproj(x, g, w):
    m, n = x.shape[0], w.shape[1]
    tm = 512
    return pl.pallas_call(
        _inproj_kernel,
        grid=(m // tm,),
        in_specs=[pl.BlockSpec((tm, D_MODEL), lambda i: (i, 0)),
                  pl.BlockSpec((1, D_MODEL), lambda i: (0, 0)),
                  pl.BlockSpec((D_MODEL, n), lambda i: (0, 0))],
        out_specs=pl.BlockSpec((tm, n), lambda i: (i, 0)),
        out_shape=jax.ShapeDtypeStruct((m, n), F32),
        compiler_params=pltpu.CompilerParams(dimension_semantics=("arbitrary",),
                                             vmem_limit_bytes=VMEM_LIMIT),
        name="inproj",
    )(x, g, w)


def _ffn_kernel(x_ref, y_ref, wo_ref, g_ref, w1_ref, w2_ref, gf_ref, o_ref, *, final):
    x1 = x_ref[...] + jnp.dot(y_ref[...], wo_ref[...], preferred_element_type=F32)
    hf = _rmsnorm(x1, g_ref[...]).astype(BF16)
    acc = x1
    fc = 1024
    for c in range(D_FF // fc):
        h = jnp.dot(hf, w1_ref[:, c * fc:(c + 1) * fc], preferred_element_type=F32)
        h = jnp.square(jnp.maximum(h, 0.0)).astype(BF16)
        acc = acc + jnp.dot(h, w2_ref[c * fc:(c + 1) * fc, :], preferred_element_type=F32)
    if final:
        acc = _rmsnorm(acc, gf_ref[...])
    o_ref[...] = acc


def _ffn(x, y, wo, g, w1, w2, gf, final):
    m = x.shape[0]
    tm = 512
    const = lambda i: (0, 0)
    return pl.pallas_call(
        functools.partial(_ffn_kernel, final=final),
        grid=(m // tm,),
        in_specs=[pl.BlockSpec((tm, D_MODEL), lambda i: (i, 0)),
                  pl.BlockSpec((tm, D_MODEL), lambda i: (i, 0)),
                  pl.BlockSpec((D_MODEL, D_MODEL), const, pipeline_mode=pl.Buffered(1)),
                  pl.BlockSpec((1, D_MODEL), const),
                  pl.BlockSpec((D_MODEL, D_FF), const, pipeline_mode=pl.Buffered(1)),
                  pl.BlockSpec((D_FF, D_MODEL), const, pipeline_mode=pl.Buffered(1)),
                  pl.BlockSpec((1, D_MODEL), const)],
        out_specs=pl.BlockSpec((tm, D_MODEL), lambda i: (i, 0)),
        out_shape=jax.ShapeDtypeStruct((m, D_MODEL), F32),
        compiler_params=pltpu.CompilerParams(dimension_semantics=("arbitrary",),
                                             vmem_limit_bytes=VMEM_LIMIT),
        name="ffn",
    )(x, y, wo, g, w1, w2, gf)


def _even_prompt_kernel(p_ref, stp_ref, sts_ref, stw_ref,
                        mu_ref, w0_ref, wdec_ref, a0_ref, wa_ref, gw2_ref, kk_ref, ka_ref, rk_ref,
                        gng_ref, gnb_ref, eseg_ref, poolw_ref, pools_ref, tri_ref,
                        y_ref, opool_ref, oshift_ref, owkv_ref,
                        hpool, hshift, S, r_s, kp_s, v_s, ld_s, kk_s, a_s, o_s, *, tt, start):
    i = pl.program_id(1)
    nt = pl.num_programs(1)
    C = WKV_CHUNK

    @pl.when(i == 0)
    def _init():
        hpool[0:1, :] = jnp.zeros((1, POOL_DIM), F32)
        hpool[1:16, :] = stp_ref[0]
        hshift[...] = sts_ref[0]
        S[...] = jnp.zeros(S.shape, F32)
        for j in range(HEAD_PAIRS):
            S[j, 0:HEAD_DIM, 0:HEAD_DIM] = stw_ref[0, 2 * j]
            S[j, HEAD_DIM:PAIR_DIM, HEAD_DIM:PAIR_DIM] = stw_ref[0, 2 * j + 1]

    p = p_ref[0]
    rows = lax.broadcasted_iota(jnp.int32, (tt, 1), 0)

    u = p[:, 0:POOL_DIM]
    ext = jnp.concatenate([hpool[...], u], axis=0)
    s2 = ext + pltpu.roll(ext, 1, 0)
    s4 = s2 + pltpu.roll(s2, 2, 0)
    s8 = s4 + pltpu.roll(s4, 4, 0)
    s16 = s8 + pltpu.roll(s8, 8, 0)
    sel = _pool_lane_select(s2, s4, s8, s16)[16:, :]
    pos = start + i * tt + rows
    cnt = jnp.minimum(_pool_window_lanes(), pos + 1).astype(F32)
    d = sel / cnt - u
    y_ref[0, :, 0:POOL_DIM] = (_bdot(d, poolw_ref[...]) * pools_ref[...]).astype(BF16)
    hpool[...] = ext[tt:tt + 16, :]

    P = p[:, POOL_DIM:EVEN_PROJ]
    Pprev = jnp.where(rows == 0, hshift[...], pltpu.roll(P, 1, 0))
    hshift[...] = P[tt - 1:tt, :]
    e_seg = eseg_ref[...]
    prm = (mu_ref[...], w0_ref[...], wdec_ref[...], a0_ref[...], wa_ref[...], gw2_ref[...],
           kk_ref[...], ka_ref[...], rk_ref[...], e_seg)
    r, kp, v, ld, kk, a, g, bonus = _rwkv_pointwise(P, Pprev, prm)
    r_s[...] = r
    kp_s[...] = kp
    v_s[...] = v
    ld_s[...] = ld
    kk_s[...] = kk
    a_s[...] = a

    rr = lax.broadcasted_iota(jnp.int32, (PAIR_DIM, PAIR_DIM), 0)
    cc = lax.broadcasted_iota(jnp.int32, (PAIR_DIM, PAIR_DIM), 1)
    same_head = (rr >= C) == (cc >= C)
    strict = same_head & (cc < rr)
    incl = same_head & (cc <= rr)
    eye = (rr == cc).astype(F32)
    lane_c = lax.broadcasted_iota(jnp.int32, (C, PAIR_DIM), 1)
    head0 = lane_c < HEAD_DIM
    tri = tri_ref[...]

    def chunk(c, carry):
        sl = pl.ds(pl.multiple_of(c * C, C), C)
        R = r_s[sl, :]
        K = kp_s[sl, :]
        V = v_s[sl, :]
        LD = ld_s[sl, :]
        KK = kk_s[sl, :]
        KA = KK * a_s[sl, :]
        L = _exact_dot_lhs01(tri, LD)
        Lend = L[C - 1:C, :]
        enL = jnp.exp(-L)
        eE = jnp.exp(Lend - L)
        Qr = R * jnp.exp(L)
        Qa = KK * jnp.exp(L - LD)
        Kt = K * enL
        Bt = KA * enL
        Kend = K * eE
        Bend = KA * eE
        Pend = jnp.exp(Lend)
        for j in range(HEAD_PAIRS):
            ls = slice(j * PAIR_DIM, (j + 1) * PAIR_DIM)
            qa, qr, vj = Qa[:, ls], Qr[:, ls], V[:, ls]
            z = jnp.zeros_like(qa)
            lhs = jnp.concatenate([jnp.where(head0, qa, z), jnp.where(head0, z, qa),
                                   jnp.where(head0, qr, z), jnp.where(head0, z, qr)], axis=0).astype(BF16)
            bt = Bt[:, ls].astype(BF16)
            kt = Kt[:, ls].astype(BF16)
            G = _bdot_nt(lhs, jnp.concatenate([bt, bt, kt, kt], axis=0))
            Aab = jnp.where(strict, G[0:2 * C, 0:2 * C], 0.0)
            Aak = jnp.where(strict, G[0:2 * C, 2 * C:4 * C], 0.0)
            Arb = jnp.where(incl, G[2 * C:4 * C, 0:2 * C], 0.0)
            Ark = jnp.where(incl, G[2 * C:4 * C, 2 * C:4 * C], 0.0)
            X = Aab
            Tm = eye - Aab
            n_sq = 1
            while (1 << n_sq) < C:
                X = _bdot(X, X)
                Tm = Tm + _bdot(Tm, X)
                n_sq += 1
            Sj = S[j]
            QS = _bdot_nt(lhs, Sj)
            vsm = jnp.concatenate([jnp.where(head0, vj, z), jnp.where(head0, z, vj)], axis=0)
            AV = _bdot(jnp.concatenate([Aak, Ark], axis=0), vsm)
            U = _bdot(Tm, QS[0:2 * C] + AV[0:2 * C])
            O = QS[2 * C:4 * C] + AV[2 * C:4 * C] - _bdot(Arb, U)
            u_pair = U[0:C] + U[C:2 * C]
            o_s[sl, ls] = O[0:C] + O[C:2 * C]
            vu_t = jnp.transpose(jnp.concatenate([vj, -u_pair], axis=0))
            upd = _bdot(vu_t, jnp.concatenate([Kend[:, ls], Bend[:, ls]], axis=0))
            S[j] = Pend[:, ls] * Sj + jnp.where(same_head, upd, 0.0)
        return carry

    lax.fori_loop(0, tt // C, chunk, 0)

    yb = _rwkv_post(o_s[...], bonus, g, gng_ref[...], gnb_ref[...], e_seg)
    y_ref[0, :, POOL_DIM:D_MODEL] = yb.astype(BF16)

    @pl.when(i == nt - 1)
    def _fin():
        opool_ref[0] = hpool[1:16, :]
        oshift_ref[0] = hshift[...]
        for j in range(HEAD_PAIRS):
            owkv_ref[0, 2 * j] = S[j, 0:HEAD_DIM, 0:HEAD_DIM]
            owkv_ref[0, 2 * j + 1] = S[j, HEAD_DIM:PAIR_DIM, HEAD_DIM:PAIR_DIM]


def _even_prompt(p, st_pool, st_shift, st_wkv, prm, start):
    B, T, _ = p.shape
    tt = 256
    bt = lambda b, i: (b, i, 0)
    bs3 = lambda b, i: (b, 0, 0)
    bs4 = lambda b, i: (b, 0, 0, 0)
    c2 = lambda b, i: (0, 0)
    vec = lambda n: pl.BlockSpec((1, n), c2)
    scr = lambda: pltpu.VMEM((tt, RWKV_DIM), F32)
    return pl.pallas_call(
        functools.partial(_even_prompt_kernel, tt=tt, start=start),
        grid=(B, T // tt),
        in_specs=[pl.BlockSpec((1, tt, EVEN_PROJ), bt),
                  pl.BlockSpec((1, POOL_BUF, POOL_DIM), bs3),
                  pl.BlockSpec((1, 1, RWKV_PROJ), bs3),
                  pl.BlockSpec((1, RWKV_HEADS, HEAD_DIM, HEAD_DIM), bs4),
                  vec(RWKV_PROJ), vec(RWKV_DIM), pl.BlockSpec((128, RWKV_DIM), c2), vec(RWKV_DIM),
                  pl.BlockSpec((128, RWKV_DIM), c2), pl.BlockSpec((128, RWKV_DIM), c2),
                  vec(RWKV_DIM), vec(RWKV_DIM), vec(RWKV_DIM), vec(RWKV_DIM), vec(RWKV_DIM),
                  pl.BlockSpec((SEG_TILE, SEG_TILE), c2), pl.BlockSpec((POOL_DIM, POOL_DIM), c2),
                  vec(POOL_DIM), pl.BlockSpec((WKV_CHUNK, WKV_CHUNK), c2)],
        out_specs=[pl.BlockSpec((1, tt, D_MODEL), bt),
                   pl.BlockSpec((1, POOL_BUF, POOL_DIM), bs3),
                   pl.BlockSpec((1, 1, RWKV_PROJ), bs3),
                   pl.BlockSpec((1, RWKV_HEADS, HEAD_DIM, HEAD_DIM), bs4)],
        out_shape=[jax.ShapeDtypeStruct((B, T, D_MODEL), BF16),
                   jax.ShapeDtypeStruct((B, POOL_BUF, POOL_DIM), F32),
                   jax.ShapeDtypeStruct((B, 1, RWKV_PROJ), F32),
                   jax.ShapeDtypeStruct((B, RWKV_HEADS, HEAD_DIM, HEAD_DIM), F32)],
        scratch_shapes=[pltpu.VMEM((16, POOL_DIM), F32), pltpu.VMEM((1, RWKV_PROJ), F32),
                        pltpu.VMEM((HEAD_PAIRS, PAIR_DIM, PAIR_DIM), F32),
                        scr(), scr(), scr(), scr(), scr(), scr(), scr()],
        compiler_params=pltpu.CompilerParams(dimension_semantics=("arbitrary", "arbitrary"),
                                             vmem_limit_bytes=VMEM_LIMIT),
        name="even_prompt",
    )(p, st_pool, st_shift, st_wkv, *prm)


def _odd_prompt_kernel(q_ref, stc_ref, stl_ref,
                       lng_ref, lnb_ref, ws_ref, bias_ref, cw_ref, cb_ref, wx_ref, bx_ref, wa_ref, ba_ref,
                       lam_ref, y_ref, oconv_ref, olru_ref, hconv, hl, mix_s, *, tt):
    i = pl.program_id(1)
    nt = pl.num_programs(1)

    @pl.when(i == 0)
    def _init():
        hconv[0:5, :] = jnp.zeros((5, LRU_DIM), F32)
        hconv[5:8, :] = stc_ref[0]
        hl[...] = stl_ref[0]

    q = q_ref[0]
    rows = lax.broadcasted_iota(jnp.int32, (tt, 1), 0)

    u, vn = _gmlp_pre(q[:, 0:2 * GMLP_DIM], lng_ref[...], lnb_ref[...])
    rr = lax.broadcasted_iota(jnp.int32, (CHUNK, CHUNK), 0)
    cc = lax.broadcasted_iota(jnp.int32, (CHUNK, CHUNK), 1)
    causal = cc <= rr
    for h in range(GMLP_HEADS):
        wm = jnp.where(causal, ws_ref[h], 0.0).astype(BF16)
        ls = slice(h * CHUNK, (h + 1) * CHUNK)
        for c in range(tt // CHUNK):
            rs = slice(c * CHUNK, (c + 1) * CHUNK)
            mix_s[rs, ls] = jnp.dot(wm, vn[rs, ls].astype(BF16), preferred_element_type=F32) + bias_ref[:, ls]
    y_ref[0, :, 0:GMLP_DIM] = (u * mix_s[...]).astype(BF16)

    gate_in = q[:, 2 * GMLP_DIM:2 * GMLP_DIM + LRU_DIM]
    xr = q[:, 2 * GMLP_DIM + LRU_DIM:ODD_PROJ]
    ext = jnp.concatenate([hconv[...], xr], axis=0)
    xc = ext[8:, :] * cw_ref[3:4, :] + cb_ref[...]
    for j in range(1, CONV_WIDTH):
        xc = xc + pltpu.roll(ext, j, 0)[8:, :] * cw_ref[3 - j:4 - j, :]
    hconv[...] = ext[tt:tt + 8, :]
    a, b = _lru_gates(xc, wx_ref[...], bx_ref[...], wa_ref[...], ba_ref[...], lam_ref[...])
    dist = 1
    while dist < tt:
        keep = rows >= dist
        a_sh = jnp.where(keep, pltpu.roll(a, dist, 0), 1.0)
        b_sh = jnp.where(keep, pltpu.roll(b, dist, 0), 0.0)
        b = a * b_sh + b
        a = a * a_sh
        dist *= 2
    h = a * hl[...] + b
    hl[...] = h[tt - 1:tt, :]
    y_ref[0, :, GMLP_DIM:D_MODEL] = (h * _gelu(gate_in)).astype(BF16)

    @pl.when(i == nt - 1)
    def _fin():
        oconv_ref[0] = hconv[5:8, :]
        olru_ref[0] = hl[...]


def _odd_prompt(q, st_conv, st_lru, prm):
    B, T, _ = q.shape
    tt = 256
    bt = lambda b, i: (b, i, 0)
    bs3 = lambda b, i: (b, 0, 0)
    c2 = lambda b, i: (0, 0)
    c3 = lambda b, i: (0, 0, 0)
    vec = lambda n: pl.BlockSpec((1, n), c2)
    return pl.pallas_call(
        functools.partial(_odd_prompt_kernel, tt=tt),
        grid=(B, T // tt),
        in_specs=[pl.BlockSpec((1, tt, ODD_PROJ), bt),
                  pl.BlockSpec((1, CONV_WIDTH - 1, LRU_DIM), bs3),
                  pl.BlockSpec((1, 1, LRU_DIM), bs3),
                  vec(GMLP_DIM), vec(GMLP_DIM),
                  pl.BlockSpec((GMLP_HEADS, CHUNK, CHUNK), c3),
                  pl.BlockSpec((CHUNK, GMLP_DIM), c2),
                  pl.BlockSpec((CONV_WIDTH, LRU_DIM), c2), vec(LRU_DIM),
                  pl.BlockSpec((LRU_DIM, LRU_DIM), c2), vec(LRU_DIM),
                  pl.BlockSpec((LRU_DIM, LRU_DIM), c2), vec(LRU_DIM), vec(LRU_DIM)],
        out_specs=[pl.BlockSpec((1, tt, D_MODEL), bt),
                   pl.BlockSpec((1, CONV_WIDTH - 1, LRU_DIM), bs3),
                   pl.BlockSpec((1, 1, LRU_DIM), bs3)],
        out_shape=[jax.ShapeDtypeStruct((B, T, D_MODEL), BF16),
                   jax.ShapeDtypeStruct((B, CONV_WIDTH - 1, LRU_DIM), F32),
                   jax.ShapeDtypeStruct((B, 1, LRU_DIM), F32)],
        scratch_shapes=[pltpu.VMEM((8, LRU_DIM), F32), pltpu.VMEM((1, LRU_DIM), F32),
                        pltpu.VMEM((tt, GMLP_DIM), F32)],
        compiler_params=pltpu.CompilerParams(dimension_semantics=("arbitrary", "arbitrary"),
                                             vmem_limit_bytes=VMEM_LIMIT),
        name="odd_prompt",
    )(q, st_conv, st_lru, *prm)


def _even_sample_pre_kernel(p_ref, stp_ref, sts_ref,
                            mu_ref, w0_ref, wdec_ref, a0_ref, wa_ref, gw2_ref, kk_ref, ka_ref, rk_ref,
                            eseg_ref, poolw_ref, pools_ref,
                            r_ref, w_ref, kkn_ref, kka_ref, kp_ref, v_ref, g_ref, bonus_ref, ya_ref,
                            opool_ref, oshift_ref, *, T, start):
    prm = (mu_ref[...], w0_ref[...], wdec_ref[...], a0_ref[...], wa_ref[...], gw2_ref[...],
           kk_ref[...], ka_ref[...], rk_ref[...], eseg_ref[...])
    full = [stp_ref[s] for s in range(POOL_BUF)] + [p_ref[t][:, 0:POOL_DIM] for t in range(T)]
    wl = _pool_window_lanes()
    for t in range(T):
        P = p_ref[t][:, POOL_DIM:EVEN_PROJ]
        Pprev = sts_ref[...] if t == 0 else p_ref[t - 1][:, POOL_DIM:EVEN_PROJ]
        r, kp, v, ld, kk, a, g, bonus = _rwkv_pointwise(P, Pprev, prm)
        r_ref[t] = r
        w_ref[t] = jnp.exp(ld)
        kkn_ref[t] = kk
        kka_ref[t] = kk * a
        kp_ref[t] = kp
        v_ref[t] = v
        g_ref[t] = g
        bonus_ref[t] = bonus
        e = POOL_BUF + t
        s2 = full[e] + full[e - 1]
        s4 = s2 + full[e - 2] + full[e - 3]
        s8 = s4 + full[e - 4] + full[e - 5] + full[e - 6] + full[e - 7]
        s16 = s8
        for s in range(8, 16):
            s16 = s16 + full[e - s]
        sel = _pool_lane_select(s2, s4, s8, s16)
        cnt = jnp.minimum(wl, start + t + 1).astype(F32)
        d = sel / cnt - full[e]
        ya_ref[t] = _bdot(d, poolw_ref[...]) * pools_ref[...]
    for s in range(POOL_BUF):
        opool_ref[s] = full[T + s]
    oshift_ref[...] = p_ref[T - 1][:, POOL_DIM:EVEN_PROJ]


def _even_sample_pre(p, st_pool, st_shift, prm, start):
    T, B, _ = p.shape
    n_out = 8
    big = jax.ShapeDtypeStruct((T, B, RWKV_DIM), F32)
    return pl.pallas_call(
        functools.partial(_even_sample_pre_kernel, T=T, start=start),
        out_shape=[big] * n_out + [jax.ShapeDtypeStruct((T, B, POOL_DIM), F32),
                                   jax.ShapeDtypeStruct((POOL_BUF, B, POOL_DIM), F32),
                                   jax.ShapeDtypeStruct((B, RWKV_PROJ), F32)],
        compiler_params=pltpu.CompilerParams(vmem_limit_bytes=VMEM_LIMIT),
        name="even_sample_pre",
    )(p, st_pool, st_shift, *prm)


def _wkv_sample_kernel(r_ref, w_ref, kk_ref, kka_ref, kp_ref, v_ref, s_ref, o_ref, so_ref, *, T, nb):
    rr = lax.broadcasted_iota(jnp.int32, (HEAD_DIM, HEAD_DIM), 0)
    cc = lax.broadcasted_iota(jnp.int32, (HEAD_DIM, HEAD_DIM), 1)
    eye = (rr == cc).astype(F32)

    def body(i, carry):
        row = pl.ds(i, 1)
        S = s_ref[i]
        for t in range(T):
            sk = jnp.sum(S * kk_ref[t, row, :], axis=1, keepdims=True)
            vcol = jnp.sum(eye * v_ref[t, row, :], axis=1, keepdims=True)
            S = S * w_ref[t, row, :] - sk * kka_ref[t, row, :] + vcol * kp_ref[t, row, :]
            ocol = jnp.sum(S * r_ref[t, row, :], axis=1, keepdims=True)
            o_ref[t, row, :] = jnp.sum(eye * ocol, axis=0, keepdims=True)
        so_ref[i] = S
        return carry

    lax.fori_loop(0, nb, body, 0)


def _wkv_sample(r, w, kk, kka, kp, v, s):
    T, R, _ = r.shape
    nb = 128
    row_spec = pl.BlockSpec((T, nb, HEAD_DIM), lambda i: (0, i, 0))
    st_spec = pl.BlockSpec((nb, HEAD_DIM, HEAD_DIM), lambda i: (i, 0, 0))
    return pl.pallas_call(
        functools.partial(_wkv_sample_kernel, T=T, nb=nb),
        grid=(R // nb,),
        in_specs=[row_spec] * 6 + [st_spec],
        out_specs=[row_spec, st_spec],
        out_shape=[jax.ShapeDtypeStruct((T, R, HEAD_DIM), F32),
                   jax.ShapeDtypeStruct((R, HEAD_DIM, HEAD_DIM), F32)],
        compiler_params=pltpu.CompilerParams(dimension_semantics=("arbitrary",),
                                             vmem_limit_bytes=VMEM_LIMIT),
        name="wkv_sample",
    )(r, w, kk, kka, kp, v, s)


def _even_sample_post_kernel(o_ref, bonus_ref, g_ref, ya_ref, gng_ref, gnb_ref, eseg_ref, y_ref, *, T):
    for t in range(T):
        yb = _rwkv_post(o_ref[t], bonus_ref[t], g_ref[t], gng_ref[...], gnb_ref[...], eseg_ref[...])
        y_ref[t, :, 0:POOL_DIM] = ya_ref[t].astype(BF16)
        y_ref[t, :, POOL_DIM:D_MODEL] = yb.astype(BF16)


def _even_sample_post(o, bonus, g, ya, gn_g, gn_b, e_seg):
    T, B, _ = o.shape
    return pl.pallas_call(
        functools.partial(_even_sample_post_kernel, T=T),
        out_shape=jax.ShapeDtypeStruct((T, B, D_MODEL), BF16),
        compiler_params=pltpu.CompilerParams(vmem_limit_bytes=VMEM_LIMIT),
        name="even_sample_post",
    )(o, bonus, g, ya, gn_g, gn_b, e_seg)


def _odd_sample_kernel(q_ref, stc_ref, stl_ref,
                       lng_ref, lnb_ref, wsm_ref, bsm_ref, cw_ref, cb_ref, wx_ref, bx_ref, wa_ref, ba_ref,
                       lam_ref, y_ref, v_ref, oconv_ref, olru_ref, *, T):
    vns = []
    us = []
    for t in range(T):
        u, vn = _gmlp_pre(q_ref[t][:, 0:2 * GMLP_DIM], lng_ref[...], lnb_ref[...])
        us.append(u)
        vns.append(vn)
        v_ref[t] = vn
    full = [stc_ref[s] for s in range(CONV_WIDTH - 1)] + \
           [q_ref[t][:, 2 * GMLP_DIM + LRU_DIM:ODD_PROJ] for t in range(T)]
    h = stl_ref[...]
    for t in range(T):
        mix = bsm_ref[t:t + 1, :]
        for j in range(t + 1):
            mix = mix + wsm_ref[t * T + j:t * T + j + 1, :] * vns[j]
        y_ref[t, :, 0:GMLP_DIM] = (us[t] * mix).astype(BF16)
        xc = full[t + CONV_WIDTH - 1] * cw_ref[CONV_WIDTH - 1:CONV_WIDTH, :] + cb_ref[...]
        for j in range(CONV_WIDTH - 1):
            xc = xc + full[t + j] * cw_ref[j:j + 1, :]
        a, b = _lru_gates(xc, wx_ref[...], bx_ref[...], wa_ref[...], ba_ref[...], lam_ref[...])
        h = a * h + b
        gate_in = q_ref[t][:, 2 * GMLP_DIM:2 * GMLP_DIM + LRU_DIM]
        y_ref[t, :, GMLP_DIM:D_MODEL] = (h * _gelu(gate_in)).astype(BF16)
    for s in range(CONV_WIDTH - 1):
        oconv_ref[s] = full[T + s]
    olru_ref[...] = h


def _odd_sample(q, st_conv, st_lru, prm):
    T, B, _ = q.shape
    return pl.pallas_call(
        functools.partial(_odd_sample_kernel, T=T),
        out_shape=[jax.ShapeDtypeStruct((T, B, D_MODEL), BF16),
                   jax.ShapeDtypeStruct((T, B, GMLP_DIM), F32),
                   jax.ShapeDtypeStruct((CONV_WIDTH - 1, B, LRU_DIM), F32),
                   jax.ShapeDtypeStruct((B, LRU_DIM), F32)],
        compiler_params=pltpu.CompilerParams(vmem_limit_bytes=VMEM_LIMIT),
        name="odd_sample",
    )(q, st_conv, st_lru, *prm)


def _block_diag(w):
    n, c, d = w.shape
    eye = jnp.eye(n, dtype=w.dtype)
    return (eye[:, None, :, None] * w[:, :, None, :]).reshape(n * c, n * d)


def _row(x):
    return x.reshape(1, -1)


def kernel(x_prompt, x_sample, state_pool, state_shift, state_wkv, state_conv, state_lru, ev_norm_g, ev_w_in, pool_w, pool_scale, rwkv_mu, rwkv_w0, rwkv_w_w2, rwkv_a0, rwkv_a_w2, rwkv_g_w2, rwkv_k_k, rwkv_k_a, rwkv_r_k, rwkv_gn_g, rwkv_gn_b, ev_w_out, od_norm_g, od_w_in, gmlp_ln_g, gmlp_ln_b, gmlp_ws, gmlp_bs, lru_conv_w, lru_conv_b, lru_wx, lru_bx, lru_wa, lru_ba, lru_lam, od_w_out, ff_norm_g, ff_w1, ff_w2, final_norm_g):
    B, T, _ = x_prompt.shape
    DB, DT, _ = x_sample.shape
    past_len = 16384

    seg_ids = jnp.arange(SEG_TILE) // HEAD_DIM
    e_seg = (seg_ids[:, None] == seg_ids[None, :]).astype(BF16)
    tri = (jnp.arange(WKV_CHUNK)[None, :] <= jnp.arange(WKV_CHUNK)[:, None]).astype(BF16)
    zlora = jnp.zeros((64, RWKV_DIM), F32)

    ev_common = (_row(rwkv_mu[0]), _row(rwkv_w0[0]),
                 jnp.concatenate([rwkv_w_w2[0], zlora], 0).astype(BF16), _row(rwkv_a0[0]),
                 jnp.concatenate([zlora, rwkv_a_w2[0]], 0).astype(BF16), rwkv_g_w2[0].astype(BF16),
                 _row(rwkv_k_k[0]), _row(rwkv_k_a[0]), _row(rwkv_r_k[0]))
    gn_g, gn_b = _row(rwkv_gn_g[0]), _row(rwkv_gn_b[0])
    pool_bd = _block_diag(pool_w[0]).astype(BF16)
    pool_sc = _row(pool_scale[0])
    w_in0 = ev_w_in[0].astype(BF16)
    g_in0 = _row(ev_norm_g[0])

    xp = x_prompt.reshape(B * T, D_MODEL)
    xs = jnp.transpose(x_sample, (1, 0, 2)).reshape(DT * DB, D_MODEL)

    pp = _inproj(xp, g_in0, w_in0).reshape(B, T, EVEN_PROJ)
    ps = _inproj(xs, g_in0, w_in0).reshape(DT, DB, EVEN_PROJ)

    yp, p_pool, p_shift, p_wkv = _even_prompt(
        pp, jnp.zeros((B, POOL_BUF, POOL_DIM), F32), jnp.zeros((B, 1, RWKV_PROJ), F32),
        jnp.zeros((B, RWKV_HEADS, HEAD_DIM, HEAD_DIM), F32),
        ev_common + (gn_g, gn_b, e_seg, pool_bd, pool_sc, tri), 0)

    pre = _even_sample_pre(ps, jnp.transpose(state_pool[0], (1, 0, 2)), state_shift[0],
                           ev_common + (e_seg, pool_bd, pool_sc), past_len)
    r_s, w_s, kk_s, kka_s, kp_s, v_s, g_s, bonus_s, ya_s, s_pool_tm, s_shift = pre
    hm = lambda t: t.reshape(DT, DB * RWKV_HEADS, HEAD_DIM)
    o_s, s_wkv = _wkv_sample(hm(r_s), hm(w_s), hm(kk_s), hm(kka_s), hm(kp_s), hm(v_s),
                             state_wkv[0].reshape(DB * RWKV_HEADS, HEAD_DIM, HEAD_DIM))
    ys = _even_sample_post(o_s.reshape(DT, DB, RWKV_DIM), bonus_s, g_s, ya_s, gn_g, gn_b, e_seg)

    w_out0 = ev_w_out[0].astype(BF16)
    ffg = lambda l: _row(ff_norm_g[l])
    gfin = _row(final_norm_g)
    xp = _ffn(xp, yp.reshape(B * T, D_MODEL), w_out0, ffg(0), ff_w1[0].astype(BF16), ff_w2[0].astype(BF16),
              gfin, False)
    xs = _ffn(xs, ys.reshape(DT * DB, D_MODEL), w_out0, ffg(0), ff_w1[0].astype(BF16), ff_w2[0].astype(BF16),
              gfin, False)

    w_in1 = od_w_in[0].astype(BF16)
    g_in1 = _row(od_norm_g[0])
    qp = _inproj(xp, g_in1, w_in1).reshape(B, T, ODD_PROJ)
    qs = _inproj(xs, g_in1, w_in1).reshape(DT, DB, ODD_PROJ)

    lru_common = (lru_conv_w[0], _row(lru_conv_b[0]), _block_diag(lru_wx[0]).astype(BF16), _row(lru_bx[0]),
                  _block_diag(lru_wa[0]).astype(BF16), _row(lru_ba[0]), _row(lru_lam[0]))
    ln = (_row(gmlp_ln_g[0]), _row(gmlp_ln_b[0]))
    bias_full = jnp.repeat(jnp.transpose(gmlp_bs[0]), CHUNK, axis=1)
    yp, p_conv, p_lru = _odd_prompt(
        qp, jnp.zeros((B, CONV_WIDTH - 1, LRU_DIM), F32), jnp.zeros((B, 1, LRU_DIM), F32),
        ln + (gmlp_ws[0], bias_full) + lru_common)

    ws_small = jnp.repeat(jnp.transpose(gmlp_ws[0][:, :DT, :DT], (1, 2, 0)).reshape(DT * DT, GMLP_HEADS),
                          CHUNK, axis=1)
    ys, s_v, s_conv_tm, s_lru = _odd_sample(
        qs, jnp.transpose(state_conv[0], (1, 0, 2)), state_lru[0],
        ln + (ws_small, bias_full[:DT]) + lru_common)

    w_out1 = od_w_out[0].astype(BF16)
    xp = _ffn(xp, yp.reshape(B * T, D_MODEL), w_out1, ffg(1), ff_w1[1].astype(BF16), ff_w2[1].astype(BF16),
              gfin, True)
    xs = _ffn(xs, ys.reshape(DT * DB, D_MODEL), w_out1, ffg(1), ff_w1[1].astype(BF16), ff_w2[1].astype(BF16),
              gfin, True)

    tm2bm = lambda t: jnp.transpose(t, (1, 0, 2))
    y_prompt = xp.reshape(B, T, D_MODEL)
    y_sample = tm2bm(xs.reshape(DT, DB, D_MODEL))
    return (y_prompt, y_sample,
            p_pool[None], p_shift.reshape(1, B, RWKV_PROJ), p_wkv[None],
            p_conv[None], p_lru.reshape(1, B, LRU_DIM),
            tm2bm(s_pool_tm)[None], s_shift[None],
            s_wkv.reshape(1, DB, RWKV_HEADS, HEAD_DIM, HEAD_DIM),
            tm2bm(s_conv_tm)[None], s_lru[None], tm2bm(s_v)[None])
```

```python
import functools

import jax
import jax.numpy as jnp
from jax import lax
from jax.experimental import pallas as pl
from jax.experimental.pallas import tpu as pltpu

F32 = jnp.float32
BF16 = jnp.bfloat16

D_MODEL = 1024
NORM_EPS = 1e-6
D_FF = 4 * D_MODEL

POOL_WINDOWS = (2, 4, 8, 16)
POOL_GROUP_DIM = 64
POOL_DIM = 256
POOL_BUF = 15

HEAD_DIM = 64
RWKV_DIM = 768
RWKV_HEADS = 12
HEAD_PAIRS = RWKV_HEADS // 2
PAIR_DIM = 2 * HEAD_DIM
RWKV_PROJ = 2560
RWKV_GN_EPS = 64e-5
EVEN_PROJ = POOL_DIM + RWKV_PROJ
LORA_OFF = 3 * RWKV_DIM
GATE_OFF = LORA_OFF + 128

CHUNK = 128
GMLP_DIM = 512
GMLP_HEADS = 4
LN_EPS = 1e-5
LRU_DIM = 512
CONV_WIDTH = 4
LRU_C = 8.0
ODD_PROJ = 2048

WKV_CHUNK = 64
SEG_TILE = 256
WKV_SAMPLE_GROUP = 8

VMEM_LIMIT = 48 * 1024 * 1024


def _bdot(a, b):
    return jnp.dot(a.astype(BF16), b.astype(BF16), preferred_element_type=F32)


def _bdot_nt(a, b):
    return lax.dot_general(a.astype(BF16), b.astype(BF16), (((1,), (1,)), ((), ())),
                           preferred_element_type=F32)


def _split3(x):
    hi = x.astype(BF16)
    r1 = x - hi.astype(F32)
    mid = r1.astype(BF16)
    lo = (r1 - mid.astype(F32)).astype(BF16)
    return hi, mid, lo


def _exact_dot_rhs01(x, e):
    hi, mid, lo = _split3(x)
    d = lambda t: jnp.dot(t, e, preferred_element_type=F32)
    return d(hi) + d(mid) + d(lo)


def _exact_dot_lhs01(e, x):
    hi, mid, lo = _split3(x)
    d = lambda t: jnp.dot(e, t, preferred_element_type=F32)
    return d(hi) + d(mid) + d(lo)


def _segsum(x, e_seg):
    parts = [_exact_dot_rhs01(x[:, g * SEG_TILE:(g + 1) * SEG_TILE], e_seg)
             for g in range(RWKV_DIM // SEG_TILE)]
    return jnp.concatenate(parts, axis=1)


def _softplus(z):
    return jnp.maximum(z, 0.0) + jnp.log(1.0 + jnp.exp(-jnp.abs(z)))


def _sigmoid(z):
    return 1.0 / (1.0 + jnp.exp(-z))


def _gelu(z):
    return 0.5 * z * (1.0 + jnp.tanh(0.7978845608028654 * (z + 0.044715 * (z * z * z))))


def _rmsnorm(x, g):
    ms = jnp.mean(x * x, axis=-1, keepdims=True)
    return x * lax.rsqrt(ms + NORM_EPS) * g


def _rwkv_pointwise(P, Pprev, prm):
    (mu, w0, wdec, a0, wa, gw2, k_k, k_a, r_k, e_seg) = prm
    xs = P + (Pprev - P) * mu
    r = xs[:, 0:RWKV_DIM]
    k = xs[:, RWKV_DIM:2 * RWKV_DIM]
    v = xs[:, 2 * RWKV_DIM:3 * RWKV_DIM]
    c_wa = xs[:, LORA_OFF:GATE_OFF]
    cg = xs[:, GATE_OFF:RWKV_PROJ]
    w = -_softplus(-(w0 + _bdot(jnp.tanh(c_wa), wdec))) - 0.5
    ld = -jnp.exp(w)
    a = _sigmoid(a0 + _bdot(c_wa, wa))
    g = _bdot(_sigmoid(cg), gw2)
    kk = k * k_k
    kk = kk / jnp.maximum(jnp.sqrt(_segsum(kk * kk, e_seg)), 1e-12)
    kp = k * (1.0 + (a - 1.0) * k_a)
    bonus = _segsum(r * kp * r_k, e_seg) * v
    return r, kp, v, ld, kk, a, g, bonus


def _rwkv_post(o, bonus, g, gn_g, gn_b, e_seg):
    m = _segsum(o, e_seg) * (1.0 / HEAD_DIM)
    d = o - m
    var = _segsum(d * d, e_seg) * (1.0 / HEAD_DIM)
    on = d * lax.rsqrt(var + RWKV_GN_EPS) * gn_g + gn_b
    return (on + bonus) * g


def _pool_lane_select(s2, s4, s8, s16):
    lane = lax.broadcasted_iota(jnp.int32, (1, POOL_DIM), 1)
    return jnp.where(lane < 64, s2, jnp.where(lane < 128, s4, jnp.where(lane < 192, s8, s16)))


def _pool_window_lanes():
    lane = lax.broadcasted_iota(jnp.int32, (1, POOL_DIM), 1)
    return jnp.where(lane < 64, 2, jnp.where(lane < 128, 4, jnp.where(lane < 192, 8, 16)))


def _lru_gates(xc, wx, bx, wa, ba, lam):
    gx = _sigmoid(_bdot(xc, wx) + bx)
    ga = _sigmoid(_bdot(xc, wa) + ba)
    log_a = -LRU_C * ga * _softplus(-lam)
    a = jnp.exp(log_a)
    b = jnp.sqrt(-jnp.tanh(log_a) * (a * a + 1.0)) * gx * xc
    return a, b


def _gmlp_pre(zq, ln_g, ln_b):
    z = _gelu(zq)
    u = z[:, :GMLP_DIM]
    v = z[:, GMLP_DIM:]
    m = jnp.mean(v, axis=-1, keepdims=True)
    d = v - m
    var = jnp.mean(d * d, axis=-1, keepdims=True)
    return u, d * lax.rsqrt(var + LN_EPS) * ln_g + ln_b


def _inproj_kernel(x_ref, g_ref, w_ref, o_ref):
    h = _rmsnorm(x_ref[...], g_ref[...])
    o_ref[...] = jnp.dot(h.astype(BF16), w_ref[...], preferred_element_type=F32)


def _inproj(x, g, w):
    m, n = x.shape[0], w.shape[1]
    tm = 512
    return pl.pallas_call(
        _inproj_kernel,
        grid=(m // tm,),
        in_specs=[pl.BlockSpec((tm, D_MODEL), lambda i: (i, 0)),
                  pl.BlockSpec((1, D_MODEL), lambda i: (0, 0)),
                  pl.BlockSpec((D_MODEL, n), lambda i: (0, 0))],
        out_specs=pl.BlockSpec((tm, n), lambda i: (i, 0)),
        out_shape=jax.ShapeDtypeStruct((m, n), F32),
        compiler_params=pltpu.CompilerParams(dimension_semantics=("arbitrary",),
                                             vmem_limit_bytes=VMEM_LIMIT),
        name="inproj",
    )(x, g, w)


def _ffn_kernel(x_ref, y_ref, wo_ref, g_ref, w1_ref, w2_ref, gf_ref, o_ref, *, final):
    x1 = x_ref[...] + jnp.dot(y_ref[...], wo_ref[...], preferred_element_type=F32)
    hf = _rmsnorm(x1, g_ref[...]).astype(BF16)
    acc = x1
    fc = 1024
    for c in range(D_FF // fc):
        h = jnp.dot(hf, w1_ref[:, c * fc:(c + 1) * fc], preferred_element_type=F32)
        h = jnp.square(jnp.maximum(h, 0.0)).astype(BF16)
        acc = acc + jnp.dot(h, w2_ref[c * fc:(c + 1) * fc, :], preferred_element_type=F32)
    if final:
        acc = _rmsnorm(acc, gf_ref[...])
    o_ref[...] = acc


def _ffn(x, y, wo, g, w1, w2, gf, final):
    m = x.shape[0]
    tm = 512
    const = lambda i: (0, 0)
    return pl.pallas_call(
        functools.partial(_ffn_kernel, final=final),
        grid=(m // tm,),
        in_specs=[pl.BlockSpec((tm, D_MODEL), lambda i: (i, 0)),
                  pl.BlockSpec((tm, D_MODEL), lambda i: (i, 0)),
                  pl.BlockSpec((D_MODEL, D_MODEL), const, pipeline_mode=pl.Buffered(1)),
                  pl.BlockSpec((1, D_MODEL), const),
                  pl.BlockSpec((D_MODEL, D_FF), const, pipeline_mode=pl.Buffered(1)),
                  pl.BlockSpec((D_FF, D_MODEL), const, pipeline_mode=pl.Buffered(1)),
                  pl.BlockSpec((1, D_MODEL), const)],
        out_specs=pl.BlockSpec((tm, D_MODEL), lambda i: (i, 0)),
        out_shape=jax.ShapeDtypeStruct((m, D_MODEL), F32),
        compiler_params=pltpu.CompilerParams(dimension_semantics=("arbitrary",),
                                             vmem_limit_bytes=VMEM_LIMIT),
        name="ffn",
    )(x, y, wo, g, w1, w2, gf)


def _even_prompt_kernel(p_ref, stp_ref, sts_ref, stw_ref,
                        mu_ref, w0_ref, wdec_ref, a0_ref, wa_ref, gw2_ref, kk_ref, ka_ref, rk_ref,
                        gng_ref, gnb_ref, eseg_ref, poolw_ref, pools_ref, tri_ref,
                        y_ref, opool_ref, oshift_ref, owkv_ref,
                        hpool, hshift, S, r_s, kp_s, v_s, ld_s, kk_s, a_s, o_s,
                        lhs_b, add_b, vk_b, bend_b, pend_b, *, tt, start):
    i = pl.program_id(1)
    nt = pl.num_programs(1)
    C = WKV_CHUNK

    @pl.when(i == 0)
    def _init():
        hpool[0:1, :] = jnp.zeros((1, POOL_DIM), F32)
        hpool[1:16, :] = stp_ref[0]
        hshift[...] = sts_ref[0]
        S[...] = jnp.zeros(S.shape, F32)
        for j in range(HEAD_PAIRS):
            S[j, 0:HEAD_DIM, 0:HEAD_DIM] = stw_ref[0, 2 * j]
            S[j, HEAD_DIM:PAIR_DIM, HEAD_DIM:PAIR_DIM] = stw_ref[0, 2 * j + 1]
        for j in range(HEAD_PAIRS):
            S[j] = jnp.transpose(S[j])

    p = p_ref[0]
    rows = lax.broadcasted_iota(jnp.int32, (tt, 1), 0)

    u = p[:, 0:POOL_DIM]
    ext = jnp.concatenate([hpool[...], u], axis=0)
    s2 = ext + pltpu.roll(ext, 1, 0)
    s4 = s2 + pltpu.roll(s2, 2, 0)
    s8 = s4 + pltpu.roll(s4, 4, 0)
    s16 = s8 + pltpu.roll(s8, 8, 0)
    sel = _pool_lane_select(s2, s4, s8, s16)[16:, :]
    pos = start + i * tt + rows
    cnt = jnp.minimum(_pool_window_lanes(), pos + 1).astype(F32)
    d = sel / cnt - u
    y_ref[0, :, 0:POOL_DIM] = (_bdot(d, poolw_ref[...]) * pools_ref[...]).astype(BF16)
    hpool[...] = ext[tt:tt + 16, :]

    P = p[:, POOL_DIM:EVEN_PROJ]
    Pprev = jnp.where(rows == 0, hshift[...], pltpu.roll(P, 1, 0))
    hshift[...] = P[tt - 1:tt, :]
    e_seg = eseg_ref[...]
    prm = (mu_ref[...], w0_ref[...], wdec_ref[...], a0_ref[...], wa_ref[...], gw2_ref[...],
           kk_ref[...], ka_ref[...], rk_ref[...], e_seg)
    r, kp, v, ld, kk, a, g, bonus = _rwkv_pointwise(P, Pprev, prm)
    r_s[...] = r
    kp_s[...] = kp
    v_s[...] = v
    ld_s[...] = ld
    kk_s[...] = kk
    a_s[...] = a

    rr = lax.broadcasted_iota(jnp.int32, (PAIR_DIM, PAIR_DIM), 0)
    cc = lax.broadcasted_iota(jnp.int32, (PAIR_DIM, PAIR_DIM), 1)
    same_head = (rr >= C) == (cc >= C)
    strict = same_head & (cc < rr)
    incl = same_head & (cc <= rr)
    eye = (rr == cc).astype(F32)
    lane_c = lax.broadcasted_iota(jnp.int32, (C, PAIR_DIM), 1)
    head0 = lane_c < HEAD_DIM
    tri = tri_ref[...]
    pairs = range(HEAD_PAIRS)

    def stack_heads(x):
        z = jnp.zeros_like(x)
        return jnp.concatenate([jnp.where(head0, x, z), jnp.where(head0, z, x)], axis=0)

    def prepare(c, carry):
        sl = pl.ds(pl.multiple_of(c * C, C), C)
        R = r_s[sl, :]
        K = kp_s[sl, :]
        V = v_s[sl, :]
        LD = ld_s[sl, :]
        KK = kk_s[sl, :]
        KA = KK * a_s[sl, :]
        L = _exact_dot_lhs01(tri, LD)
        Lend = L[C - 1:C, :]
        enL = jnp.exp(-L)
        eE = jnp.exp(Lend - L)
        Qr = R * jnp.exp(L)
        Qa = KK * jnp.exp(L - LD)
        Kt = K * enL
        Bt = KA * enL
        Kend = K * eE
        Bend = KA * eE
        Pend = jnp.exp(Lend)
        ls = [slice(j * PAIR_DIM, (j + 1) * PAIR_DIM) for j in pairs]
        qa_sm = [stack_heads(Qa[:, ls[j]]) for j in pairs]
        qr_sm = [stack_heads(Qr[:, ls[j]]) for j in pairs]
        v_sm = [stack_heads(V[:, ls[j]]).astype(BF16) for j in pairs]
        G = []
        for j in pairs:
            bt = Bt[:, ls[j]].astype(BF16)
            kt = Kt[:, ls[j]].astype(BF16)
            G.append(_bdot_nt(jnp.concatenate([qa_sm[j], qr_sm[j]], axis=0),
                              jnp.concatenate([bt, bt, kt, kt], axis=0)))
        Y = [jnp.where(strict, -G[j][0:2 * C, 0:2 * C], 0.0) for j in pairs]
        Aak = [jnp.where(strict, G[j][0:2 * C, 2 * C:4 * C], 0.0) for j in pairs]
        Arb = [jnp.where(incl, G[j][2 * C:4 * C, 0:2 * C], 0.0).astype(BF16) for j in pairs]
        Ark = [jnp.where(incl, G[j][2 * C:4 * C, 2 * C:4 * C], 0.0) for j in pairs]
        X = [_bdot(Y[j], Y[j]) for j in pairs]
        Tm = [eye + Y[j] for j in pairs]
        power = 2
        while 2 * power < C:
            for j in pairs:
                xb = X[j].astype(BF16)
                P2 = jnp.dot(xb, jnp.concatenate([xb, Tm[j].astype(BF16)], axis=1), preferred_element_type=F32)
                X[j] = P2[:, 0:PAIR_DIM]
                Tm[j] = Tm[j] + P2[:, PAIR_DIM:2 * PAIR_DIM]
            power *= 2
        for j in pairs:
            Tm[j] = Tm[j] + _bdot(X[j], Tm[j])
        AV = [_bdot(jnp.concatenate([Aak[j], Ark[j]], axis=0), v_sm[j]) for j in pairs]
        TQ = [_bdot(Tm[j], jnp.concatenate([qa_sm[j], AV[j][0:2 * C]], axis=1)) for j in pairs]
        AT = [jnp.dot(Arb[j], TQ[j].astype(BF16), preferred_element_type=F32) for j in pairs]
        for j in pairs:
            kb_t = jnp.transpose(jnp.concatenate(
                [stack_heads(Kend[:, ls[j]]), stack_heads(Bend[:, ls[j]]),
                 jnp.broadcast_to(Pend[:, ls[j]], (PAIR_DIM, PAIR_DIM))], axis=1))
            u = c * HEAD_PAIRS + j
            lhs_b[u, 0:2 * C, :] = TQ[j][:, 0:PAIR_DIM].astype(BF16)
            lhs_b[u, 2 * C:4 * C, :] = (qr_sm[j] - AT[j][:, 0:PAIR_DIM]).astype(BF16)
            add_b[u, 0:2 * C, :] = TQ[j][:, PAIR_DIM:2 * PAIR_DIM]
            add_b[u, 2 * C:4 * C, :] = AV[j][2 * C:4 * C] - AT[j][:, PAIR_DIM:2 * PAIR_DIM]
            vk_b[u] = _bdot(kb_t[0:PAIR_DIM], v_sm[j])
            bend_b[u] = kb_t[PAIR_DIM:2 * PAIR_DIM].astype(BF16)
            pend_b[u] = kb_t[2 * PAIR_DIM:3 * PAIR_DIM]
        return carry

    lax.fori_loop(0, tt // C, prepare, 0)

    def advance(c, carry):
        sl = pl.ds(pl.multiple_of(c * C, C), C)
        UO = [jnp.dot(lhs_b[c * HEAD_PAIRS + j], S[j].astype(BF16), preferred_element_type=F32)
              + add_b[c * HEAD_PAIRS + j] for j in pairs]
        for j in pairs:
            u = c * HEAD_PAIRS + j
            S[j] = pend_b[u] * S[j] + vk_b[u] - jnp.dot(bend_b[u], UO[j][0:2 * C].astype(BF16),
                                                        preferred_element_type=F32)
            o_s[sl, j * PAIR_DIM:(j + 1) * PAIR_DIM] = UO[j][2 * C:3 * C] + UO[j][3 * C:4 * C]
        return carry

    lax.fori_loop(0, tt // C, advance, 0)

    yb = _rwkv_post(o_s[...], bonus, g, gng_ref[...], gnb_ref[...], e_seg)
    y_ref[0, :, POOL_DIM:D_MODEL] = yb.astype(BF16)

    @pl.when(i == nt - 1)
    def _fin():
        opool_ref[0] = hpool[1:16, :]
        oshift_ref[0] = hshift[...]
        for j in range(HEAD_PAIRS):
            S[j] = jnp.transpose(S[j])
        for j in range(HEAD_PAIRS):
            owkv_ref[0, 2 * j] = S[j, 0:HEAD_DIM, 0:HEAD_DIM]
            owkv_ref[0, 2 * j + 1] = S[j, HEAD_DIM:PAIR_DIM, HEAD_DIM:PAIR_DIM]


def _even_prompt(p, st_pool, st_shift, st_wkv, prm, start):
    B, T, _ = p.shape
    tt = 256
    bt = lambda b, i: (b, i, 0)
    bs3 = lambda b, i: (b, 0, 0)
    bs4 = lambda b, i: (b, 0, 0, 0)
    c2 = lambda b, i: (0, 0)
    vec = lambda n: pl.BlockSpec((1, n), c2)
    scr = lambda: pltpu.VMEM((tt, RWKV_DIM), F32)
    n_units = (tt // WKV_CHUNK) * HEAD_PAIRS
    return pl.pallas_call(
        functools.partial(_even_prompt_kernel, tt=tt, start=start),
        grid=(B, T // tt),
        in_specs=[pl.BlockSpec((1, tt, EVEN_PROJ), bt),
                  pl.BlockSpec((1, POOL_BUF, POOL_DIM), bs3),
                  pl.BlockSpec((1, 1, RWKV_PROJ), bs3),
                  pl.BlockSpec((1, RWKV_HEADS, HEAD_DIM, HEAD_DIM), bs4),
                  vec(RWKV_PROJ), vec(RWKV_DIM), pl.BlockSpec((128, RWKV_DIM), c2), vec(RWKV_DIM),
                  pl.BlockSpec((128, RWKV_DIM), c2), pl.BlockSpec((128, RWKV_DIM), c2),
                  vec(RWKV_DIM), vec(RWKV_DIM), vec(RWKV_DIM), vec(RWKV_DIM), vec(RWKV_DIM),
                  pl.BlockSpec((SEG_TILE, SEG_TILE), c2), pl.BlockSpec((POOL_DIM, POOL_DIM), c2),
                  vec(POOL_DIM), pl.BlockSpec((WKV_CHUNK, WKV_CHUNK), c2)],
        out_specs=[pl.BlockSpec((1, tt, D_MODEL), bt),
                   pl.BlockSpec((1, POOL_BUF, POOL_DIM), bs3),
                   pl.BlockSpec((1, 1, RWKV_PROJ), bs3),
                   pl.BlockSpec((1, RWKV_HEADS, HEAD_DIM, HEAD_DIM), bs4)],
        out_shape=[jax.ShapeDtypeStruct((B, T, D_MODEL), BF16),
                   jax.ShapeDtypeStruct((B, POOL_BUF, POOL_DIM), F32),
                   jax.ShapeDtypeStruct((B, 1, RWKV_PROJ), F32),
                   jax.ShapeDtypeStruct((B, RWKV_HEADS, HEAD_DIM, HEAD_DIM), F32)],
        scratch_shapes=[pltpu.VMEM((16, POOL_DIM), F32), pltpu.VMEM((1, RWKV_PROJ), F32),
                        pltpu.VMEM((HEAD_PAIRS, PAIR_DIM, PAIR_DIM), F32),
                        scr(), scr(), scr(), scr(), scr(), scr(), scr(),
                        pltpu.VMEM((n_units, 2 * PAIR_DIM, PAIR_DIM), BF16),
                        pltpu.VMEM((n_units, 2 * PAIR_DIM, PAIR_DIM), F32),
                        pltpu.VMEM((n_units, PAIR_DIM, PAIR_DIM), F32),
                        pltpu.VMEM((n_units, PAIR_DIM, PAIR_DIM), BF16),
                        pltpu.VMEM((n_units, PAIR_DIM, PAIR_DIM), F32)],
        compiler_params=pltpu.CompilerParams(dimension_semantics=("arbitrary", "arbitrary"),
                                             vmem_limit_bytes=VMEM_LIMIT),
        name="even_prompt",
    )(p, st_pool, st_shift, st_wkv, *prm)


def _odd_prompt_kernel(q_ref, stc_ref, stl_ref,
                       lng_ref, lnb_ref, ws_ref, bias_ref, cw_ref, cb_ref, wx_ref, bx_ref, wa_ref, ba_ref,
                       lam_ref, y_ref, oconv_ref, olru_ref, hconv, hl, mix_s, *, tt):
    i = pl.program_id(1)
    nt = pl.num_programs(1)

    @pl.when(i == 0)
    def _init():
        hconv[0:5, :] = jnp.zeros((5, LRU_DIM), F32)
        hconv[5:8, :] = stc_ref[0]
        hl[...] = stl_ref[0]

    q = q_ref[0]
    rows = lax.broadcasted_iota(jnp.int32, (tt, 1), 0)

    u, vn = _gmlp_pre(q[:, 0:2 * GMLP_DIM], lng_ref[...], lnb_ref[...])
    rr = lax.broadcasted_iota(jnp.int32, (CHUNK, CHUNK), 0)
    cc = lax.broadcasted_iota(jnp.int32, (CHUNK, CHUNK), 1)
    causal = cc <= rr
    for h in range(GMLP_HEADS):
        wm = jnp.where(causal, ws_ref[h], 0.0).astype(BF16)
        ls = slice(h * CHUNK, (h + 1) * CHUNK)
        for c in range(tt // CHUNK):
            rs = slice(c * CHUNK, (c + 1) * CHUNK)
            mix_s[rs, ls] = jnp.dot(wm, vn[rs, ls].astype(BF16), preferred_element_type=F32) + bias_ref[:, ls]
    y_ref[0, :, 0:GMLP_DIM] = (u * mix_s[...]).astype(BF16)

    gate_in = q[:, 2 * GMLP_DIM:2 * GMLP_DIM + LRU_DIM]
    xr = q[:, 2 * GMLP_DIM + LRU_DIM:ODD_PROJ]
    ext = jnp.concatenate([hconv[...], xr], axis=0)
    xc = ext[8:, :] * cw_ref[3:4, :] + cb_ref[...]
    for j in range(1, CONV_WIDTH):
        xc = xc + pltpu.roll(ext, j, 0)[8:, :] * cw_ref[3 - j:4 - j, :]
    hconv[...] = ext[tt:tt + 8, :]
    a, b = _lru_gates(xc, wx_ref[...], bx_ref[...], wa_ref[...], ba_ref[...], lam_ref[...])
    dist = 1
    while dist < tt:
        keep = rows >= dist
        a_sh = jnp.where(keep, pltpu.roll(a, dist, 0), 1.0)
        b_sh = jnp.where(keep, pltpu.roll(b, dist, 0), 0.0)
        b = a * b_sh + b
        a = a * a_sh
        dist *= 2
    h = a * hl[...] + b
    hl[...] = h[tt - 1:tt, :]
    y_ref[0, :, GMLP_DIM:D_MODEL] = (h * _gelu(gate_in)).astype(BF16)

    @pl.when(i == nt - 1)
    def _fin():
        oconv_ref[0] = hconv[5:8, :]
        olru_ref[0] = hl[...]


def _odd_prompt(q, st_conv, st_lru, prm):
    B, T, _ = q.shape
    tt = 256
    bt = lambda b, i: (b, i, 0)
    bs3 = lambda b, i: (b, 0, 0)
    c2 = lambda b, i: (0, 0)
    c3 = lambda b, i: (0, 0, 0)
    vec = lambda n: pl.BlockSpec((1, n), c2)
    return pl.pallas_call(
        functools.partial(_odd_prompt_kernel, tt=tt),
        grid=(B, T // tt),
        in_specs=[pl.BlockSpec((1, tt, ODD_PROJ), bt),
                  pl.BlockSpec((1, CONV_WIDTH - 1, LRU_DIM), bs3),
                  pl.BlockSpec((1, 1, LRU_DIM), bs3),
                  vec(GMLP_DIM), vec(GMLP_DIM),
                  pl.BlockSpec((GMLP_HEADS, CHUNK, CHUNK), c3),
                  pl.BlockSpec((CHUNK, GMLP_DIM), c2),
                  pl.BlockSpec((CONV_WIDTH, LRU_DIM), c2), vec(LRU_DIM),
                  pl.BlockSpec((LRU_DIM, LRU_DIM), c2), vec(LRU_DIM),
                  pl.BlockSpec((LRU_DIM, LRU_DIM), c2), vec(LRU_DIM), vec(LRU_DIM)],
        out_specs=[pl.BlockSpec((1, tt, D_MODEL), bt),
                   pl.BlockSpec((1, CONV_WIDTH - 1, LRU_DIM), bs3),
                   pl.BlockSpec((1, 1, LRU_DIM), bs3)],
        out_shape=[jax.ShapeDtypeStruct((B, T, D_MODEL), BF16),
                   jax.ShapeDtypeStruct((B, CONV_WIDTH - 1, LRU_DIM), F32),
                   jax.ShapeDtypeStruct((B, 1, LRU_DIM), F32)],
        scratch_shapes=[pltpu.VMEM((8, LRU_DIM), F32), pltpu.VMEM((1, LRU_DIM), F32),
                        pltpu.VMEM((tt, GMLP_DIM), F32)],
        compiler_params=pltpu.CompilerParams(dimension_semantics=("arbitrary", "arbitrary"),
                                             vmem_limit_bytes=VMEM_LIMIT),
        name="odd_prompt",
    )(q, st_conv, st_lru, *prm)


def _even_sample_pre_kernel(p_ref, stp_ref, sts_ref,
                            mu_ref, w0_ref, wdec_ref, a0_ref, wa_ref, gw2_ref, kk_ref, ka_ref, rk_ref,
                            eseg_ref, poolw_ref, pools_ref,
                            r_ref, w_ref, kkn_ref, kka_ref, kp_ref, v_ref, g_ref, bonus_ref, ya_ref,
                            opool_ref, oshift_ref, *, T, start):
    prm = (mu_ref[...], w0_ref[...], wdec_ref[...], a0_ref[...], wa_ref[...], gw2_ref[...],
           kk_ref[...], ka_ref[...], rk_ref[...], eseg_ref[...])
    full = [stp_ref[s] for s in range(POOL_BUF)] + [p_ref[t][:, 0:POOL_DIM] for t in range(T)]
    wl = _pool_window_lanes()
    for t in range(T):
        P = p_ref[t][:, POOL_DIM:EVEN_PROJ]
        Pprev = sts_ref[...] if t == 0 else p_ref[t - 1][:, POOL_DIM:EVEN_PROJ]
        r, kp, v, ld, kk, a, g, bonus = _rwkv_pointwise(P, Pprev, prm)
        r_ref[t] = r
        w_ref[t] = jnp.exp(ld)
        kkn_ref[t] = kk
        kka_ref[t] = kk * a
        kp_ref[t] = kp
        v_ref[t] = v
        g_ref[t] = g
        bonus_ref[t] = bonus
        e = POOL_BUF + t
        s2 = full[e] + full[e - 1]
        s4 = s2 + full[e - 2] + full[e - 3]
        s8 = s4 + full[e - 4] + full[e - 5] + full[e - 6] + full[e - 7]
        s16 = s8
        for s in range(8, 16):
            s16 = s16 + full[e - s]
        sel = _pool_lane_select(s2, s4, s8, s16)
        cnt = jnp.minimum(wl, start + t + 1).astype(F32)
        d = sel / cnt - full[e]
        ya_ref[t] = _bdot(d, poolw_ref[...]) * pools_ref[...]
    for s in range(POOL_BUF):
        opool_ref[s] = full[T + s]
    oshift_ref[...] = p_ref[T - 1][:, POOL_DIM:EVEN_PROJ]


def _even_sample_pre(p, st_pool, st_shift, prm, start):
    T, B, _ = p.shape
    n_out = 8
    big = jax.ShapeDtypeStruct((T, B, RWKV_DIM), F32)
    return pl.pallas_call(
        functools.partial(_even_sample_pre_kernel, T=T, start=start),
        out_shape=[big] * n_out + [jax.ShapeDtypeStruct((T, B, POOL_DIM), F32),
                                   jax.ShapeDtypeStruct((POOL_BUF, B, POOL_DIM), F32),
                                   jax.ShapeDtypeStruct((B, RWKV_PROJ), F32)],
        compiler_params=pltpu.CompilerParams(vmem_limit_bytes=VMEM_LIMIT),
        name="even_sample_pre",
    )(p, st_pool, st_shift, *prm)


def _wkv_sample_kernel(r_ref, w_ref, kk_ref, kka_ref, kp_ref, v_ref, s_ref, o_ref, so_ref, *, T, nb):
    rr = lax.broadcasted_iota(jnp.int32, (HEAD_DIM, HEAD_DIM), 0)
    cc = lax.broadcasted_iota(jnp.int32, (HEAD_DIM, HEAD_DIM), 1)
    eye = (rr == cc).astype(F32)

    def body(ib, carry):
        base = pl.multiple_of(ib * WKV_SAMPLE_GROUP, WKV_SAMPLE_GROUP)
        blk = pl.ds(base, WKV_SAMPLE_GROUP)
        group = range(WKV_SAMPLE_GROUP)
        S = [s_ref[base + u] for u in group]
        for t in range(T):
            kk, v, w, kka, kp, r = (ref[t, blk, :] for ref in (kk_ref, v_ref, w_ref, kka_ref, kp_ref, r_ref))
            row = lambda x, u: x[u:u + 1, :]
            sk = [jnp.sum(S[u] * row(kk, u), axis=1, keepdims=True) for u in group]
            vcol = [jnp.sum(eye * row(v, u), axis=1, keepdims=True) for u in group]
            S = [S[u] * row(w, u) - sk[u] * row(kka, u) + vcol[u] * row(kp, u) for u in group]
            ocol = [jnp.sum(S[u] * row(r, u), axis=1, keepdims=True) for u in group]
            o_ref[t, blk, :] = jnp.concatenate(
                [jnp.sum(eye * ocol[u], axis=0, keepdims=True) for u in group], axis=0)
        for u in group:
            so_ref[base + u] = S[u]
        return carry

    lax.fori_loop(0, nb // WKV_SAMPLE_GROUP, body, 0)


def _wkv_sample(r, w, kk, kka, kp, v, s):
    T, R, _ = r.shape
    nb = 128
    row_spec = pl.BlockSpec((T, nb, HEAD_DIM), lambda i: (0, i, 0))
    st_spec = pl.BlockSpec((nb, HEAD_DIM, HEAD_DIM), lambda i: (i, 0, 0))
    return pl.pallas_call(
        functools.partial(_wkv_sample_kernel, T=T, nb=nb),
        grid=(R // nb,),
        in_specs=[row_spec] * 6 + [st_spec],
        out_specs=[row_spec, st_spec],
        out_shape=[jax.ShapeDtypeStruct((T, R, HEAD_DIM), F32),
                   jax.ShapeDtypeStruct((R, HEAD_DIM, HEAD_DIM), F32)],
        compiler_params=pltpu.CompilerParams(dimension_semantics=("arbitrary",),
                                             vmem_limit_bytes=VMEM_LIMIT),
        name="wkv_sample",
    )(r, w, kk, kka, kp, v, s)


def _even_sample_post_kernel(o_ref, bonus_ref, g_ref, ya_ref, gng_ref, gnb_ref, eseg_ref, y_ref, *, T):
    for t in range(T):
        yb = _rwkv_post(o_ref[t], bonus_ref[t], g_ref[t], gng_ref[...], gnb_ref[...], eseg_ref[...])
        y_ref[t, :, 0:POOL_DIM] = ya_ref[t].astype(BF16)
        y_ref[t, :, POOL_DIM:D_MODEL] = yb.astype(BF16)


def _even_sample_post(o, bonus, g, ya, gn_g, gn_b, e_seg):
    T, B, _ = o.shape
    return pl.pallas_call(
        functools.partial(_even_sample_post_kernel, T=T),
        out_shape=jax.ShapeDtypeStruct((T, B, D_MODEL), BF16),
        compiler_params=pltpu.CompilerParams(vmem_limit_bytes=VMEM_LIMIT),
        name="even_sample_post",
    )(o, bonus, g, ya, gn_g, gn_b, e_seg)


def _odd_sample_kernel(q_ref, stc_ref, stl_ref,
                       lng_ref, lnb_ref, wsm_ref, bsm_ref, cw_ref, cb_ref, wx_ref, bx_ref, wa_ref, ba_ref,
                       lam_ref, y_ref, v_ref, oconv_ref, olru_ref, *, T):
    vns = []
    us = []
    for t in range(T):
        u, vn = _gmlp_pre(q_ref[t][:, 0:2 * GMLP_DIM], lng_ref[...], lnb_ref[...])
        us.append(u)
        vns.append(vn)
        v_ref[t] = vn
    full = [stc_ref[s] for s in range(CONV_WIDTH - 1)] + \
           [q_ref[t][:, 2 * GMLP_DIM + LRU_DIM:ODD_PROJ] for t in range(T)]
    h = stl_ref[...]
    for t in range(T):
        mix = bsm_ref[t:t + 1, :]
        for j in range(t + 1):
            mix = mix + wsm_ref[t * T + j:t * T + j + 1, :] * vns[j]
        y_ref[t, :, 0:GMLP_DIM] = (us[t] * mix).astype(BF16)
        xc = full[t + CONV_WIDTH - 1] * cw_ref[CONV_WIDTH - 1:CONV_WIDTH, :] + cb_ref[...]
        for j in range(CONV_WIDTH - 1):
            xc = xc + full[t + j] * cw_ref[j:j + 1, :]
        a, b = _lru_gates(xc, wx_ref[...], bx_ref[...], wa_ref[...], ba_ref[...], lam_ref[...])
        h = a * h + b
        gate_in = q_ref[t][:, 2 * GMLP_DIM:2 * GMLP_DIM + LRU_DIM]
        y_ref[t, :, GMLP_DIM:D_MODEL] = (h * _gelu(gate_in)).astype(BF16)
    for s in range(CONV_WIDTH - 1):
        oconv_ref[s] = full[T + s]
    olru_ref[...] = h


def _odd_sample(q, st_conv, st_lru, prm):
    T, B, _ = q.shape
    return pl.pallas_call(
        functools.partial(_odd_sample_kernel, T=T),
        out_shape=[jax.ShapeDtypeStruct((T, B, D_MODEL), BF16),
                   jax.ShapeDtypeStruct((T, B, GMLP_DIM), F32),
                   jax.ShapeDtypeStruct((CONV_WIDTH - 1, B, LRU_DIM), F32),
                   jax.ShapeDtypeStruct((B, LRU_DIM), F32)],
        compiler_params=pltpu.CompilerParams(vmem_limit_bytes=VMEM_LIMIT),
        name="odd_sample",
    )(q, st_conv, st_lru, *prm)


def _block_diag(w):
    n, c, d = w.shape
    eye = jnp.eye(n, dtype=w.dtype)
    return (eye[:, None, :, None] * w[:, :, None, :]).reshape(n * c, n * d)


def _row(x):
    return x.reshape(1, -1)


def kernel(x_prompt, x_sample, state_pool, state_shift, state_wkv, state_conv, state_lru, ev_norm_g, ev_w_in, pool_w, pool_scale, rwkv_mu, rwkv_w0, rwkv_w_w2, rwkv_a0, rwkv_a_w2, rwkv_g_w2, rwkv_k_k, rwkv_k_a, rwkv_r_k, rwkv_gn_g, rwkv_gn_b, ev_w_out, od_norm_g, od_w_in, gmlp_ln_g, gmlp_ln_b, gmlp_ws, gmlp_bs, lru_conv_w, lru_conv_b, lru_wx, lru_bx, lru_wa, lru_ba, lru_lam, od_w_out, ff_norm_g, ff_w1, ff_w2, final_norm_g):
    B, T, _ = x_prompt.shape
    DB, DT, _ = x_sample.shape
    past_len = 16384

    seg_ids = jnp.arange(SEG_TILE) // HEAD_DIM
    e_seg = (seg_ids[:, None] == seg_ids[None, :]).astype(BF16)
    tri = (jnp.arange(WKV_CHUNK)[None, :] <= jnp.arange(WKV_CHUNK)[:, None]).astype(BF16)
    zlora = jnp.zeros((64, RWKV_DIM), F32)

    ev_common = (_row(rwkv_mu[0]), _row(rwkv_w0[0]),
                 jnp.concatenate([rwkv_w_w2[0], zlora], 0).astype(BF16), _row(rwkv_a0[0]),
                 jnp.concatenate([zlora, rwkv_a_w2[0]], 0).astype(BF16), rwkv_g_w2[0].astype(BF16),
                 _row(rwkv_k_k[0]), _row(rwkv_k_a[0]), _row(rwkv_r_k[0]))
    gn_g, gn_b = _row(rwkv_gn_g[0]), _row(rwkv_gn_b[0])
    pool_bd = _block_diag(pool_w[0]).astype(BF16)
    pool_sc = _row(pool_scale[0])
    w_in0 = ev_w_in[0].astype(BF16)
    g_in0 = _row(ev_norm_g[0])

    xp = x_prompt.reshape(B * T, D_MODEL)
    xs = jnp.transpose(x_sample, (1, 0, 2)).reshape(DT * DB, D_MODEL)

    pp = _inproj(xp, g_in0, w_in0).reshape(B, T, EVEN_PROJ)
    ps = _inproj(xs, g_in0, w_in0).reshape(DT, DB, EVEN_PROJ)

    yp, p_pool, p_shift, p_wkv = _even_prompt(
        pp, jnp.zeros((B, POOL_BUF, POOL_DIM), F32), jnp.zeros((B, 1, RWKV_PROJ), F32),
        jnp.zeros((B, RWKV_HEADS, HEAD_DIM, HEAD_DIM), F32),
        ev_common + (gn_g, gn_b, e_seg, pool_bd, pool_sc, tri), 0)

    pre = _even_sample_pre(ps, jnp.transpose(state_pool[0], (1, 0, 2)), state_shift[0],
                           ev_common + (e_seg, pool_bd, pool_sc), past_len)
    r_s, w_s, kk_s, kka_s, kp_s, v_s, g_s, bonus_s, ya_s, s_pool_tm, s_shift = pre
    hm = lambda t: t.reshape(DT, DB * RWKV_HEADS, HEAD_DIM)
    o_s, s_wkv = _wkv_sample(hm(r_s), hm(w_s), hm(kk_s), hm(kka_s), hm(kp_s), hm(v_s),
                             state_wkv[0].reshape(DB * RWKV_HEADS, HEAD_DIM, HEAD_DIM))
    ys = _even_sample_post(o_s.reshape(DT, DB, RWKV_DIM), bonus_s, g_s, ya_s, gn_g, gn_b, e_seg)

    w_out0 = ev_w_out[0].astype(BF16)
    ffg = lambda l: _row(ff_norm_g[l])
    gfin = _row(final_norm_g)
    xp = _ffn(xp, yp.reshape(B * T, D_MODEL), w_out0, ffg(0), ff_w1[0].astype(BF16), ff_w2[0].astype(BF16),
              gfin, False)
    xs = _ffn(xs, ys.reshape(DT * DB, D_MODEL), w_out0, ffg(0), ff_w1[0].astype(BF16), ff_w2[0].astype(BF16),
              gfin, False)

    w_in1 = od_w_in[0].astype(BF16)
    g_in1 = _row(od_norm_g[0])
    qp = _inproj(xp, g_in1, w_in1).reshape(B, T, ODD_PROJ)
    qs = _inproj(xs, g_in1, w_in1).reshape(DT, DB, ODD_PROJ)

    lru_common = (lru_conv_w[0], _row(lru_conv_b[0]), _block_diag(lru_wx[0]).astype(BF16), _row(lru_bx[0]),
                  _block_diag(lru_wa[0]).astype(BF16), _row(lru_ba[0]), _row(lru_lam[0]))
    ln = (_row(gmlp_ln_g[0]), _row(gmlp_ln_b[0]))
    bias_full = jnp.repeat(jnp.transpose(gmlp_bs[0]), CHUNK, axis=1)
    yp, p_conv, p_lru = _odd_prompt(
        qp, jnp.zeros((B, CONV_WIDTH - 1, LRU_DIM), F32), jnp.zeros((B, 1, LRU_DIM), F32),
        ln + (gmlp_ws[0], bias_full) + lru_common)

    ws_small = jnp.repeat(jnp.transpose(gmlp_ws[0][:, :DT, :DT], (1, 2, 0)).reshape(DT * DT, GMLP_HEADS),
                          CHUNK, axis=1)
    ys, s_v, s_conv_tm, s_lru = _odd_sample(
        qs, jnp.transpose(state_conv[0], (1, 0, 2)), state_lru[0],
        ln + (ws_small, bias_full[:DT]) + lru_common)

    w_out1 = od_w_out[0].astype(BF16)
    xp = _ffn(xp, yp.reshape(B * T, D_MODEL), w_out1, ffg(1), ff_w1[1].astype(BF16), ff_w2[1].astype(BF16),
              gfin, True)
    xs = _ffn(xs, ys.reshape(DT * DB, D_MODEL), w_out1, ffg(1), ff_w1[1].astype(BF16), ff_w2[1].astype(BF16),
              gfin, True)

    tm2bm = lambda t: jnp.transpose(t, (1, 0, 2))
    y_prompt = xp.reshape(B, T, D_MODEL)
    y_sample = tm2bm(xs.reshape(DT, DB, D_MODEL))
    return (y_prompt, y_sample,
            p_pool[None], p_shift.reshape(1, B, RWKV_PROJ), p_wkv[None],
            p_conv[None], p_lru.reshape(1, B, LRU_DIM),
            tm2bm(s_pool_tm)[None], s_shift[None],
            s_wkv.reshape(1, DB, RWKV_HEADS, HEAD_DIM, HEAD_DIM),
            tm2bm(s_conv_tm)[None], s_lru[None], tm2bm(s_v)[None])
```

```python
import functools

import jax
import jax.numpy as jnp
from jax import lax
from jax.experimental import pallas as pl
from jax.experimental.pallas import tpu as pltpu

F32 = jnp.float32
BF16 = jnp.bfloat16

D_MODEL = 1024
NORM_EPS = 1e-6
D_FF = 4 * D_MODEL

POOL_WINDOWS = (2, 4, 8, 16)
POOL_GROUP_DIM = 64
POOL_DIM = 256
POOL_BUF = 15

HEAD_DIM = 64
RWKV_DIM = 768
RWKV_HEADS = 12
HEAD_PAIRS = RWKV_HEADS // 2
PAIR_DIM = 2 * HEAD_DIM
RWKV_PROJ = 2560
RWKV_GN_EPS = 64e-5
EVEN_PROJ = POOL_DIM + RWKV_PROJ
LORA_OFF = 3 * RWKV_DIM
GATE_OFF = LORA_OFF + 128

CHUNK = 128
GMLP_DIM = 512
GMLP_HEADS = 4
LN_EPS = 1e-5
LRU_DIM = 512
CONV_WIDTH = 4
LRU_C = 8.0
ODD_PROJ = 2048

WKV_CHUNK = 64
SEG_TILE = 256
WKV_PREP_CHUNKS = 2
WKV_SAMPLE_GROUP = 8

VMEM_LIMIT = 48 * 1024 * 1024


def _bdot(a, b):
    return jnp.dot(a.astype(BF16), b.astype(BF16), preferred_element_type=F32)


def _bdot_nt(a, b):
    return lax.dot_general(a.astype(BF16), b.astype(BF16), (((1,), (1,)), ((), ())),
                           preferred_element_type=F32)


def _split3(x):
    hi = x.astype(BF16)
    r1 = x - hi.astype(F32)
    mid = r1.astype(BF16)
    lo = (r1 - mid.astype(F32)).astype(BF16)
    return hi, mid, lo


def _exact_dot_rhs01(x, e):
    hi = x.astype(BF16)
    lo = (x - hi.astype(F32)).astype(BF16)
    d = lambda t: jnp.dot(t, e, preferred_element_type=F32)
    return d(hi) + d(lo)


def _exact_dot_lhs01(e, x):
    hi, mid, lo = _split3(x)
    d = lambda t: jnp.dot(e, t, preferred_element_type=F32)
    return d(hi) + d(mid) + d(lo)


def _segsum(x, e_seg):
    parts = [_exact_dot_rhs01(x[:, g * SEG_TILE:(g + 1) * SEG_TILE], e_seg)
             for g in range(RWKV_DIM // SEG_TILE)]
    return jnp.concatenate(parts, axis=1)


def _softplus(z):
    return jnp.maximum(z, 0.0) + jnp.log(1.0 + jnp.exp(-jnp.abs(z)))


def _sigmoid(z):
    return 0.5 * jnp.tanh(0.5 * z) + 0.5


def _gelu(z):
    return 0.5 * z * (1.0 + jnp.tanh(0.7978845608028654 * (z + 0.044715 * (z * z * z))))


def _rmsnorm(x, g):
    ms = jnp.mean(x * x, axis=-1, keepdims=True)
    return x * lax.rsqrt(ms + NORM_EPS) * g


def _rwkv_pointwise(P, Pprev, prm):
    (mu, w0, wdec, a0, wa, gw2, k_k, k_a, r_k, e_seg) = prm
    xs = P + (Pprev - P) * mu
    r = xs[:, 0:RWKV_DIM]
    k = xs[:, RWKV_DIM:2 * RWKV_DIM]
    v = xs[:, 2 * RWKV_DIM:3 * RWKV_DIM]
    c_wa = xs[:, LORA_OFF:GATE_OFF]
    cg = xs[:, GATE_OFF:RWKV_PROJ]
    w = -_softplus(-(w0 + _bdot(jnp.tanh(c_wa), wdec))) - 0.5
    ld = -jnp.exp(w)
    a = _sigmoid(a0 + _bdot(c_wa, wa))
    g = _bdot(_sigmoid(cg), gw2)
    kk = k * k_k
    kk = kk * lax.rsqrt(jnp.maximum(_segsum(kk * kk, e_seg), 1e-24))
    kp = k * (1.0 + (a - 1.0) * k_a)
    bonus = _segsum(r * kp * r_k, e_seg) * v
    return r, kp, v, ld, kk, a, g, bonus


def _rwkv_post(o, bonus, g, gn_g, gn_b, e_seg):
    m = _segsum(o, e_seg) * (1.0 / HEAD_DIM)
    d = o - m
    var = _segsum(d * d, e_seg) * (1.0 / HEAD_DIM)
    on = d * lax.rsqrt(var + RWKV_GN_EPS) * gn_g + gn_b
    return (on + bonus) * g


def _pool_lane_select(s2, s4, s8, s16):
    lane = lax.broadcasted_iota(jnp.int32, (1, POOL_DIM), 1)
    return jnp.where(lane < 64, s2, jnp.where(lane < 128, s4, jnp.where(lane < 192, s8, s16)))


def _pool_window_lanes():
    lane = lax.broadcasted_iota(jnp.int32, (1, POOL_DIM), 1)
    return jnp.where(lane < 64, 2, jnp.where(lane < 128, 4, jnp.where(lane < 192, 8, 16)))


def _lru_gates(xc, wx, bx, wa, ba, lam):
    gx = _sigmoid(_bdot(xc, wx) + bx)
    ga = _sigmoid(_bdot(xc, wa) + ba)
    log_a = -LRU_C * ga * _softplus(-lam)
    a = jnp.exp(log_a)
    b = jnp.sqrt(-jnp.tanh(log_a) * (a * a + 1.0)) * gx * xc
    return a, b


def _gmlp_pre(zq, ln_g, ln_b):
    z = _gelu(zq)
    u = z[:, :GMLP_DIM]
    v = z[:, GMLP_DIM:]
    m = jnp.mean(v, axis=-1, keepdims=True)
    d = v - m
    var = jnp.mean(d * d, axis=-1, keepdims=True)
    return u, d * lax.rsqrt(var + LN_EPS) * ln_g + ln_b


def _inproj_kernel(x_ref, g_ref, w_ref, o_ref):
    h = _rmsnorm(x_ref[...], g_ref[...])
    o_ref[...] = jnp.dot(h.astype(BF16), w_ref[...], preferred_element_type=F32)


def _inproj(x, g, w):
    m, n = x.shape[0], w.shape[1]
    tm = 512
    return pl.pallas_call(
        _inproj_kernel,
        grid=(m // tm,),
        in_specs=[pl.BlockSpec((tm, D_MODEL), lambda i: (i, 0)),
                  pl.BlockSpec((1, D_MODEL), lambda i: (0, 0)),
                  pl.BlockSpec((D_MODEL, n), lambda i: (0, 0))],
        out_specs=pl.BlockSpec((tm, n), lambda i: (i, 0)),
        out_shape=jax.ShapeDtypeStruct((m, n), F32),
        compiler_params=pltpu.CompilerParams(dimension_semantics=("arbitrary",),
                                             vmem_limit_bytes=VMEM_LIMIT),
        name="inproj",
    )(x, g, w)


def _ffn_kernel(x_ref, y_ref, wo_ref, g_ref, w1_ref, w2_ref, gf_ref, o_ref, *, final):
    x1 = x_ref[...] + jnp.dot(y_ref[...], wo_ref[...], preferred_element_type=F32)
    hf = _rmsnorm(x1, g_ref[...]).astype(BF16)
    acc = x1
    fc = 1024
    for c in range(D_FF // fc):
        h = jnp.dot(hf, w1_ref[:, c * fc:(c + 1) * fc], preferred_element_type=F32)
        h = jnp.square(jnp.maximum(h, 0.0)).astype(BF16)
        acc = acc + jnp.dot(h, w2_ref[c * fc:(c + 1) * fc, :], preferred_element_type=F32)
    if final:
        acc = _rmsnorm(acc, gf_ref[...])
    o_ref[...] = acc


def _ffn(x, y, wo, g, w1, w2, gf, final):
    m = x.shape[0]
    tm = 512
    const = lambda i: (0, 0)
    return pl.pallas_call(
        functools.partial(_ffn_kernel, final=final),
        grid=(m // tm,),
        in_specs=[pl.BlockSpec((tm, D_MODEL), lambda i: (i, 0)),
                  pl.BlockSpec((tm, D_MODEL), lambda i: (i, 0)),
                  pl.BlockSpec((D_MODEL, D_MODEL), const, pipeline_mode=pl.Buffered(1)),
                  pl.BlockSpec((1, D_MODEL), const),
                  pl.BlockSpec((D_MODEL, D_FF), const, pipeline_mode=pl.Buffered(1)),
                  pl.BlockSpec((D_FF, D_MODEL), const, pipeline_mode=pl.Buffered(1)),
                  pl.BlockSpec((1, D_MODEL), const)],
        out_specs=pl.BlockSpec((tm, D_MODEL), lambda i: (i, 0)),
        out_shape=jax.ShapeDtypeStruct((m, D_MODEL), F32),
        compiler_params=pltpu.CompilerParams(dimension_semantics=("arbitrary",),
                                             vmem_limit_bytes=VMEM_LIMIT),
        name="ffn",
    )(x, y, wo, g, w1, w2, gf)


def _even_prompt_kernel(p_ref, stp_ref, sts_ref, stw_ref,
                        mu_ref, w0_ref, wdec_ref, a0_ref, wa_ref, gw2_ref, kk_ref, ka_ref, rk_ref,
                        gng_ref, gnb_ref, eseg_ref, poolw_ref, pools_ref, tri_ref,
                        y_ref, opool_ref, oshift_ref, owkv_ref,
                        hpool, hshift, S, r_s, kp_s, v_s, ld_s, kk_s, a_s, o_s,
                        lhs_b, add_b, vk_b, bend_b, pend_b, *, tt, start):
    i = pl.program_id(1)
    nt = pl.num_programs(1)
    C = WKV_CHUNK

    @pl.when(i == 0)
    def _init():
        hpool[0:1, :] = jnp.zeros((1, POOL_DIM), F32)
        hpool[1:16, :] = stp_ref[0]
        hshift[...] = sts_ref[0]
        S[...] = jnp.zeros(S.shape, F32)
        for j in range(HEAD_PAIRS):
            S[j, 0:HEAD_DIM, 0:HEAD_DIM] = stw_ref[0, 2 * j]
            S[j, HEAD_DIM:PAIR_DIM, HEAD_DIM:PAIR_DIM] = stw_ref[0, 2 * j + 1]
        for j in range(HEAD_PAIRS):
            S[j] = jnp.transpose(S[j])

    p = p_ref[0]
    rows = lax.broadcasted_iota(jnp.int32, (tt, 1), 0)

    u = p[:, 0:POOL_DIM]
    ext = jnp.concatenate([hpool[...], u], axis=0)
    s2 = ext + pltpu.roll(ext, 1, 0)
    s4 = s2 + pltpu.roll(s2, 2, 0)
    s8 = s4 + pltpu.roll(s4, 4, 0)
    s16 = s8 + pltpu.roll(s8, 8, 0)
    sel = _pool_lane_select(s2, s4, s8, s16)[16:, :]
    pos = start + i * tt + rows
    cnt = jnp.minimum(_pool_window_lanes(), pos + 1).astype(F32)
    d = sel / cnt - u
    y_ref[0, :, 0:POOL_DIM] = (_bdot(d, poolw_ref[...]) * pools_ref[...]).astype(BF16)
    hpool[...] = ext[tt:tt + 16, :]

    P = p[:, POOL_DIM:EVEN_PROJ]
    Pprev = jnp.where(rows == 0, hshift[...], pltpu.roll(P, 1, 0))
    hshift[...] = P[tt - 1:tt, :]
    e_seg = eseg_ref[...]
    prm = (mu_ref[...], w0_ref[...], wdec_ref[...], a0_ref[...], wa_ref[...], gw2_ref[...],
           kk_ref[...], ka_ref[...], rk_ref[...], e_seg)
    r, kp, v, ld, kk, a, g, bonus = _rwkv_pointwise(P, Pprev, prm)
    r_s[...] = r
    kp_s[...] = kp
    v_s[...] = v
    ld_s[...] = ld
    kk_s[...] = kk
    a_s[...] = a

    rr = lax.broadcasted_iota(jnp.int32, (PAIR_DIM, PAIR_DIM), 0)
    cc = lax.broadcasted_iota(jnp.int32, (PAIR_DIM, PAIR_DIM), 1)
    same_head = (rr >= C) == (cc >= C)
    strict = same_head & (cc < rr)
    incl = same_head & (cc <= rr)
    eye = (rr == cc).astype(F32)
    lane_c = lax.broadcasted_iota(jnp.int32, (C, PAIR_DIM), 1)
    head0 = lane_c < HEAD_DIM
    tri = tri_ref[...]
    pairs = range(HEAD_PAIRS)

    def stack_heads(x):
        z = jnp.zeros_like(x)
        return jnp.concatenate([jnp.where(head0, x, z), jnp.where(head0, z, x)], axis=0)

    def prepare(cg, carry):
        qa_sm, qr_sm, v_sm, rhs_g, kb_src, slot = [], [], [], [], [], []
        for ci in range(WKV_PREP_CHUNKS):
            c = cg * WKV_PREP_CHUNKS + ci
            sl = pl.ds(pl.multiple_of(c * C, C), C)
            R = r_s[sl, :]
            K = kp_s[sl, :]
            V = v_s[sl, :]
            LD = ld_s[sl, :]
            KK = kk_s[sl, :]
            KA = KK * a_s[sl, :]
            L = _exact_dot_lhs01(tri, LD)
            Lend = L[C - 1:C, :]
            enL = jnp.exp(-L)
            eE = jnp.exp(Lend - L)
            Qr = R * jnp.exp(L)
            Qa = KK * jnp.exp(L - LD)
            Kt = K * enL
            Bt = KA * enL
            Kend = K * eE
            Bend = KA * eE
            Pend = jnp.exp(Lend)
            for j in pairs:
                ls = slice(j * PAIR_DIM, (j + 1) * PAIR_DIM)
                qa_sm.append(stack_heads(Qa[:, ls]))
                qr_sm.append(stack_heads(Qr[:, ls]))
                v_sm.append(stack_heads(V[:, ls]).astype(BF16))
                bt = Bt[:, ls].astype(BF16)
                kt = Kt[:, ls].astype(BF16)
                rhs_g.append(jnp.concatenate([bt, bt, kt, kt], axis=0))
                kb_src.append((Kend[:, ls], Bend[:, ls], Pend[:, ls]))
                slot.append(c * HEAD_PAIRS + j)
        units = range(len(slot))
        G = [_bdot_nt(jnp.concatenate([qa_sm[u], qr_sm[u]], axis=0), rhs_g[u]) for u in units]
        Y = [jnp.where(strict, -G[u][0:2 * C, 0:2 * C], 0.0) for u in units]
        Aak = [jnp.where(strict, G[u][0:2 * C, 2 * C:4 * C], 0.0) for u in units]
        Arb = [jnp.where(incl, G[u][2 * C:4 * C, 0:2 * C], 0.0).astype(BF16) for u in units]
        Ark = [jnp.where(incl, G[u][2 * C:4 * C, 2 * C:4 * C], 0.0) for u in units]
        X = [_bdot(Y[u], Y[u]) for u in units]
        Tm = [eye + Y[u] for u in units]
        power = 2
        while 2 * power < C:
            for u in units:
                xb = X[u].astype(BF16)
                P2 = jnp.dot(xb, jnp.concatenate([xb, Tm[u].astype(BF16)], axis=1), preferred_element_type=F32)
                X[u] = P2[:, 0:PAIR_DIM]
                Tm[u] = Tm[u] + P2[:, PAIR_DIM:2 * PAIR_DIM]
            power *= 2
        for u in units:
            Tm[u] = Tm[u] + _bdot(X[u], Tm[u])
        AV = [_bdot(jnp.concatenate([Aak[u], Ark[u]], axis=0), v_sm[u]) for u in units]
        TQ = [_bdot(Tm[u], jnp.concatenate([qa_sm[u], AV[u][0:2 * C]], axis=1)) for u in units]
        AT = [jnp.dot(Arb[u], TQ[u].astype(BF16), preferred_element_type=F32) for u in units]
        for u in units:
            kend, bend, pend = kb_src[u]
            kb_t = jnp.transpose(jnp.concatenate(
                [stack_heads(kend), stack_heads(bend), jnp.broadcast_to(pend, (PAIR_DIM, PAIR_DIM))],
                axis=1))
            lhs_b[slot[u], 0:2 * C, :] = TQ[u][:, 0:PAIR_DIM].astype(BF16)
            lhs_b[slot[u], 2 * C:4 * C, :] = (qr_sm[u] - AT[u][:, 0:PAIR_DIM]).astype(BF16)
            add_b[slot[u], 0:2 * C, :] = TQ[u][:, PAIR_DIM:2 * PAIR_DIM]
            add_b[slot[u], 2 * C:4 * C, :] = AV[u][2 * C:4 * C] - AT[u][:, PAIR_DIM:2 * PAIR_DIM]
            vk_b[slot[u]] = _bdot(kb_t[0:PAIR_DIM], v_sm[u])
            bend_b[slot[u]] = kb_t[PAIR_DIM:2 * PAIR_DIM].astype(BF16)
            pend_b[slot[u]] = kb_t[2 * PAIR_DIM:3 * PAIR_DIM]
        return carry

    lax.fori_loop(0, tt // (C * WKV_PREP_CHUNKS), prepare, 0)

    def advance(c, carry):
        sl = pl.ds(pl.multiple_of(c * C, C), C)
        UO = [jnp.dot(lhs_b[c * HEAD_PAIRS + j], S[j].astype(BF16), preferred_element_type=F32)
              + add_b[c * HEAD_PAIRS + j] for j in pairs]
        for j in pairs:
            u = c * HEAD_PAIRS + j
            S[j] = pend_b[u] * S[j] + vk_b[u] - jnp.dot(bend_b[u], UO[j][0:2 * C].astype(BF16),
                                                        preferred_element_type=F32)
            o_s[sl, j * PAIR_DIM:(j + 1) * PAIR_DIM] = UO[j][2 * C:3 * C] + UO[j][3 * C:4 * C]
        return carry

    lax.fori_loop(0, tt // C, advance, 0)

    yb = _rwkv_post(o_s[...], bonus, g, gng_ref[...], gnb_ref[...], e_seg)
    y_ref[0, :, POOL_DIM:D_MODEL] = yb.astype(BF16)

    @pl.when(i == nt - 1)
    def _fin():
        opool_ref[0] = hpool[1:16, :]
        oshift_ref[0] = hshift[...]
        for j in range(HEAD_PAIRS):
            S[j] = jnp.transpose(S[j])
        for j in range(HEAD_PAIRS):
            owkv_ref[0, 2 * j] = S[j, 0:HEAD_DIM, 0:HEAD_DIM]
            owkv_ref[0, 2 * j + 1] = S[j, HEAD_DIM:PAIR_DIM, HEAD_DIM:PAIR_DIM]


def _even_prompt(p, st_pool, st_shift, st_wkv, prm, start):
    B, T, _ = p.shape
    tt = 256
    bt = lambda b, i: (b, i, 0)
    bs3 = lambda b, i: (b, 0, 0)
    bs4 = lambda b, i: (b, 0, 0, 0)
    c2 = lambda b, i: (0, 0)
    vec = lambda n: pl.BlockSpec((1, n), c2)
    scr = lambda: pltpu.VMEM((tt, RWKV_DIM), F32)
    n_units = (tt // WKV_CHUNK) * HEAD_PAIRS
    return pl.pallas_call(
        functools.partial(_even_prompt_kernel, tt=tt, start=start),
        grid=(B, T // tt),
        in_specs=[pl.BlockSpec((1, tt, EVEN_PROJ), bt),
                  pl.BlockSpec((1, POOL_BUF, POOL_DIM), bs3),
                  pl.BlockSpec((1, 1, RWKV_PROJ), bs3),
                  pl.BlockSpec((1, RWKV_HEADS, HEAD_DIM, HEAD_DIM), bs4),
                  vec(RWKV_PROJ), vec(RWKV_DIM), pl.BlockSpec((128, RWKV_DIM), c2), vec(RWKV_DIM),
                  pl.BlockSpec((128, RWKV_DIM), c2), pl.BlockSpec((128, RWKV_DIM), c2),
                  vec(RWKV_DIM), vec(RWKV_DIM), vec(RWKV_DIM), vec(RWKV_DIM), vec(RWKV_DIM),
                  pl.BlockSpec((SEG_TILE, SEG_TILE), c2), pl.BlockSpec((POOL_DIM, POOL_DIM), c2),
                  vec(POOL_DIM), pl.BlockSpec((WKV_CHUNK, WKV_CHUNK), c2)],
        out_specs=[pl.BlockSpec((1, tt, D_MODEL), bt),
                   pl.BlockSpec((1, POOL_BUF, POOL_DIM), bs3),
                   pl.BlockSpec((1, 1, RWKV_PROJ), bs3),
                   pl.BlockSpec((1, RWKV_HEADS, HEAD_DIM, HEAD_DIM), bs4)],
        out_shape=[jax.ShapeDtypeStruct((B, T, D_MODEL), BF16),
                   jax.ShapeDtypeStruct((B, POOL_BUF, POOL_DIM), F32),
                   jax.ShapeDtypeStruct((B, 1, RWKV_PROJ), F32),
                   jax.ShapeDtypeStruct((B, RWKV_HEADS, HEAD_DIM, HEAD_DIM), F32)],
        scratch_shapes=[pltpu.VMEM((16, POOL_DIM), F32), pltpu.VMEM((1, RWKV_PROJ), F32),
                        pltpu.VMEM((HEAD_PAIRS, PAIR_DIM, PAIR_DIM), F32),
                        scr(), scr(), scr(), scr(), scr(), scr(), scr(),
                        pltpu.VMEM((n_units, 2 * PAIR_DIM, PAIR_DIM), BF16),
                        pltpu.VMEM((n_units, 2 * PAIR_DIM, PAIR_DIM), F32),
                        pltpu.VMEM((n_units, PAIR_DIM, PAIR_DIM), F32),
                        pltpu.VMEM((n_units, PAIR_DIM, PAIR_DIM), BF16),
                        pltpu.VMEM((n_units, PAIR_DIM, PAIR_DIM), F32)],
        compiler_params=pltpu.CompilerParams(dimension_semantics=("arbitrary", "arbitrary"),
                                             vmem_limit_bytes=VMEM_LIMIT),
        name="even_prompt",
    )(p, st_pool, st_shift, st_wkv, *prm)


def _odd_prompt_kernel(q_ref, stc_ref, stl_ref,
                       lng_ref, lnb_ref, ws_ref, bias_ref, cw_ref, cb_ref, wx_ref, bx_ref, wa_ref, ba_ref,
                       lam_ref, y_ref, oconv_ref, olru_ref, hconv, hl, mix_s, *, tt):
    i = pl.program_id(1)
    nt = pl.num_programs(1)

    @pl.when(i == 0)
    def _init():
        hconv[0:5, :] = jnp.zeros((5, LRU_DIM), F32)
        hconv[5:8, :] = stc_ref[0]
        hl[...] = stl_ref[0]

    q = q_ref[0]
    rows = lax.broadcasted_iota(jnp.int32, (tt, 1), 0)

    u, vn = _gmlp_pre(q[:, 0:2 * GMLP_DIM], lng_ref[...], lnb_ref[...])
    rr = lax.broadcasted_iota(jnp.int32, (CHUNK, CHUNK), 0)
    cc = lax.broadcasted_iota(jnp.int32, (CHUNK, CHUNK), 1)
    causal = cc <= rr
    for h in range(GMLP_HEADS):
        wm = jnp.where(causal, ws_ref[h], 0.0).astype(BF16)
        ls = slice(h * CHUNK, (h + 1) * CHUNK)
        for c in range(tt // CHUNK):
            rs = slice(c * CHUNK, (c + 1) * CHUNK)
            mix_s[rs, ls] = jnp.dot(wm, vn[rs, ls].astype(BF16), preferred_element_type=F32) + bias_ref[:, ls]
    y_ref[0, :, 0:GMLP_DIM] = (u * mix_s[...]).astype(BF16)

    gate_in = q[:, 2 * GMLP_DIM:2 * GMLP_DIM + LRU_DIM]
    xr = q[:, 2 * GMLP_DIM + LRU_DIM:ODD_PROJ]
    ext = jnp.concatenate([hconv[...], xr], axis=0)
    xc = ext[8:, :] * cw_ref[3:4, :] + cb_ref[...]
    for j in range(1, CONV_WIDTH):
        xc = xc + pltpu.roll(ext, j, 0)[8:, :] * cw_ref[3 - j:4 - j, :]
    hconv[...] = ext[tt:tt + 8, :]
    a, b = _lru_gates(xc, wx_ref[...], bx_ref[...], wa_ref[...], ba_ref[...], lam_ref[...])
    dist = 1
    while dist < tt:
        keep = rows >= dist
        a_sh = jnp.where(keep, pltpu.roll(a, dist, 0), 1.0)
        b_sh = jnp.where(keep, pltpu.roll(b, dist, 0), 0.0)
        b = a * b_sh + b
        a = a * a_sh
        dist *= 2
    h = a * hl[...] + b
    hl[...] = h[tt - 1:tt, :]
    y_ref[0, :, GMLP_DIM:D_MODEL] = (h * _gelu(gate_in)).astype(BF16)

    @pl.when(i == nt - 1)
    def _fin():
        oconv_ref[0] = hconv[5:8, :]
        olru_ref[0] = hl[...]


def _odd_prompt(q, st_conv, st_lru, prm):
    B, T, _ = q.shape
    tt = 256
    bt = lambda b, i: (b, i, 0)
    bs3 = lambda b, i: (b, 0, 0)
    c2 = lambda b, i: (0, 0)
    c3 = lambda b, i: (0, 0, 0)
    vec = lambda n: pl.BlockSpec((1, n), c2)
    return pl.pallas_call(
        functools.partial(_odd_prompt_kernel, tt=tt),
        grid=(B, T // tt),
        in_specs=[pl.BlockSpec((1, tt, ODD_PROJ), bt),
                  pl.BlockSpec((1, CONV_WIDTH - 1, LRU_DIM), bs3),
                  pl.BlockSpec((1, 1, LRU_DIM), bs3),
                  vec(GMLP_DIM), vec(GMLP_DIM),
                  pl.BlockSpec((GMLP_HEADS, CHUNK, CHUNK), c3),
                  pl.BlockSpec((CHUNK, GMLP_DIM), c2),
                  pl.BlockSpec((CONV_WIDTH, LRU_DIM), c2), vec(LRU_DIM),
                  pl.BlockSpec((LRU_DIM, LRU_DIM), c2), vec(LRU_DIM),
                  pl.BlockSpec((LRU_DIM, LRU_DIM), c2), vec(LRU_DIM), vec(LRU_DIM)],
        out_specs=[pl.BlockSpec((1, tt, D_MODEL), bt),
                   pl.BlockSpec((1, CONV_WIDTH - 1, LRU_DIM), bs3),
                   pl.BlockSpec((1, 1, LRU_DIM), bs3)],
        out_shape=[jax.ShapeDtypeStruct((B, T, D_MODEL), BF16),
                   jax.ShapeDtypeStruct((B, CONV_WIDTH - 1, LRU_DIM), F32),
                   jax.ShapeDtypeStruct((B, 1, LRU_DIM), F32)],
        scratch_shapes=[pltpu.VMEM((8, LRU_DIM), F32), pltpu.VMEM((1, LRU_DIM), F32),
                        pltpu.VMEM((tt, GMLP_DIM), F32)],
        compiler_params=pltpu.CompilerParams(dimension_semantics=("arbitrary", "arbitrary"),
                                             vmem_limit_bytes=VMEM_LIMIT),
        name="odd_prompt",
    )(q, st_conv, st_lru, *prm)


def _even_sample_pre_kernel(p_ref, stp_ref, sts_ref,
                            mu_ref, w0_ref, wdec_ref, a0_ref, wa_ref, gw2_ref, kk_ref, ka_ref, rk_ref,
                            eseg_ref, poolw_ref, pools_ref,
                            r_ref, w_ref, kkn_ref, kka_ref, kp_ref, v_ref, g_ref, bonus_ref, ya_ref,
                            opool_ref, oshift_ref, *, T, start):
    prm = (mu_ref[...], w0_ref[...], wdec_ref[...], a0_ref[...], wa_ref[...], gw2_ref[...],
           kk_ref[...], ka_ref[...], rk_ref[...], eseg_ref[...])
    full = [stp_ref[s] for s in range(POOL_BUF)] + [p_ref[t][:, 0:POOL_DIM] for t in range(T)]
    wl = _pool_window_lanes()
    for t in range(T):
        P = p_ref[t][:, POOL_DIM:EVEN_PROJ]
        Pprev = sts_ref[...] if t == 0 else p_ref[t - 1][:, POOL_DIM:EVEN_PROJ]
        r, kp, v, ld, kk, a, g, bonus = _rwkv_pointwise(P, Pprev, prm)
        r_ref[t] = jnp.transpose(r)
        w_ref[t] = jnp.transpose(jnp.exp(ld))
        kkn_ref[t] = jnp.transpose(kk)
        kka_ref[t] = jnp.transpose(kk * a)
        kp_ref[t] = jnp.transpose(kp)
        v_ref[t] = jnp.transpose(v)
        g_ref[t] = g
        bonus_ref[t] = bonus
        e = POOL_BUF + t
        s2 = full[e] + full[e - 1]
        s4 = s2 + full[e - 2] + full[e - 3]
        s8 = s4 + full[e - 4] + full[e - 5] + full[e - 6] + full[e - 7]
        s16 = s8
        for s in range(8, 16):
            s16 = s16 + full[e - s]
        sel = _pool_lane_select(s2, s4, s8, s16)
        cnt = jnp.minimum(wl, start + t + 1).astype(F32)
        d = sel / cnt - full[e]
        ya_ref[t] = _bdot(d, poolw_ref[...]) * pools_ref[...]
    for s in range(POOL_BUF):
        opool_ref[s] = full[T + s]
    oshift_ref[...] = p_ref[T - 1][:, POOL_DIM:EVEN_PROJ]


def _even_sample_pre(p, st_pool, st_shift, prm, start):
    T, B, _ = p.shape
    cm = jax.ShapeDtypeStruct((T, RWKV_DIM, B), F32)
    bm = jax.ShapeDtypeStruct((T, B, RWKV_DIM), F32)
    return pl.pallas_call(
        functools.partial(_even_sample_pre_kernel, T=T, start=start),
        out_shape=[cm] * 6 + [bm] * 2 + [jax.ShapeDtypeStruct((T, B, POOL_DIM), F32),
                                   jax.ShapeDtypeStruct((POOL_BUF, B, POOL_DIM), F32),
                                   jax.ShapeDtypeStruct((B, RWKV_PROJ), F32)],
        compiler_params=pltpu.CompilerParams(vmem_limit_bytes=VMEM_LIMIT),
        name="even_sample_pre",
    )(p, st_pool, st_shift, *prm)


def _wkv_sample_kernel(r_ref, w_ref, kk_ref, kka_ref, kp_ref, v_ref, s_ref, o_ref, so_ref, *, T):
    group = range(WKV_SAMPLE_GROUP)

    def body(ib, carry):
        v0 = pl.multiple_of(ib * WKV_SAMPLE_GROUP, WKV_SAMPLE_GROUP)
        blk = pl.ds(v0, WKV_SAMPLE_GROUP)
        S = [s_ref[0, v0 + u] for u in group]
        for t in range(T):
            kk, w, kka, kp, r = kk_ref[t], w_ref[t], kka_ref[t], kp_ref[t], r_ref[t]
            vv = v_ref[t, blk, :]
            sk = [jnp.sum(S[u] * kk, axis=0, keepdims=True) for u in group]
            S = [S[u] * w - sk[u] * kka + vv[u:u + 1, :] * kp for u in group]
            o_ref[t, blk, :] = jnp.concatenate(
                [jnp.sum(S[u] * r, axis=0, keepdims=True) for u in group], axis=0)
        for u in group:
            so_ref[0, v0 + u] = S[u]
        return carry

    lax.fori_loop(0, HEAD_DIM // WKV_SAMPLE_GROUP, body, 0)


def _wkv_sample(r, w, kk, kka, kp, v, s):
    T, _, B = r.shape
    row_spec = pl.BlockSpec((T, HEAD_DIM, B), lambda h: (0, h, 0))
    st_spec = pl.BlockSpec((1, HEAD_DIM, HEAD_DIM, B), lambda h: (h, 0, 0, 0))
    return pl.pallas_call(
        functools.partial(_wkv_sample_kernel, T=T),
        grid=(RWKV_HEADS,),
        in_specs=[row_spec] * 6 + [st_spec],
        out_specs=[row_spec, st_spec],
        out_shape=[jax.ShapeDtypeStruct((T, RWKV_DIM, B), F32),
                   jax.ShapeDtypeStruct((RWKV_HEADS, HEAD_DIM, HEAD_DIM, B), F32)],
        compiler_params=pltpu.CompilerParams(dimension_semantics=("arbitrary",),
                                             vmem_limit_bytes=VMEM_LIMIT),
        name="wkv_sample",
    )(r, w, kk, kka, kp, v, s)


def _even_sample_post_kernel(o_ref, bonus_ref, g_ref, ya_ref, gng_ref, gnb_ref, eseg_ref, y_ref, *, T):
    for t in range(T):
        o = jnp.transpose(o_ref[t])
        yb = _rwkv_post(o, bonus_ref[t], g_ref[t], gng_ref[...], gnb_ref[...], eseg_ref[...])
        y_ref[t, :, 0:POOL_DIM] = ya_ref[t].astype(BF16)
        y_ref[t, :, POOL_DIM:D_MODEL] = yb.astype(BF16)


def _even_sample_post(o, bonus, g, ya, gn_g, gn_b, e_seg):
    T, _, B = o.shape
    return pl.pallas_call(
        functools.partial(_even_sample_post_kernel, T=T),
        out_shape=jax.ShapeDtypeStruct((T, B, D_MODEL), BF16),
        compiler_params=pltpu.CompilerParams(vmem_limit_bytes=VMEM_LIMIT),
        name="even_sample_post",
    )(o, bonus, g, ya, gn_g, gn_b, e_seg)


def _odd_sample_kernel(q_ref, stc_ref, stl_ref,
                       lng_ref, lnb_ref, wsm_ref, bsm_ref, cw_ref, cb_ref, wx_ref, bx_ref, wa_ref, ba_ref,
                       lam_ref, y_ref, v_ref, oconv_ref, olru_ref, *, T):
    vns = []
    us = []
    for t in range(T):
        u, vn = _gmlp_pre(q_ref[t][:, 0:2 * GMLP_DIM], lng_ref[...], lnb_ref[...])
        us.append(u)
        vns.append(vn)
        v_ref[t] = vn
    full = [stc_ref[s] for s in range(CONV_WIDTH - 1)] + \
           [q_ref[t][:, 2 * GMLP_DIM + LRU_DIM:ODD_PROJ] for t in range(T)]
    h = stl_ref[...]
    for t in range(T):
        mix = bsm_ref[t:t + 1, :]
        for j in range(t + 1):
            mix = mix + wsm_ref[t * T + j:t * T + j + 1, :] * vns[j]
        y_ref[t, :, 0:GMLP_DIM] = (us[t] * mix).astype(BF16)
        xc = full[t + CONV_WIDTH - 1] * cw_ref[CONV_WIDTH - 1:CONV_WIDTH, :] + cb_ref[...]
        for j in range(CONV_WIDTH - 1):
            xc = xc + full[t + j] * cw_ref[j:j + 1, :]
        a, b = _lru_gates(xc, wx_ref[...], bx_ref[...], wa_ref[...], ba_ref[...], lam_ref[...])
        h = a * h + b
        gate_in = q_ref[t][:, 2 * GMLP_DIM:2 * GMLP_DIM + LRU_DIM]
        y_ref[t, :, GMLP_DIM:D_MODEL] = (h * _gelu(gate_in)).astype(BF16)
    for s in range(CONV_WIDTH - 1):
        oconv_ref[s] = full[T + s]
    olru_ref[...] = h


def _odd_sample(q, st_conv, st_lru, prm):
    T, B, _ = q.shape
    return pl.pallas_call(
        functools.partial(_odd_sample_kernel, T=T),
        out_shape=[jax.ShapeDtypeStruct((T, B, D_MODEL), BF16),
                   jax.ShapeDtypeStruct((T, B, GMLP_DIM), F32),
                   jax.ShapeDtypeStruct((CONV_WIDTH - 1, B, LRU_DIM), F32),
                   jax.ShapeDtypeStruct((B, LRU_DIM), F32)],
        compiler_params=pltpu.CompilerParams(vmem_limit_bytes=VMEM_LIMIT),
        name="odd_sample",
    )(q, st_conv, st_lru, *prm)


def _block_diag(w):
    n, c, d = w.shape
    eye = jnp.eye(n, dtype=w.dtype)
    return (eye[:, None, :, None] * w[:, :, None, :]).reshape(n * c, n * d)


def _row(x):
    return x.reshape(1, -1)


def kernel(x_prompt, x_sample, state_pool, state_shift, state_wkv, state_conv, state_lru, ev_norm_g, ev_w_in, pool_w, pool_scale, rwkv_mu, rwkv_w0, rwkv_w_w2, rwkv_a0, rwkv_a_w2, rwkv_g_w2, rwkv_k_k, rwkv_k_a, rwkv_r_k, rwkv_gn_g, rwkv_gn_b, ev_w_out, od_norm_g, od_w_in, gmlp_ln_g, gmlp_ln_b, gmlp_ws, gmlp_bs, lru_conv_w, lru_conv_b, lru_wx, lru_bx, lru_wa, lru_ba, lru_lam, od_w_out, ff_norm_g, ff_w1, ff_w2, final_norm_g):
    B, T, _ = x_prompt.shape
    DB, DT, _ = x_sample.shape
    past_len = 16384

    seg_ids = jnp.arange(SEG_TILE) // HEAD_DIM
    e_seg = (seg_ids[:, None] == seg_ids[None, :]).astype(BF16)
    tri = (jnp.arange(WKV_CHUNK)[None, :] <= jnp.arange(WKV_CHUNK)[:, None]).astype(BF16)
    zlora = jnp.zeros((64, RWKV_DIM), F32)

    ev_common = (_row(rwkv_mu[0]), _row(rwkv_w0[0]),
                 jnp.concatenate([rwkv_w_w2[0], zlora], 0).astype(BF16), _row(rwkv_a0[0]),
                 jnp.concatenate([zlora, rwkv_a_w2[0]], 0).astype(BF16), rwkv_g_w2[0].astype(BF16),
                 _row(rwkv_k_k[0]), _row(rwkv_k_a[0]), _row(rwkv_r_k[0]))
    gn_g, gn_b = _row(rwkv_gn_g[0]), _row(rwkv_gn_b[0])
    pool_bd = _block_diag(pool_w[0]).astype(BF16)
    pool_sc = _row(pool_scale[0])
    w_in0 = ev_w_in[0].astype(BF16)
    g_in0 = _row(ev_norm_g[0])

    xp = x_prompt.reshape(B * T, D_MODEL)
    xs = jnp.transpose(x_sample, (1, 0, 2)).reshape(DT * DB, D_MODEL)

    pp = _inproj(xp, g_in0, w_in0).reshape(B, T, EVEN_PROJ)
    ps = _inproj(xs, g_in0, w_in0).reshape(DT, DB, EVEN_PROJ)

    yp, p_pool, p_shift, p_wkv = _even_prompt(
        pp, jnp.zeros((B, POOL_BUF, POOL_DIM), F32), jnp.zeros((B, 1, RWKV_PROJ), F32),
        jnp.zeros((B, RWKV_HEADS, HEAD_DIM, HEAD_DIM), F32),
        ev_common + (gn_g, gn_b, e_seg, pool_bd, pool_sc, tri), 0)

    pre = _even_sample_pre(ps, jnp.transpose(state_pool[0], (1, 0, 2)), state_shift[0],
                           ev_common + (e_seg, pool_bd, pool_sc), past_len)
    r_s, w_s, kk_s, kka_s, kp_s, v_s, g_s, bonus_s, ya_s, s_pool_tm, s_shift = pre
    o_s, s_wkv_bl = _wkv_sample(r_s, w_s, kk_s, kka_s, kp_s, v_s,
                                jnp.transpose(state_wkv[0], (1, 2, 3, 0)))
    ys = _even_sample_post(o_s, bonus_s, g_s, ya_s, gn_g, gn_b, e_seg)

    w_out0 = ev_w_out[0].astype(BF16)
    ffg = lambda l: _row(ff_norm_g[l])
    gfin = _row(final_norm_g)
    xp = _ffn(xp, yp.reshape(B * T, D_MODEL), w_out0, ffg(0), ff_w1[0].astype(BF16), ff_w2[0].astype(BF16),
              gfin, False)
    xs = _ffn(xs, ys.reshape(DT * DB, D_MODEL), w_out0, ffg(0), ff_w1[0].astype(BF16), ff_w2[0].astype(BF16),
              gfin, False)

    w_in1 = od_w_in[0].astype(BF16)
    g_in1 = _row(od_norm_g[0])
    qp = _inproj(xp, g_in1, w_in1).reshape(B, T, ODD_PROJ)
    qs = _inproj(xs, g_in1, w_in1).reshape(DT, DB, ODD_PROJ)

    lru_common = (lru_conv_w[0], _row(lru_conv_b[0]), _block_diag(lru_wx[0]).astype(BF16), _row(lru_bx[0]),
                  _block_diag(lru_wa[0]).astype(BF16), _row(lru_ba[0]), _row(lru_lam[0]))
    ln = (_row(gmlp_ln_g[0]), _row(gmlp_ln_b[0]))
    bias_full = jnp.repeat(jnp.transpose(gmlp_bs[0]), CHUNK, axis=1)
    yp, p_conv, p_lru = _odd_prompt(
        qp, jnp.zeros((B, CONV_WIDTH - 1, LRU_DIM), F32), jnp.zeros((B, 1, LRU_DIM), F32),
        ln + (gmlp_ws[0], bias_full) + lru_common)

    ws_small = jnp.repeat(jnp.transpose(gmlp_ws[0][:, :DT, :DT], (1, 2, 0)).reshape(DT * DT, GMLP_HEADS),
                          CHUNK, axis=1)
    ys, s_v, s_conv_tm, s_lru = _odd_sample(
        qs, jnp.transpose(state_conv[0], (1, 0, 2)), state_lru[0],
        ln + (ws_small, bias_full[:DT]) + lru_common)

    w_out1 = od_w_out[0].astype(BF16)
    xp = _ffn(xp, yp.reshape(B * T, D_MODEL), w_out1, ffg(1), ff_w1[1].astype(BF16), ff_w2[1].astype(BF16),
              gfin, True)
    xs = _ffn(xs, ys.reshape(DT * DB, D_MODEL), w_out1, ffg(1), ff_w1[1].astype(BF16), ff_w2[1].astype(BF16),
              gfin, True)

    tm2bm = lambda t: jnp.transpose(t, (1, 0, 2))
    y_prompt = xp.reshape(B, T, D_MODEL)
    y_sample = tm2bm(xs.reshape(DT, DB, D_MODEL))
    return (y_prompt, y_sample,
            p_pool[None], p_shift.reshape(1, B, RWKV_PROJ), p_wkv[None],
            p_conv[None], p_lru.reshape(1, B, LRU_DIM),
            tm2bm(s_pool_tm)[None], s_shift[None],
            jnp.transpose(s_wkv_bl, (3, 0, 1, 2))[None],
            tm2bm(s_conv_tm)[None], s_lru[None], tm2bm(s_v)[None])
```

```python
import functools

import jax
import jax.numpy as jnp
from jax import lax
from jax.experimental import pallas as pl
from jax.experimental.pallas import tpu as pltpu

F32 = jnp.float32
BF16 = jnp.bfloat16

D_MODEL = 1024
NORM_EPS = 1e-6
D_FF = 4 * D_MODEL

POOL_WINDOWS = (2, 4, 8, 16)
POOL_GROUP_DIM = 64
POOL_DIM = 256
POOL_BUF = 15

HEAD_DIM = 64
RWKV_DIM = 768
RWKV_HEADS = 12
HEAD_PAIRS = RWKV_HEADS // 2
PAIR_DIM = 2 * HEAD_DIM
RWKV_PROJ = 2560
RWKV_GN_EPS = 64e-5
EXP_NEG_HALF = 0.6065306597126334
EVEN_PROJ = POOL_DIM + RWKV_PROJ
LORA_OFF = 3 * RWKV_DIM
GATE_OFF = LORA_OFF + 128

CHUNK = 128
GMLP_DIM = 512
GMLP_HEADS = 4
LN_EPS = 1e-5
LRU_DIM = 512
CONV_WIDTH = 4
LRU_C = 8.0
ODD_PROJ = 2048

WKV_CHUNK = 64
SEG_TILE = 256
LANES = 128
SCAN_GROUP = 8
WKV_PREP_CHUNKS = 2
WKV_SAMPLE_GROUP = 8

VMEM_LIMIT = 48 * 1024 * 1024


def _bdot(a, b):
    return jnp.dot(a.astype(BF16), b.astype(BF16), preferred_element_type=F32)


def _bdot_nt(a, b):
    return lax.dot_general(a.astype(BF16), b.astype(BF16), (((1,), (1,)), ((), ())),
                           preferred_element_type=F32)


def _split3(x):
    hi = x.astype(BF16)
    r1 = x - hi.astype(F32)
    mid = r1.astype(BF16)
    lo = (r1 - mid.astype(F32)).astype(BF16)
    return hi, mid, lo


def _exact_dot_rhs01(x, e):
    hi = x.astype(BF16)
    lo = (x - hi.astype(F32)).astype(BF16)
    d = lambda t: jnp.dot(t, e, preferred_element_type=F32)
    return d(hi) + d(lo)


def _exact_dot_lhs01(e, x):
    hi, mid, lo = _split3(x)
    d = lambda t: jnp.dot(e, t, preferred_element_type=F32)
    return d(hi) + d(mid) + d(lo)


def _segsum(x, e_seg):
    parts = [_exact_dot_rhs01(x[:, g * SEG_TILE:(g + 1) * SEG_TILE], e_seg)
             for g in range(RWKV_DIM // SEG_TILE)]
    return jnp.concatenate(parts, axis=1)


def _softplus(z):
    return jnp.maximum(z, 0.0) + jnp.log(1.0 + jnp.exp(-jnp.abs(z)))


def _sigmoid(z):
    return 0.5 * jnp.tanh(0.5 * z) + 0.5


def _gelu(z):
    return 0.5 * z * (1.0 + jnp.tanh(0.7978845608028654 * (z + 0.044715 * (z * z * z))))


def _rmsnorm(x, g):
    ms = jnp.mean(x * x, axis=-1, keepdims=True)
    return x * lax.rsqrt(ms + NORM_EPS) * g


def _rwkv_pointwise(P, Pprev, prm):
    (mu, w0, wdec, a0, wa, gw2, k_k, k_a, r_k, e_seg) = prm
    xs = P + (Pprev - P) * mu
    r = xs[:, 0:RWKV_DIM]
    k = xs[:, RWKV_DIM:2 * RWKV_DIM]
    v = xs[:, 2 * RWKV_DIM:3 * RWKV_DIM]
    c_wa = xs[:, LORA_OFF:GATE_OFF]
    cg = xs[:, GATE_OFF:RWKV_PROJ]
    ld = -EXP_NEG_HALF * _sigmoid(w0 + _bdot(jnp.tanh(c_wa), wdec))
    a = _sigmoid(a0 + _bdot(c_wa, wa))
    g = _bdot(_sigmoid(cg), gw2)
    kk = k * k_k
    kk = kk * lax.rsqrt(jnp.maximum(_segsum(kk * kk, e_seg), 1e-24))
    kp = k * (1.0 + (a - 1.0) * k_a)
    bonus = _segsum(r * kp * r_k, e_seg) * v
    return r, kp, v, ld, kk, a, g, bonus


def _rwkv_post(o, bonus, g, gn_g, gn_b, e_seg):
    m = _segsum(o, e_seg) * (1.0 / HEAD_DIM)
    d = o - m
    var = _segsum(d * d, e_seg) * (1.0 / HEAD_DIM)
    on = d * lax.rsqrt(var + RWKV_GN_EPS) * gn_g + gn_b
    return (on + bonus) * g


def _pool_lane_select(s2, s4, s8, s16):
    lane = lax.broadcasted_iota(jnp.int32, (1, POOL_DIM), 1)
    return jnp.where(lane < 64, s2, jnp.where(lane < 128, s4, jnp.where(lane < 192, s8, s16)))


def _pool_window_lanes():
    lane = lax.broadcasted_iota(jnp.int32, (1, POOL_DIM), 1)
    return jnp.where(lane < 64, 2, jnp.where(lane < 128, 4, jnp.where(lane < 192, 8, 16)))


def _lru_gates(xc, wx, bx, wa, ba, lam):
    gx = _sigmoid(_bdot(xc, wx) + bx)
    ga = _sigmoid(_bdot(xc, wa) + ba)
    log_a = -LRU_C * ga * _softplus(-lam)
    a = jnp.exp(log_a)
    b = jnp.sqrt(-jnp.tanh(log_a) * (a * a + 1.0)) * gx * xc
    return a, b


def _gmlp_pre(zq, ln_g, ln_b):
    z = _gelu(zq)
    u = z[:, :GMLP_DIM]
    v = z[:, GMLP_DIM:]
    m = jnp.mean(v, axis=-1, keepdims=True)
    d = v - m
    var = jnp.mean(d * d, axis=-1, keepdims=True)
    return u, d * lax.rsqrt(var + LN_EPS) * ln_g + ln_b


def _inproj_kernel(x_ref, g_ref, w_ref, o_ref):
    h = _rmsnorm(x_ref[...], g_ref[...])
    o_ref[...] = jnp.dot(h.astype(BF16), w_ref[...], preferred_element_type=F32)


def _inproj(x, g, w):
    m, n = x.shape[0], w.shape[1]
    tm = 512
    return pl.pallas_call(
        _inproj_kernel,
        grid=(m // tm,),
        in_specs=[pl.BlockSpec((tm, D_MODEL), lambda i: (i, 0)),
                  pl.BlockSpec((1, D_MODEL), lambda i: (0, 0)),
                  pl.BlockSpec((D_MODEL, n), lambda i: (0, 0))],
        out_specs=pl.BlockSpec((tm, n), lambda i: (i, 0)),
        out_shape=jax.ShapeDtypeStruct((m, n), F32),
        compiler_params=pltpu.CompilerParams(dimension_semantics=("arbitrary",),
                                             vmem_limit_bytes=VMEM_LIMIT),
        name="inproj",
    )(x, g, w)


def _ffn_kernel(x_ref, y_ref, wo_ref, g_ref, w1_ref, w2_ref, gf_ref, o_ref, *, final):
    x1 = x_ref[...] + jnp.dot(y_ref[...], wo_ref[...], preferred_element_type=F32)
    hf = _rmsnorm(x1, g_ref[...]).astype(BF16)
    acc = x1
    fc = 1024
    for c in range(D_FF // fc):
        h = jnp.dot(hf, w1_ref[:, c * fc:(c + 1) * fc], preferred_element_type=F32)
        h = jnp.square(jnp.maximum(h, 0.0)).astype(BF16)
        acc = acc + jnp.dot(h, w2_ref[c * fc:(c + 1) * fc, :], preferred_element_type=F32)
    if final:
        acc = _rmsnorm(acc, gf_ref[...])
    o_ref[...] = acc


def _ffn(x, y, wo, g, w1, w2, gf, final):
    m = x.shape[0]
    tm = 512
    const = lambda i: (0, 0)
    return pl.pallas_call(
        functools.partial(_ffn_kernel, final=final),
        grid=(m // tm,),
        in_specs=[pl.BlockSpec((tm, D_MODEL), lambda i: (i, 0)),
                  pl.BlockSpec((tm, D_MODEL), lambda i: (i, 0)),
                  pl.BlockSpec((D_MODEL, D_MODEL), const, pipeline_mode=pl.Buffered(1)),
                  pl.BlockSpec((1, D_MODEL), const),
                  pl.BlockSpec((D_MODEL, D_FF), const, pipeline_mode=pl.Buffered(1)),
                  pl.BlockSpec((D_FF, D_MODEL), const, pipeline_mode=pl.Buffered(1)),
                  pl.BlockSpec((1, D_MODEL), const)],
        out_specs=pl.BlockSpec((tm, D_MODEL), lambda i: (i, 0)),
        out_shape=jax.ShapeDtypeStruct((m, D_MODEL), F32),
        compiler_params=pltpu.CompilerParams(dimension_semantics=("arbitrary",),
                                             vmem_limit_bytes=VMEM_LIMIT),
        name="ffn",
    )(x, y, wo, g, w1, w2, gf)


def _even_prompt_kernel(p_ref, stp_ref, sts_ref, stw_ref,
                        mu_ref, w0_ref, wdec_ref, a0_ref, wa_ref, gw2_ref, kk_ref, ka_ref, rk_ref,
                        gng_ref, gnb_ref, eseg_ref, poolw_ref, pools_ref, tri_ref,
                        y_ref, opool_ref, oshift_ref, owkv_ref,
                        hpool, hshift, S, r_s, kp_s, v_s, ld_s, kk_s, a_s, o_s,
                        lhs_b, add_b, vk_b, bend_b, pend_b, *, tt, start):
    i = pl.program_id(1)
    nt = pl.num_programs(1)
    C = WKV_CHUNK

    @pl.when(i == 0)
    def _init():
        hpool[0:1, :] = jnp.zeros((1, POOL_DIM), F32)
        hpool[1:16, :] = stp_ref[0]
        hshift[...] = sts_ref[0]
        S[...] = jnp.zeros(S.shape, F32)
        for j in range(HEAD_PAIRS):
            S[j, 0:HEAD_DIM, 0:HEAD_DIM] = stw_ref[0, 2 * j]
            S[j, HEAD_DIM:PAIR_DIM, HEAD_DIM:PAIR_DIM] = stw_ref[0, 2 * j + 1]
        for j in range(HEAD_PAIRS):
            S[j] = jnp.transpose(S[j])

    p = p_ref[0]
    rows = lax.broadcasted_iota(jnp.int32, (tt, 1), 0)

    u = p[:, 0:POOL_DIM]
    ext = jnp.concatenate([hpool[...], u], axis=0)
    s2 = ext + pltpu.roll(ext, 1, 0)
    s4 = s2 + pltpu.roll(s2, 2, 0)
    s8 = s4 + pltpu.roll(s4, 4, 0)
    s16 = s8 + pltpu.roll(s8, 8, 0)
    sel = _pool_lane_select(s2, s4, s8, s16)[16:, :]
    pos = start + i * tt + rows
    cnt = jnp.minimum(_pool_window_lanes(), pos + 1).astype(F32)
    d = sel / cnt - u
    y_ref[0, :, 0:POOL_DIM] = (_bdot(d, poolw_ref[...]) * pools_ref[...]).astype(BF16)
    hpool[...] = ext[tt:tt + 16, :]

    P = p[:, POOL_DIM:EVEN_PROJ]
    Pprev = jnp.where(rows == 0, hshift[...], pltpu.roll(P, 1, 0))
    hshift[...] = P[tt - 1:tt, :]
    e_seg = eseg_ref[...]
    prm = (mu_ref[...], w0_ref[...], wdec_ref[...], a0_ref[...], wa_ref[...], gw2_ref[...],
           kk_ref[...], ka_ref[...], rk_ref[...], e_seg)
    r, kp, v, ld, kk, a, g, bonus = _rwkv_pointwise(P, Pprev, prm)
    r_s[...] = r
    kp_s[...] = kp
    v_s[...] = v
    ld_s[...] = ld
    kk_s[...] = kk
    a_s[...] = a

    rr = lax.broadcasted_iota(jnp.int32, (PAIR_DIM, PAIR_DIM), 0)
    cc = lax.broadcasted_iota(jnp.int32, (PAIR_DIM, PAIR_DIM), 1)
    same_head = (rr >= C) == (cc >= C)
    strict = same_head & (cc < rr)
    incl = same_head & (cc <= rr)
    eye = (rr == cc).astype(F32)
    lane_c = lax.broadcasted_iota(jnp.int32, (C, PAIR_DIM), 1)
    head0 = lane_c < HEAD_DIM
    tri = tri_ref[...]
    pairs = range(HEAD_PAIRS)

    def stack_heads(x):
        z = jnp.zeros_like(x)
        return jnp.concatenate([jnp.where(head0, x, z), jnp.where(head0, z, x)], axis=0)

    def prepare(cg):
        qa_sm, qr_sm, v_sm, rhs_g, kb_src, slot = [], [], [], [], [], []
        for ci in range(WKV_PREP_CHUNKS):
            c = cg * WKV_PREP_CHUNKS + ci
            sl = slice(c * C, (c + 1) * C)
            R = r_s[sl, :]
            K = kp_s[sl, :]
            V = v_s[sl, :]
            LD = ld_s[sl, :]
            KK = kk_s[sl, :]
            KA = KK * a_s[sl, :]
            L = _exact_dot_lhs01(tri, LD)
            Lend = L[C - 1:C, :]
            enL = jnp.exp(-L)
            eE = jnp.exp(Lend - L)
            Qr = R * jnp.exp(L)
            Qa = KK * jnp.exp(L - LD)
            Kt = K * enL
            Bt = KA * enL
            Kend = K * eE
            Bend = KA * eE
            Pend = jnp.exp(Lend)
            for j in pairs:
                ls = slice(j * PAIR_DIM, (j + 1) * PAIR_DIM)
                qa_sm.append(stack_heads(Qa[:, ls]))
                qr_sm.append(stack_heads(Qr[:, ls]))
                v_sm.append(stack_heads(V[:, ls]).astype(BF16))
                bt = Bt[:, ls].astype(BF16)
                kt = Kt[:, ls].astype(BF16)
                rhs_g.append(jnp.concatenate([bt, bt, kt, kt], axis=0))
                kb_src.append((Kend[:, ls], Bend[:, ls], Pend[:, ls]))
                slot.append(c * HEAD_PAIRS + j)
        units = range(len(slot))
        G = [_bdot_nt(jnp.concatenate([qa_sm[u], qr_sm[u]], axis=0), rhs_g[u]) for u in units]
        yield
        Y = [jnp.where(strict, -G[u][0:2 * C, 0:2 * C], 0.0) for u in units]
        Aak = [jnp.where(strict, G[u][0:2 * C, 2 * C:4 * C], 0.0) for u in units]
        Arb = [jnp.where(incl, G[u][2 * C:4 * C, 0:2 * C], 0.0).astype(BF16) for u in units]
        Ark = [jnp.where(incl, G[u][2 * C:4 * C, 2 * C:4 * C], 0.0) for u in units]
        X = [_bdot(Y[u], Y[u]) for u in units]
        Tm = [eye + Y[u] for u in units]
        yield
        power = 2
        while 2 * power < C:
            for u in units:
                xb = X[u].astype(BF16)
                P2 = jnp.dot(xb, jnp.concatenate([xb, Tm[u].astype(BF16)], axis=1), preferred_element_type=F32)
                X[u] = P2[:, 0:PAIR_DIM]
                Tm[u] = Tm[u] + P2[:, PAIR_DIM:2 * PAIR_DIM]
            power *= 2
            yield
        for u in units:
            Tm[u] = Tm[u] + _bdot(X[u], Tm[u])
        yield
        AV = [_bdot(jnp.concatenate([Aak[u], Ark[u]], axis=0), v_sm[u]) for u in units]
        yield
        TQ = [_bdot(Tm[u], jnp.concatenate([qa_sm[u], AV[u][0:2 * C]], axis=1)) for u in units]
        yield
        AT = [jnp.dot(Arb[u], TQ[u].astype(BF16), preferred_element_type=F32) for u in units]
        yield
        for u in units:
            kend, bend, pend = kb_src[u]
            kb_t = jnp.transpose(jnp.concatenate(
                [stack_heads(kend), stack_heads(bend), jnp.broadcast_to(pend, (PAIR_DIM, PAIR_DIM))],
                axis=1))
            lhs_b[slot[u], 0:2 * C, :] = TQ[u][:, 0:PAIR_DIM].astype(BF16)
            lhs_b[slot[u], 2 * C:4 * C, :] = (qr_sm[u] - AT[u][:, 0:PAIR_DIM]).astype(BF16)
            add_b[slot[u], 0:2 * C, :] = TQ[u][:, PAIR_DIM:2 * PAIR_DIM]
            add_b[slot[u], 2 * C:4 * C, :] = AV[u][2 * C:4 * C] - AT[u][:, PAIR_DIM:2 * PAIR_DIM]
            vk_b[slot[u]] = _bdot(kb_t[0:PAIR_DIM], v_sm[u])
            bend_b[slot[u]] = kb_t[PAIR_DIM:2 * PAIR_DIM].astype(BF16)
            pend_b[slot[u]] = kb_t[2 * PAIR_DIM:3 * PAIR_DIM]

    def advance(cg):
        for ci in range(WKV_PREP_CHUNKS):
            c = cg * WKV_PREP_CHUNKS + ci
            UO = [jnp.dot(lhs_b[c * HEAD_PAIRS + j], S[j].astype(BF16), preferred_element_type=F32)
                  + add_b[c * HEAD_PAIRS + j] for j in pairs]
            yield
            for j in pairs:
                u = c * HEAD_PAIRS + j
                S[j] = pend_b[u] * S[j] + vk_b[u] - jnp.dot(bend_b[u], UO[j][0:2 * C].astype(BF16),
                                                            preferred_element_type=F32)
                o_s[c * C:(c + 1) * C, j * PAIR_DIM:(j + 1) * PAIR_DIM] = UO[j][2 * C:3 * C] + UO[j][3 * C:4 * C]
            yield

    def run_interleaved(*stages):
        live = list(stages)
        while live:
            for gen in list(live):
                if next(gen, StopIteration) is StopIteration:
                    live.remove(gen)

    n_groups = tt // (C * WKV_PREP_CHUNKS)
    run_interleaved(prepare(0))
    for cg in range(1, n_groups):
        run_interleaved(prepare(cg), advance(cg - 1))
    run_interleaved(advance(n_groups - 1))

    yb = _rwkv_post(o_s[...], bonus, g, gng_ref[...], gnb_ref[...], e_seg)
    y_ref[0, :, POOL_DIM:D_MODEL] = yb.astype(BF16)

    @pl.when(i == nt - 1)
    def _fin():
        opool_ref[0] = hpool[1:16, :]
        oshift_ref[0] = hshift[...]
        for j in range(HEAD_PAIRS):
            S[j] = jnp.transpose(S[j])
        for j in range(HEAD_PAIRS):
            owkv_ref[0, 2 * j] = S[j, 0:HEAD_DIM, 0:HEAD_DIM]
            owkv_ref[0, 2 * j + 1] = S[j, HEAD_DIM:PAIR_DIM, HEAD_DIM:PAIR_DIM]


def _even_prompt(p, st_pool, st_shift, st_wkv, prm, start):
    B, T, _ = p.shape
    tt = 256
    bt = lambda b, i: (b, i, 0)
    bs3 = lambda b, i: (b, 0, 0)
    bs4 = lambda b, i: (b, 0, 0, 0)
    c2 = lambda b, i: (0, 0)
    vec = lambda n: pl.BlockSpec((1, n), c2)
    scr = lambda: pltpu.VMEM((tt, RWKV_DIM), F32)
    n_units = (tt // WKV_CHUNK) * HEAD_PAIRS
    return pl.pallas_call(
        functools.partial(_even_prompt_kernel, tt=tt, start=start),
        grid=(B, T // tt),
        in_specs=[pl.BlockSpec((1, tt, EVEN_PROJ), bt),
                  pl.BlockSpec((1, POOL_BUF, POOL_DIM), bs3),
                  pl.BlockSpec((1, 1, RWKV_PROJ), bs3),
                  pl.BlockSpec((1, RWKV_HEADS, HEAD_DIM, HEAD_DIM), bs4),
                  vec(RWKV_PROJ), vec(RWKV_DIM), pl.BlockSpec((128, RWKV_DIM), c2), vec(RWKV_DIM),
                  pl.BlockSpec((128, RWKV_DIM), c2), pl.BlockSpec((128, RWKV_DIM), c2),
                  vec(RWKV_DIM), vec(RWKV_DIM), vec(RWKV_DIM), vec(RWKV_DIM), vec(RWKV_DIM),
                  pl.BlockSpec((SEG_TILE, SEG_TILE), c2), pl.BlockSpec((POOL_DIM, POOL_DIM), c2),
                  vec(POOL_DIM), pl.BlockSpec((WKV_CHUNK, WKV_CHUNK), c2)],
        out_specs=[pl.BlockSpec((1, tt, D_MODEL), bt),
                   pl.BlockSpec((1, POOL_BUF, POOL_DIM), bs3),
                   pl.BlockSpec((1, 1, RWKV_PROJ), bs3),
                   pl.BlockSpec((1, RWKV_HEADS, HEAD_DIM, HEAD_DIM), bs4)],
        out_shape=[jax.ShapeDtypeStruct((B, T, D_MODEL), BF16),
                   jax.ShapeDtypeStruct((B, POOL_BUF, POOL_DIM), F32),
                   jax.ShapeDtypeStruct((B, 1, RWKV_PROJ), F32),
                   jax.ShapeDtypeStruct((B, RWKV_HEADS, HEAD_DIM, HEAD_DIM), F32)],
        scratch_shapes=[pltpu.VMEM((16, POOL_DIM), F32), pltpu.VMEM((1, RWKV_PROJ), F32),
                        pltpu.VMEM((HEAD_PAIRS, PAIR_DIM, PAIR_DIM), F32),
                        scr(), scr(), scr(), scr(), scr(), scr(), scr(),
                        pltpu.VMEM((n_units, 2 * PAIR_DIM, PAIR_DIM), BF16),
                        pltpu.VMEM((n_units, 2 * PAIR_DIM, PAIR_DIM), F32),
                        pltpu.VMEM((n_units, PAIR_DIM, PAIR_DIM), F32),
                        pltpu.VMEM((n_units, PAIR_DIM, PAIR_DIM), BF16),
                        pltpu.VMEM((n_units, PAIR_DIM, PAIR_DIM), F32)],
        compiler_params=pltpu.CompilerParams(dimension_semantics=("arbitrary", "arbitrary"),
                                             vmem_limit_bytes=VMEM_LIMIT),
        name="even_prompt",
    )(p, st_pool, st_shift, st_wkv, *prm)


def _odd_prompt_kernel(q_ref, stc_ref, stl_ref,
                       lng_ref, lnb_ref, ws_ref, bias_ref, cw_ref, cb_ref, wx_ref, bx_ref, wa_ref,
                       ba_ref, lam_ref, y_ref, oconv_ref, olru_ref, hconv, hl, mix_s, *, tt):
    i = pl.program_id(1)
    nt = pl.num_programs(1)

    @pl.when(i == 0)
    def _init():
        hconv[0:5, :] = jnp.zeros((5, LRU_DIM), F32)
        hconv[5:8, :] = stc_ref[0]
        hl[...] = stl_ref[0]

    q = q_ref[0]
    rows = lax.broadcasted_iota(jnp.int32, (tt, 1), 0)

    u, vn = _gmlp_pre(q[:, 0:2 * GMLP_DIM], lng_ref[...], lnb_ref[...])
    rr = lax.broadcasted_iota(jnp.int32, (CHUNK, CHUNK), 0)
    cc = lax.broadcasted_iota(jnp.int32, (CHUNK, CHUNK), 1)
    causal = cc <= rr
    for h in range(GMLP_HEADS):
        wm = jnp.where(causal, ws_ref[h], 0.0).astype(BF16)
        ls = slice(h * CHUNK, (h + 1) * CHUNK)
        for c in range(tt // CHUNK):
            rs = slice(c * CHUNK, (c + 1) * CHUNK)
            mix_s[rs, ls] = jnp.dot(wm, vn[rs, ls].astype(BF16), preferred_element_type=F32) + bias_ref[:, ls]
    y_ref[0, :, 0:GMLP_DIM] = (u * mix_s[...]).astype(BF16)

    gate_in = q[:, 2 * GMLP_DIM:2 * GMLP_DIM + LRU_DIM]
    xr = q[:, 2 * GMLP_DIM + LRU_DIM:ODD_PROJ]
    hconv[8:8 + tt, :] = xr
    xc = xr * cw_ref[3:4, :] + cb_ref[...]
    for j in range(1, CONV_WIDTH):
        xc = xc + hconv[8 - j:8 - j + tt, :] * cw_ref[3 - j:4 - j, :]
    hconv[0:8, :] = hconv[tt:tt + 8, :]
    a, b = _lru_gates(xc, wx_ref[...], bx_ref[...], wa_ref[...], ba_ref[...], lam_ref[...])
    n_groups = tt // SCAN_GROUP
    a = a.reshape(n_groups, SCAN_GROUP, LRU_DIM)
    b = b.reshape(n_groups, SCAN_GROUP, LRU_DIM)
    in_group = lax.broadcasted_iota(jnp.int32, (1, SCAN_GROUP, 1), 1)
    dist = 1
    while dist < SCAN_GROUP:
        keep = in_group >= dist
        a_sh = jnp.where(keep, pltpu.roll(a, dist, 1), 1.0)
        b_sh = jnp.where(keep, pltpu.roll(b, dist, 1), 0.0)
        b = a * b_sh + b
        a = a * a_sh
        dist *= 2
    carry = hl[...]
    groups = []
    for gi in range(n_groups):
        hg = a[gi] * carry + b[gi]
        groups.append(hg)
        carry = hg[SCAN_GROUP - 1:SCAN_GROUP, :]
    h = jnp.concatenate(groups, axis=0)
    hl[...] = carry
    y_ref[0, :, GMLP_DIM:D_MODEL] = (h * _gelu(gate_in)).astype(BF16)

    @pl.when(i == nt - 1)
    def _fin():
        oconv_ref[0] = hconv[5:8, :]
        olru_ref[0] = hl[...]


def _odd_prompt(q, st_conv, st_lru, prm):
    B, T, _ = q.shape
    tt = 256
    bt = lambda b, i: (b, i, 0)
    bs3 = lambda b, i: (b, 0, 0)
    c2 = lambda b, i: (0, 0)
    c3 = lambda b, i: (0, 0, 0)
    vec = lambda n: pl.BlockSpec((1, n), c2)
    return pl.pallas_call(
        functools.partial(_odd_prompt_kernel, tt=tt),
        grid=(B, T // tt),
        in_specs=[pl.BlockSpec((1, tt, ODD_PROJ), bt),
                  pl.BlockSpec((1, CONV_WIDTH - 1, LRU_DIM), bs3),
                  pl.BlockSpec((1, 1, LRU_DIM), bs3),
                  vec(GMLP_DIM), vec(GMLP_DIM),
                  pl.BlockSpec((GMLP_HEADS, CHUNK, CHUNK), c3),
                  pl.BlockSpec((CHUNK, GMLP_DIM), c2),
                  pl.BlockSpec((CONV_WIDTH, LRU_DIM), c2), vec(LRU_DIM),
                  pl.BlockSpec((LRU_DIM, LRU_DIM), c2), vec(LRU_DIM),
                  pl.BlockSpec((LRU_DIM, LRU_DIM), c2), vec(LRU_DIM), vec(LRU_DIM)],
        out_specs=[pl.BlockSpec((1, tt, D_MODEL), bt),
                   pl.BlockSpec((1, CONV_WIDTH - 1, LRU_DIM), bs3),
                   pl.BlockSpec((1, 1, LRU_DIM), bs3)],
        out_shape=[jax.ShapeDtypeStruct((B, T, D_MODEL), BF16),
                   jax.ShapeDtypeStruct((B, CONV_WIDTH - 1, LRU_DIM), F32),
                   jax.ShapeDtypeStruct((B, 1, LRU_DIM), F32)],
        scratch_shapes=[pltpu.VMEM((8 + tt, LRU_DIM), F32), pltpu.VMEM((1, LRU_DIM), F32),
                        pltpu.VMEM((tt, GMLP_DIM), F32)],
        compiler_params=pltpu.CompilerParams(dimension_semantics=("arbitrary", "arbitrary"),
                                             vmem_limit_bytes=VMEM_LIMIT),
        name="odd_prompt",
    )(q, st_conv, st_lru, *prm)


def _even_sample_pre_kernel(p_ref, stp_ref, sts_ref,
                            mu_ref, w0_ref, wdec_ref, a0_ref, wa_ref, gw2_ref, kk_ref, ka_ref, rk_ref,
                            eseg_ref, poolw_ref, pools_ref,
                            r_ref, w_ref, kkn_ref, kka_ref, kp_ref, v_ref, g_ref, bonus_ref, ya_ref,
                            opool_ref, oshift_ref, *, T, start):
    prm = (mu_ref[...], w0_ref[...], wdec_ref[...], a0_ref[...], wa_ref[...], gw2_ref[...],
           kk_ref[...], ka_ref[...], rk_ref[...], eseg_ref[...])
    full = [stp_ref[s] for s in range(POOL_BUF)] + [p_ref[t][:, 0:POOL_DIM] for t in range(T)]
    wl = _pool_window_lanes()
    for t in range(T):
        P = p_ref[t][:, POOL_DIM:EVEN_PROJ]
        Pprev = sts_ref[...] if t == 0 else p_ref[t - 1][:, POOL_DIM:EVEN_PROJ]
        r, kp, v, ld, kk, a, g, bonus = _rwkv_pointwise(P, Pprev, prm)
        r_ref[t] = jnp.transpose(r)
        w_ref[t] = jnp.transpose(jnp.exp(ld))
        kkn_ref[t] = jnp.transpose(kk)
        kka_ref[t] = jnp.transpose(kk * a)
        kp_ref[t] = jnp.transpose(kp)
        v_ref[t] = jnp.transpose(v)
        g_ref[t] = g
        bonus_ref[t] = bonus
        e = POOL_BUF + t
        s2 = full[e] + full[e - 1]
        s4 = s2 + full[e - 2] + full[e - 3]
        s8 = s4 + full[e - 4] + full[e - 5] + full[e - 6] + full[e - 7]
        s16 = s8
        for s in range(8, 16):
            s16 = s16 + full[e - s]
        sel = _pool_lane_select(s2, s4, s8, s16)
        cnt = jnp.minimum(wl, start + t + 1).astype(F32)
        d = sel / cnt - full[e]
        ya_ref[t] = _bdot(d, poolw_ref[...]) * pools_ref[...]
    for s in range(POOL_BUF):
        opool_ref[s] = full[T + s]
    oshift_ref[...] = p_ref[T - 1][:, POOL_DIM:EVEN_PROJ]


def _even_sample_pre(p, st_pool, st_shift, prm, start):
    T, B, _ = p.shape
    cm = jax.ShapeDtypeStruct((T, RWKV_DIM, B), F32)
    bm = jax.ShapeDtypeStruct((T, B, RWKV_DIM), F32)
    return pl.pallas_call(
        functools.partial(_even_sample_pre_kernel, T=T, start=start),
        out_shape=[cm] * 6 + [bm] * 2 + [jax.ShapeDtypeStruct((T, B, POOL_DIM), F32),
                                   jax.ShapeDtypeStruct((POOL_BUF, B, POOL_DIM), F32),
                                   jax.ShapeDtypeStruct((B, RWKV_PROJ), F32)],
        compiler_params=pltpu.CompilerParams(vmem_limit_bytes=VMEM_LIMIT),
        name="even_sample_pre",
    )(p, st_pool, st_shift, *prm)


def _wkv_sample_kernel(r_ref, w_ref, kk_ref, kka_ref, kp_ref, v_ref, s_ref, o_ref, so_ref, *, T):
    group = range(WKV_SAMPLE_GROUP)

    def body(ib, carry):
        v0 = pl.multiple_of(ib * WKV_SAMPLE_GROUP, WKV_SAMPLE_GROUP)
        blk = pl.ds(v0, WKV_SAMPLE_GROUP)
        S = [s_ref[0, v0 + u] for u in group]
        for t in range(T):
            kk, w, kka, kp, r = kk_ref[t], w_ref[t], kka_ref[t], kp_ref[t], r_ref[t]
            vv = v_ref[t, blk, :]
            sk = [jnp.sum(S[u] * kk, axis=0, keepdims=True) for u in group]
            S = [S[u] * w - sk[u] * kka + vv[u:u + 1, :] * kp for u in group]
            o_ref[t, blk, :] = jnp.concatenate(
                [jnp.sum(S[u] * r, axis=0, keepdims=True) for u in group], axis=0)
        for u in group:
            so_ref[0, v0 + u] = S[u]
        return carry

    lax.fori_loop(0, HEAD_DIM // WKV_SAMPLE_GROUP, body, 0)


def _wkv_sample(r, w, kk, kka, kp, v, s):
    T, _, B = r.shape
    row_spec = pl.BlockSpec((T, HEAD_DIM, B), lambda h: (0, h, 0))
    st_spec = pl.BlockSpec((1, HEAD_DIM, HEAD_DIM, B), lambda h: (h, 0, 0, 0))
    return pl.pallas_call(
        functools.partial(_wkv_sample_kernel, T=T),
        grid=(RWKV_HEADS,),
        in_specs=[row_spec] * 6 + [st_spec],
        out_specs=[row_spec, st_spec],
        out_shape=[jax.ShapeDtypeStruct((T, RWKV_DIM, B), F32),
                   jax.ShapeDtypeStruct((RWKV_HEADS, HEAD_DIM, HEAD_DIM, B), F32)],
        compiler_params=pltpu.CompilerParams(dimension_semantics=("arbitrary",),
                                             vmem_limit_bytes=VMEM_LIMIT),
        name="wkv_sample",
    )(r, w, kk, kka, kp, v, s)


def _even_sample_post_kernel(o_ref, bonus_ref, g_ref, ya_ref, gng_ref, gnb_ref, eseg_ref, y_ref, *, T):
    for t in range(T):
        o = jnp.transpose(o_ref[t])
        yb = _rwkv_post(o, bonus_ref[t], g_ref[t], gng_ref[...], gnb_ref[...], eseg_ref[...])
        y_ref[t, :, 0:POOL_DIM] = ya_ref[t].astype(BF16)
        y_ref[t, :, POOL_DIM:D_MODEL] = yb.astype(BF16)


def _even_sample_post(o, bonus, g, ya, gn_g, gn_b, e_seg):
    T, _, B = o.shape
    return pl.pallas_call(
        functools.partial(_even_sample_post_kernel, T=T),
        out_shape=jax.ShapeDtypeStruct((T, B, D_MODEL), BF16),
        compiler_params=pltpu.CompilerParams(vmem_limit_bytes=VMEM_LIMIT),
        name="even_sample_post",
    )(o, bonus, g, ya, gn_g, gn_b, e_seg)


def _odd_sample_kernel(q_ref, stc_ref, stl_ref,
                       lng_ref, lnb_ref, wsm_ref, bsm_ref, cw_ref, cb_ref, wx_ref, bx_ref, wa_ref,
                       ba_ref, lam_ref, y_ref, v_ref, oconv_ref, olru_ref, *, T):
    vns = []
    us = []
    for t in range(T):
        u, vn = _gmlp_pre(q_ref[t][:, 0:2 * GMLP_DIM], lng_ref[...], lnb_ref[...])
        us.append(u)
        vns.append(vn)
        v_ref[t] = vn
    full = [stc_ref[s] for s in range(CONV_WIDTH - 1)] + \
           [q_ref[t][:, 2 * GMLP_DIM + LRU_DIM:ODD_PROJ] for t in range(T)]
    h = stl_ref[...]
    for t in range(T):
        mix = bsm_ref[t:t + 1, :]
        for j in range(t + 1):
            mix = mix + wsm_ref[t * T + j:t * T + j + 1, :] * vns[j]
        y_ref[t, :, 0:GMLP_DIM] = (us[t] * mix).astype(BF16)
        xc = full[t + CONV_WIDTH - 1] * cw_ref[CONV_WIDTH - 1:CONV_WIDTH, :] + cb_ref[...]
        for j in range(CONV_WIDTH - 1):
            xc = xc + full[t + j] * cw_ref[j:j + 1, :]
        a, b = _lru_gates(xc, wx_ref[...], bx_ref[...], wa_ref[...], ba_ref[...], lam_ref[...])
        h = a * h + b
        gate_in = q_ref[t][:, 2 * GMLP_DIM:2 * GMLP_DIM + LRU_DIM]
        y_ref[t, :, GMLP_DIM:D_MODEL] = (h * _gelu(gate_in)).astype(BF16)
    for s in range(CONV_WIDTH - 1):
        oconv_ref[s] = full[T + s]
    olru_ref[...] = h


def _odd_sample(q, st_conv, st_lru, prm):
    T, B, _ = q.shape
    return pl.pallas_call(
        functools.partial(_odd_sample_kernel, T=T),
        out_shape=[jax.ShapeDtypeStruct((T, B, D_MODEL), BF16),
                   jax.ShapeDtypeStruct((T, B, GMLP_DIM), F32),
                   jax.ShapeDtypeStruct((CONV_WIDTH - 1, B, LRU_DIM), F32),
                   jax.ShapeDtypeStruct((B, LRU_DIM), F32)],
        compiler_params=pltpu.CompilerParams(vmem_limit_bytes=VMEM_LIMIT),
        name="odd_sample",
    )(q, st_conv, st_lru, *prm)


def _block_diag(w):
    n, c, d = w.shape
    eye = jnp.eye(n, dtype=w.dtype)
    return (eye[:, None, :, None] * w[:, :, None, :]).reshape(n * c, n * d)


def _row(x):
    return x.reshape(1, -1)


def kernel(x_prompt, x_sample, state_pool, state_shift, state_wkv, state_conv, state_lru, ev_norm_g, ev_w_in, pool_w, pool_scale, rwkv_mu, rwkv_w0, rwkv_w_w2, rwkv_a0, rwkv_a_w2, rwkv_g_w2, rwkv_k_k, rwkv_k_a, rwkv_r_k, rwkv_gn_g, rwkv_gn_b, ev_w_out, od_norm_g, od_w_in, gmlp_ln_g, gmlp_ln_b, gmlp_ws, gmlp_bs, lru_conv_w, lru_conv_b, lru_wx, lru_bx, lru_wa, lru_ba, lru_lam, od_w_out, ff_norm_g, ff_w1, ff_w2, final_norm_g):
    B, T, _ = x_prompt.shape
    DB, DT, _ = x_sample.shape
    past_len = 16384

    seg_ids = jnp.arange(SEG_TILE) // HEAD_DIM
    e_seg = (seg_ids[:, None] == seg_ids[None, :]).astype(BF16)
    tri = (jnp.arange(WKV_CHUNK)[None, :] <= jnp.arange(WKV_CHUNK)[:, None]).astype(BF16)
    zlora = jnp.zeros((64, RWKV_DIM), F32)

    ev_common = (_row(rwkv_mu[0]), _row(rwkv_w0[0]),
                 jnp.concatenate([rwkv_w_w2[0], zlora], 0).astype(BF16), _row(rwkv_a0[0]),
                 jnp.concatenate([zlora, rwkv_a_w2[0]], 0).astype(BF16), rwkv_g_w2[0].astype(BF16),
                 _row(rwkv_k_k[0]), _row(rwkv_k_a[0]), _row(rwkv_r_k[0]))
    gn_g, gn_b = _row(rwkv_gn_g[0]), _row(rwkv_gn_b[0])
    pool_bd = _block_diag(pool_w[0]).astype(BF16)
    pool_sc = _row(pool_scale[0])
    w_in0 = ev_w_in[0].astype(BF16)
    g_in0 = _row(ev_norm_g[0])

    xp = x_prompt.reshape(B * T, D_MODEL)
    xs = jnp.transpose(x_sample, (1, 0, 2)).reshape(DT * DB, D_MODEL)

    pp = _inproj(xp, g_in0, w_in0).reshape(B, T, EVEN_PROJ)
    ps = _inproj(xs, g_in0, w_in0).reshape(DT, DB, EVEN_PROJ)

    yp, p_pool, p_shift, p_wkv = _even_prompt(
        pp, jnp.zeros((B, POOL_BUF, POOL_DIM), F32), jnp.zeros((B, 1, RWKV_PROJ), F32),
        jnp.zeros((B, RWKV_HEADS, HEAD_DIM, HEAD_DIM), F32),
        ev_common + (gn_g, gn_b, e_seg, pool_bd, pool_sc, tri), 0)

    pre = _even_sample_pre(ps, jnp.transpose(state_pool[0], (1, 0, 2)), state_shift[0],
                           ev_common + (e_seg, pool_bd, pool_sc), past_len)
    r_s, w_s, kk_s, kka_s, kp_s, v_s, g_s, bonus_s, ya_s, s_pool_tm, s_shift = pre
    o_s, s_wkv_bl = _wkv_sample(r_s, w_s, kk_s, kka_s, kp_s, v_s,
                                jnp.transpose(state_wkv[0], (1, 2, 3, 0)))
    ys = _even_sample_post(o_s, bonus_s, g_s, ya_s, gn_g, gn_b, e_seg)

    w_out0 = ev_w_out[0].astype(BF16)
    ffg = lambda l: _row(ff_norm_g[l])
    gfin = _row(final_norm_g)
    xp = _ffn(xp, yp.reshape(B * T, D_MODEL), w_out0, ffg(0), ff_w1[0].astype(BF16), ff_w2[0].astype(BF16),
              gfin, False)
    xs = _ffn(xs, ys.reshape(DT * DB, D_MODEL), w_out0, ffg(0), ff_w1[0].astype(BF16), ff_w2[0].astype(BF16),
              gfin, False)

    w_in1 = od_w_in[0].astype(BF16)
    g_in1 = _row(od_norm_g[0])
    qp = _inproj(xp, g_in1, w_in1).reshape(B, T, ODD_PROJ)
    qs = _inproj(xs, g_in1, w_in1).reshape(DT, DB, ODD_PROJ)

    lru_common = (lru_conv_w[0], _row(lru_conv_b[0]), _block_diag(lru_wx[0]).astype(BF16), _row(lru_bx[0]),
                  _block_diag(lru_wa[0]).astype(BF16), _row(lru_ba[0]), _row(lru_lam[0]))
    ln = (_row(gmlp_ln_g[0]), _row(gmlp_ln_b[0]))
    bias_full = jnp.repeat(jnp.transpose(gmlp_bs[0]), CHUNK, axis=1)
    yp, p_conv, p_lru = _odd_prompt(
        qp, jnp.zeros((B, CONV_WIDTH - 1, LRU_DIM), F32), jnp.zeros((B, 1, LRU_DIM), F32),
        ln + (gmlp_ws[0], bias_full) + lru_common)

    ws_small = jnp.repeat(jnp.transpose(gmlp_ws[0][:, :DT, :DT], (1, 2, 0)).reshape(DT * DT, GMLP_HEADS),
                          CHUNK, axis=1)
    ys, s_v, s_conv_tm, s_lru = _odd_sample(
        qs, jnp.transpose(state_conv[0], (1, 0, 2)), state_lru[0],
        ln + (ws_small, bias_full[:DT]) + lru_common)

    w_out1 = od_w_out[0].astype(BF16)
    xp = _ffn(xp, yp.reshape(B * T, D_MODEL), w_out1, ffg(1), ff_w1[1].astype(BF16), ff_w2[1].astype(BF16),
              gfin, True)
    xs = _ffn(xs, ys.reshape(DT * DB, D_MODEL), w_out1, ffg(1), ff_w1[1].astype(BF16), ff_w2[1].astype(BF16),
              gfin, True)

    tm2bm = lambda t: jnp.transpose(t, (1, 0, 2))
    y_prompt = xp.reshape(B, T, D_MODEL)
    y_sample = tm2bm(xs.reshape(DT, DB, D_MODEL))
    return (y_prompt, y_sample,
            p_pool[None], p_shift.reshape(1, B, RWKV_PROJ), p_wkv[None],
            p_conv[None], p_lru.reshape(1, B, LRU_DIM),
            tm2bm(s_pool_tm)[None], s_shift[None],
            jnp.transpose(s_wkv_bl, (3, 0, 1, 2))[None],
            tm2bm(s_conv_tm)[None], s_lru[None], tm2bm(s_v)[None])
```

```python
import functools

import jax
import jax.numpy as jnp
from jax import lax
from jax.experimental import pallas as pl
from jax.experimental.pallas import tpu as pltpu

F32 = jnp.float32
BF16 = jnp.bfloat16

D_MODEL = 1024
NORM_EPS = 1e-6
D_FF = 4 * D_MODEL

POOL_WINDOWS = (2, 4, 8, 16)
POOL_GROUP_DIM = 64
POOL_DIM = 256
POOL_BUF = 15

HEAD_DIM = 64
RWKV_DIM = 768
RWKV_HEADS = 12
HEAD_PAIRS = RWKV_HEADS // 2
PAIR_DIM = 2 * HEAD_DIM
RWKV_PROJ = 2560
RWKV_GN_EPS = 64e-5
EXP_NEG_HALF = 0.6065306597126334
EVEN_PROJ = POOL_DIM + RWKV_PROJ
LORA_OFF = 3 * RWKV_DIM
GATE_OFF = LORA_OFF + 128

CHUNK = 128
GMLP_DIM = 512
GMLP_HEADS = 4
LN_EPS = 1e-5
LRU_DIM = 512
CONV_WIDTH = 4
LRU_C = 8.0
ODD_PROJ = 2048

WKV_CHUNK = 64
SEG_TILE = 256
LANES = 128
SCAN_GROUP = 8
WKV_PREP_CHUNKS = 2
WKV_SAMPLE_GROUP = 8

VMEM_LIMIT = 48 * 1024 * 1024


def _bdot(a, b):
    return jnp.dot(a.astype(BF16), b.astype(BF16), preferred_element_type=F32)


def _bdot_nt(a, b):
    return lax.dot_general(a.astype(BF16), b.astype(BF16), (((1,), (1,)), ((), ())),
                           preferred_element_type=F32)


def _split3(x):
    hi = x.astype(BF16)
    r1 = x - hi.astype(F32)
    mid = r1.astype(BF16)
    lo = (r1 - mid.astype(F32)).astype(BF16)
    return hi, mid, lo


def _exact_dot_rhs01(x, e):
    hi = x.astype(BF16)
    lo = (x - hi.astype(F32)).astype(BF16)
    d = lambda t: jnp.dot(t, e, preferred_element_type=F32)
    return d(hi) + d(lo)


def _exact_dot_lhs01(e, x):
    hi, mid, lo = _split3(x)
    d = lambda t: jnp.dot(e, t, preferred_element_type=F32)
    return d(hi) + d(mid) + d(lo)


def _segsum(x, e_seg):
    parts = [_exact_dot_rhs01(x[:, g * SEG_TILE:(g + 1) * SEG_TILE], e_seg)
             for g in range(RWKV_DIM // SEG_TILE)]
    return jnp.concatenate(parts, axis=1)


def _softplus(z):
    return jnp.maximum(z, 0.0) + jnp.log(1.0 + jnp.exp(-jnp.abs(z)))


def _sigmoid(z):
    return 0.5 * jnp.tanh(0.5 * z) + 0.5


def _gelu(z):
    return 0.5 * z * (1.0 + jnp.tanh(0.7978845608028654 * (z + 0.044715 * (z * z * z))))


def _rmsnorm(x, g):
    ms = jnp.mean(x * x, axis=-1, keepdims=True)
    return x * lax.rsqrt(ms + NORM_EPS) * g


def _rwkv_pointwise(P, Pprev, prm):
    (mu, w0, wdec, a0, wa, gw2, k_k, k_a, r_k, e_seg) = prm
    xs = P + (Pprev - P) * mu
    r = xs[:, 0:RWKV_DIM]
    k = xs[:, RWKV_DIM:2 * RWKV_DIM]
    v = xs[:, 2 * RWKV_DIM:3 * RWKV_DIM]
    c_wa = xs[:, LORA_OFF:GATE_OFF]
    cg = xs[:, GATE_OFF:RWKV_PROJ]
    ld = -EXP_NEG_HALF * _sigmoid(w0 + _bdot(jnp.tanh(c_wa), wdec))
    a = _sigmoid(a0 + _bdot(c_wa, wa))
    g = _bdot(_sigmoid(cg), gw2)
    kk = k * k_k
    kk = kk * lax.rsqrt(jnp.maximum(_segsum(kk * kk, e_seg), 1e-24))
    kp = k * (1.0 + (a - 1.0) * k_a)
    bonus = _segsum(r * kp * r_k, e_seg) * v
    return r, kp, v, ld, kk, a, g, bonus


def _rwkv_post(o, bonus, g, gn_g, gn_b, e_seg):
    m = _segsum(o, e_seg) * (1.0 / HEAD_DIM)
    d = o - m
    var = _segsum(d * d, e_seg) * (1.0 / HEAD_DIM)
    on = d * lax.rsqrt(var + RWKV_GN_EPS) * gn_g + gn_b
    return (on + bonus) * g


def _pool_lane_select(s2, s4, s8, s16):
    lane = lax.broadcasted_iota(jnp.int32, (1, POOL_DIM), 1)
    return jnp.where(lane < 64, s2, jnp.where(lane < 128, s4, jnp.where(lane < 192, s8, s16)))


def _pool_window_lanes():
    lane = lax.broadcasted_iota(jnp.int32, (1, POOL_DIM), 1)
    return jnp.where(lane < 64, 2, jnp.where(lane < 128, 4, jnp.where(lane < 192, 8, 16)))


def _lru_gates(xc, wx, bx, wa, ba, lam):
    gx = _sigmoid(_bdot(xc, wx) + bx)
    ga = _sigmoid(_bdot(xc, wa) + ba)
    log_a = -LRU_C * ga * _softplus(-lam)
    a = jnp.exp(log_a)
    b = jnp.sqrt(-jnp.tanh(log_a) * (a * a + 1.0)) * gx * xc
    return a, b


def _gmlp_pre(zq, ln_g, ln_b):
    z = _gelu(zq)
    u = z[:, :GMLP_DIM]
    v = z[:, GMLP_DIM:]
    m = jnp.mean(v, axis=-1, keepdims=True)
    d = v - m
    var = jnp.mean(d * d, axis=-1, keepdims=True)
    return u, d * lax.rsqrt(var + LN_EPS) * ln_g + ln_b


def _inproj_kernel(x_ref, g_ref, w_ref, o_ref):
    h = _rmsnorm(x_ref[...], g_ref[...])
    o_ref[...] = jnp.dot(h.astype(BF16), w_ref[...], preferred_element_type=F32)


def _inproj(x, g, w):
    m, n = x.shape[0], w.shape[1]
    tm = 512
    return pl.pallas_call(
        _inproj_kernel,
        grid=(m // tm,),
        in_specs=[pl.BlockSpec((tm, D_MODEL), lambda i: (i, 0)),
                  pl.BlockSpec((1, D_MODEL), lambda i: (0, 0)),
                  pl.BlockSpec((D_MODEL, n), lambda i: (0, 0))],
        out_specs=pl.BlockSpec((tm, n), lambda i: (i, 0)),
        out_shape=jax.ShapeDtypeStruct((m, n), F32),
        compiler_params=pltpu.CompilerParams(dimension_semantics=("arbitrary",),
                                             vmem_limit_bytes=VMEM_LIMIT),
        name="inproj",
    )(x, g, w)


def _ffn_kernel(x_ref, y_ref, wo_ref, g_ref, w1_ref, w2_ref, gf_ref, o_ref, *, final):
    x1 = x_ref[...] + jnp.dot(y_ref[...], wo_ref[...], preferred_element_type=F32)
    hf = _rmsnorm(x1, g_ref[...]).astype(BF16)
    acc = x1
    fc = 1024
    for c in range(D_FF // fc):
        h = jnp.dot(hf, w1_ref[:, c * fc:(c + 1) * fc], preferred_element_type=F32)
        h = jnp.square(jnp.maximum(h, 0.0)).astype(BF16)
        acc = acc + jnp.dot(h, w2_ref[c * fc:(c + 1) * fc, :], preferred_element_type=F32)
    if final:
        acc = _rmsnorm(acc, gf_ref[...])
    o_ref[...] = acc


def _ffn(x, y, wo, g, w1, w2, gf, final):
    m = x.shape[0]
    tm = 512
    const = lambda i: (0, 0)
    return pl.pallas_call(
        functools.partial(_ffn_kernel, final=final),
        grid=(m // tm,),
        in_specs=[pl.BlockSpec((tm, D_MODEL), lambda i: (i, 0)),
                  pl.BlockSpec((tm, D_MODEL), lambda i: (i, 0)),
                  pl.BlockSpec((D_MODEL, D_MODEL), const, pipeline_mode=pl.Buffered(1)),
                  pl.BlockSpec((1, D_MODEL), const),
                  pl.BlockSpec((D_MODEL, D_FF), const, pipeline_mode=pl.Buffered(1)),
                  pl.BlockSpec((D_FF, D_MODEL), const, pipeline_mode=pl.Buffered(1)),
                  pl.BlockSpec((1, D_MODEL), const)],
        out_specs=pl.BlockSpec((tm, D_MODEL), lambda i: (i, 0)),
        out_shape=jax.ShapeDtypeStruct((m, D_MODEL), F32),
        compiler_params=pltpu.CompilerParams(dimension_semantics=("arbitrary",),
                                             vmem_limit_bytes=VMEM_LIMIT),
        name="ffn",
    )(x, y, wo, g, w1, w2, gf)


def _even_prompt_kernel(p_ref, stp_ref, sts_ref, stw_ref,
                        mu_ref, w0_ref, wdec_ref, a0_ref, wa_ref, gw2_ref, kk_ref, ka_ref, rk_ref,
                        gng_ref, gnb_ref, eseg_ref, poolw_ref, pools_ref, tri_ref,
                        y_ref, opool_ref, oshift_ref, owkv_ref,
                        hpool, hshift, S, r_s, kp_s, v_s, ld_s, kk_s, a_s, o_s,
                        lhs_b, add_b, vk_b, bend_b, pend_b, *, tt, start):
    i = pl.program_id(1)
    nt = pl.num_programs(1)
    C = WKV_CHUNK

    @pl.when(i == 0)
    def _init():
        hpool[0:1, :] = jnp.zeros((1, POOL_DIM), F32)
        hpool[1:16, :] = stp_ref[0]
        hshift[...] = sts_ref[0]
        S[...] = jnp.zeros(S.shape, F32)
        for j in range(HEAD_PAIRS):
            S[j, 0:HEAD_DIM, 0:HEAD_DIM] = stw_ref[0, 2 * j]
            S[j, HEAD_DIM:PAIR_DIM, HEAD_DIM:PAIR_DIM] = stw_ref[0, 2 * j + 1]
        for j in range(HEAD_PAIRS):
            S[j] = jnp.transpose(S[j])

    p = p_ref[0]
    rows = lax.broadcasted_iota(jnp.int32, (tt, 1), 0)

    u = p[:, 0:POOL_DIM]
    ext = jnp.concatenate([hpool[...], u], axis=0)
    s2 = ext + pltpu.roll(ext, 1, 0)
    s4 = s2 + pltpu.roll(s2, 2, 0)
    s8 = s4 + pltpu.roll(s4, 4, 0)
    s16 = s8 + pltpu.roll(s8, 8, 0)
    sel = _pool_lane_select(s2, s4, s8, s16)[16:, :]
    pos = start + i * tt + rows
    cnt = jnp.minimum(_pool_window_lanes(), pos + 1).astype(F32)
    d = sel / cnt - u
    y_ref[0, :, 0:POOL_DIM] = (_bdot(d, poolw_ref[...]) * pools_ref[...]).astype(BF16)
    hpool[...] = ext[tt:tt + 16, :]

    P = p[:, POOL_DIM:EVEN_PROJ]
    Pprev = jnp.where(rows == 0, hshift[...], pltpu.roll(P, 1, 0))
    hshift[...] = P[tt - 1:tt, :]
    e_seg = eseg_ref[...]
    prm = (mu_ref[...], w0_ref[...], wdec_ref[...], a0_ref[...], wa_ref[...], gw2_ref[...],
           kk_ref[...], ka_ref[...], rk_ref[...], e_seg)
    r, kp, v, ld, kk, a, g, bonus = _rwkv_pointwise(P, Pprev, prm)
    r_s[...] = r
    kp_s[...] = kp
    v_s[...] = v
    ld_s[...] = ld
    kk_s[...] = kk
    a_s[...] = a

    lane_c = lax.broadcasted_iota(jnp.int32, (C, PAIR_DIM), 1)
    row_c = lax.broadcasted_iota(jnp.int32, (C, PAIR_DIM), 0)
    head0 = lane_c < HEAD_DIM
    left = lane_c < C
    lo_strict = left & (lane_c < row_c)
    lo_incl = left & (lane_c <= row_c)
    hi_strict = jnp.logical_not(left) & (lane_c - C < row_c)
    hi_incl = jnp.logical_not(left) & (lane_c - C <= row_c)
    eye_r = (lane_c - C == row_c).astype(F32)
    zb = jnp.zeros((C, PAIR_DIM), BF16)
    zbw = jnp.zeros((C, 2 * PAIR_DIM), BF16)
    tri = tri_ref[...]
    pairs = range(HEAD_PAIRS)

    def stack_heads(x):
        z = jnp.zeros_like(x)
        return jnp.concatenate([jnp.where(head0, x, z), jnp.where(head0, z, x)], axis=0)

    def prepare(cg):
        qa_sm, qr_sm, v_sm, rhs_g, kb_src, slot = [], [], [], [], [], []
        for ci in range(WKV_PREP_CHUNKS):
            c = cg * WKV_PREP_CHUNKS + ci
            sl = slice(c * C, (c + 1) * C)
            R = r_s[sl, :]
            K = kp_s[sl, :]
            V = v_s[sl, :]
            LD = ld_s[sl, :]
            KK = kk_s[sl, :]
            KA = KK * a_s[sl, :]
            L = _exact_dot_lhs01(tri, LD)
            Lend = L[C - 1:C, :]
            enL = jnp.exp(-L)
            eE = jnp.exp(Lend - L)
            Qr = R * jnp.exp(L)
            Qa = KK * jnp.exp(L - LD)
            Kt = K * enL
            Bt = KA * enL
            Kend = K * eE
            Bend = KA * eE
            Pend = jnp.exp(Lend)
            for j in pairs:
                ls = slice(j * PAIR_DIM, (j + 1) * PAIR_DIM)
                qa_sm.append(stack_heads(Qa[:, ls]))
                qr_sm.append(stack_heads(Qr[:, ls]))
                v_sm.append(stack_heads(V[:, ls]).astype(BF16))
                bt = Bt[:, ls].astype(BF16)
                kt = Kt[:, ls].astype(BF16)
                rhs_g.append(jnp.concatenate([bt, kt], axis=0))
                kb_src.append((Kend[:, ls], Bend[:, ls], Pend[:, ls]))
                slot.append(c * HEAD_PAIRS + j)
        units = range(len(slot))
        heads = [(u, h) for u in units for h in range(2)]
        hrows = lambda x, h: x[h * C:(h + 1) * C]
        G = [_bdot_nt(jnp.concatenate([qa_sm[u], qr_sm[u]], axis=0), rhs_g[u]) for u in units]
        yield
        GA = [hrows(G[u], h) for u, h in heads]
        GR = [hrows(G[u], 2 + h) for u, h in heads]
        R = [jnp.where(lo_strict, -GA[k], eye_r) for k in range(len(heads))]
        level = 1
        while level < C:
            for k in range(len(heads)):
                rb = R[k].astype(BF16)
                P2 = jnp.dot(jnp.where(left, rb, zb), jnp.concatenate([rb, zb], axis=0),
                             preferred_element_type=F32)
                R[k] = P2 + jnp.where(left, 0.0, R[k])
            level *= 2
            yield
        AV = []
        for k, (u, h) in enumerate(heads):
            lhs = jnp.concatenate([jnp.where(hi_strict, GA[k], 0.0), jnp.where(hi_incl, GR[k], 0.0)], axis=0)
            AV.append(jnp.dot(lhs.astype(BF16), jnp.concatenate([zb, hrows(v_sm[u], h)], axis=0),
                              preferred_element_type=F32))
        yield
        TQ = []
        for k, (u, h) in enumerate(heads):
            rhs = jnp.concatenate([hrows(qa_sm[u], h), AV[k][0:C]], axis=1).astype(BF16)
            TQ.append(jnp.dot(jnp.where(left, zb, R[k].astype(BF16)), jnp.concatenate([zbw, rhs], axis=0),
                              preferred_element_type=F32))
        yield
        AT = []
        for k, (u, h) in enumerate(heads):
            AT.append(jnp.dot(jnp.where(lo_incl, GR[k], 0.0).astype(BF16),
                              jnp.concatenate([TQ[k].astype(BF16), zbw], axis=0), preferred_element_type=F32))
        yield
        for u in units:
            kend, bend, pend = kb_src[u]
            kb_t = jnp.transpose(jnp.concatenate(
                [stack_heads(kend), stack_heads(bend), jnp.broadcast_to(pend, (PAIR_DIM, PAIR_DIM))],
                axis=1))
            for h in range(2):
                k = 2 * u + h
                lhs_b[slot[u], h * C:(h + 1) * C, :] = TQ[k][:, 0:PAIR_DIM].astype(BF16)
                lhs_b[slot[u], (2 + h) * C:(3 + h) * C, :] = (
                    hrows(qr_sm[u], h) - AT[k][:, 0:PAIR_DIM]).astype(BF16)
                add_b[slot[u], h * C:(h + 1) * C, :] = TQ[k][:, PAIR_DIM:2 * PAIR_DIM]
                add_b[slot[u], (2 + h) * C:(3 + h) * C, :] = AV[k][C:2 * C] - AT[k][:, PAIR_DIM:2 * PAIR_DIM]
            vk_b[slot[u]] = _bdot(kb_t[0:PAIR_DIM], v_sm[u])
            bend_b[slot[u]] = kb_t[PAIR_DIM:2 * PAIR_DIM].astype(BF16)
            pend_b[slot[u]] = kb_t[2 * PAIR_DIM:3 * PAIR_DIM]

    def advance(cg):
        for ci in range(WKV_PREP_CHUNKS):
            c = cg * WKV_PREP_CHUNKS + ci
            UO = [jnp.dot(lhs_b[c * HEAD_PAIRS + j], S[j].astype(BF16), preferred_element_type=F32)
                  + add_b[c * HEAD_PAIRS + j] for j in pairs]
            yield
            for j in pairs:
                u = c * HEAD_PAIRS + j
                S[j] = pend_b[u] * S[j] + vk_b[u] - jnp.dot(bend_b[u], UO[j][0:2 * C].astype(BF16),
                                                            preferred_element_type=F32)
                o_s[c * C:(c + 1) * C, j * PAIR_DIM:(j + 1) * PAIR_DIM] = UO[j][2 * C:3 * C] + UO[j][3 * C:4 * C]
            yield

    def run_interleaved(*stages):
        live = list(stages)
        while live:
            for gen in list(live):
                if next(gen, StopIteration) is StopIteration:
                    live.remove(gen)

    n_groups = tt // (C * WKV_PREP_CHUNKS)
    run_interleaved(prepare(0))
    for cg in range(1, n_groups):
        run_interleaved(prepare(cg), advance(cg - 1))
    run_interleaved(advance(n_groups - 1))

    yb = _rwkv_post(o_s[...], bonus, g, gng_ref[...], gnb_ref[...], e_seg)
    y_ref[0, :, POOL_DIM:D_MODEL] = yb.astype(BF16)

    @pl.when(i == nt - 1)
    def _fin():
        opool_ref[0] = hpool[1:16, :]
        oshift_ref[0] = hshift[...]
        for j in range(HEAD_PAIRS):
            S[j] = jnp.transpose(S[j])
        for j in range(HEAD_PAIRS):
            owkv_ref[0, 2 * j] = S[j, 0:HEAD_DIM, 0:HEAD_DIM]
            owkv_ref[0, 2 * j + 1] = S[j, HEAD_DIM:PAIR_DIM, HEAD_DIM:PAIR_DIM]


def _even_prompt(p, st_pool, st_shift, st_wkv, prm, start):
    B, T, _ = p.shape
    tt = 256
    bt = lambda b, i: (b, i, 0)
    bs3 = lambda b, i: (b, 0, 0)
    bs4 = lambda b, i: (b, 0, 0, 0)
    c2 = lambda b, i: (0, 0)
    vec = lambda n: pl.BlockSpec((1, n), c2)
    scr = lambda: pltpu.VMEM((tt, RWKV_DIM), F32)
    n_units = (tt // WKV_CHUNK) * HEAD_PAIRS
    return pl.pallas_call(
        functools.partial(_even_prompt_kernel, tt=tt, start=start),
        grid=(B, T // tt),
        in_specs=[pl.BlockSpec((1, tt, EVEN_PROJ), bt),
                  pl.BlockSpec((1, POOL_BUF, POOL_DIM), bs3),
                  pl.BlockSpec((1, 1, RWKV_PROJ), bs3),
                  pl.BlockSpec((1, RWKV_HEADS, HEAD_DIM, HEAD_DIM), bs4),
                  vec(RWKV_PROJ), vec(RWKV_DIM), pl.BlockSpec((128, RWKV_DIM), c2), vec(RWKV_DIM),
                  pl.BlockSpec((128, RWKV_DIM), c2), pl.BlockSpec((128, RWKV_DIM), c2),
                  vec(RWKV_DIM), vec(RWKV_DIM), vec(RWKV_DIM), vec(RWKV_DIM), vec(RWKV_DIM),
                  pl.BlockSpec((SEG_TILE, SEG_TILE), c2), pl.BlockSpec((POOL_DIM, POOL_DIM), c2),
                  vec(POOL_DIM), pl.BlockSpec((WKV_CHUNK, WKV_CHUNK), c2)],
        out_specs=[pl.BlockSpec((1, tt, D_MODEL), bt),
                   pl.BlockSpec((1, POOL_BUF, POOL_DIM), bs3),
                   pl.BlockSpec((1, 1, RWKV_PROJ), bs3),
                   pl.BlockSpec((1, RWKV_HEADS, HEAD_DIM, HEAD_DIM), bs4)],
        out_shape=[jax.ShapeDtypeStruct((B, T, D_MODEL), BF16),
                   jax.ShapeDtypeStruct((B, POOL_BUF, POOL_DIM), F32),
                   jax.ShapeDtypeStruct((B, 1, RWKV_PROJ), F32),
                   jax.ShapeDtypeStruct((B, RWKV_HEADS, HEAD_DIM, HEAD_DIM), F32)],
        scratch_shapes=[pltpu.VMEM((16, POOL_DIM), F32), pltpu.VMEM((1, RWKV_PROJ), F32),
                        pltpu.VMEM((HEAD_PAIRS, PAIR_DIM, PAIR_DIM), F32),
                        scr(), scr(), scr(), scr(), scr(), scr(), scr(),
                        pltpu.VMEM((n_units, 2 * PAIR_DIM, PAIR_DIM), BF16),
                        pltpu.VMEM((n_units, 2 * PAIR_DIM, PAIR_DIM), F32),
                        pltpu.VMEM((n_units, PAIR_DIM, PAIR_DIM), F32),
                        pltpu.VMEM((n_units, PAIR_DIM, PAIR_DIM), BF16),
                        pltpu.VMEM((n_units, PAIR_DIM, PAIR_DIM), F32)],
        compiler_params=pltpu.CompilerParams(dimension_semantics=("arbitrary", "arbitrary"),
                                             vmem_limit_bytes=VMEM_LIMIT),
        name="even_prompt",
    )(p, st_pool, st_shift, st_wkv, *prm)


def _odd_prompt_kernel(q_ref, stc_ref, stl_ref,
                       lng_ref, lnb_ref, ws_ref, bias_ref, cw_ref, cb_ref, wx_ref, bx_ref, wa_ref,
                       ba_ref, lam_ref, y_ref, oconv_ref, olru_ref, hconv, hl, mix_s, *, tt):
    i = pl.program_id(1)
    nt = pl.num_programs(1)

    @pl.when(i == 0)
    def _init():
        hconv[0:5, :] = jnp.zeros((5, LRU_DIM), F32)
        hconv[5:8, :] = stc_ref[0]
        hl[...] = stl_ref[0]

    q = q_ref[0]
    rows = lax.broadcasted_iota(jnp.int32, (tt, 1), 0)

    u, vn = _gmlp_pre(q[:, 0:2 * GMLP_DIM], lng_ref[...], lnb_ref[...])
    rr = lax.broadcasted_iota(jnp.int32, (CHUNK, CHUNK), 0)
    cc = lax.broadcasted_iota(jnp.int32, (CHUNK, CHUNK), 1)
    causal = cc <= rr
    for h in range(GMLP_HEADS):
        wm = jnp.where(causal, ws_ref[h], 0.0).astype(BF16)
        ls = slice(h * CHUNK, (h + 1) * CHUNK)
        for c in range(tt // CHUNK):
            rs = slice(c * CHUNK, (c + 1) * CHUNK)
            mix_s[rs, ls] = jnp.dot(wm, vn[rs, ls].astype(BF16), preferred_element_type=F32) + bias_ref[:, ls]
    y_ref[0, :, 0:GMLP_DIM] = (u * mix_s[...]).astype(BF16)

    gate_in = q[:, 2 * GMLP_DIM:2 * GMLP_DIM + LRU_DIM]
    xr = q[:, 2 * GMLP_DIM + LRU_DIM:ODD_PROJ]
    hconv[8:8 + tt, :] = xr
    xc = xr * cw_ref[3:4, :] + cb_ref[...]
    for j in range(1, CONV_WIDTH):
        xc = xc + hconv[8 - j:8 - j + tt, :] * cw_ref[3 - j:4 - j, :]
    hconv[0:8, :] = hconv[tt:tt + 8, :]
    a, b = _lru_gates(xc, wx_ref[...], bx_ref[...], wa_ref[...], ba_ref[...], lam_ref[...])
    n_groups = tt // SCAN_GROUP
    a = a.reshape(n_groups, SCAN_GROUP, LRU_DIM)
    b = b.reshape(n_groups, SCAN_GROUP, LRU_DIM)
    in_group = lax.broadcasted_iota(jnp.int32, (1, SCAN_GROUP, 1), 1)
    dist = 1
    while dist < SCAN_GROUP:
        keep = in_group >= dist
        a_sh = jnp.where(keep, pltpu.roll(a, dist, 1), 1.0)
        b_sh = jnp.where(keep, pltpu.roll(b, dist, 1), 0.0)
        b = a * b_sh + b
        a = a * a_sh
        dist *= 2
    carry = hl[...]
    groups = []
    for gi in range(n_groups):
        hg = a[gi] * carry + b[gi]
        groups.append(hg)
        carry = hg[SCAN_GROUP - 1:SCAN_GROUP, :]
    h = jnp.concatenate(groups, axis=0)
    hl[...] = carry
    y_ref[0, :, GMLP_DIM:D_MODEL] = (h * _gelu(gate_in)).astype(BF16)

    @pl.when(i == nt - 1)
    def _fin():
        oconv_ref[0] = hconv[5:8, :]
        olru_ref[0] = hl[...]


def _odd_prompt(q, st_conv, st_lru, prm):
    B, T, _ = q.shape
    tt = 256
    bt = lambda b, i: (b, i, 0)
    bs3 = lambda b, i: (b, 0, 0)
    c2 = lambda b, i: (0, 0)
    c3 = lambda b, i: (0, 0, 0)
    vec = lambda n: pl.BlockSpec((1, n), c2)
    return pl.pallas_call(
        functools.partial(_odd_prompt_kernel, tt=tt),
        grid=(B, T // tt),
        in_specs=[pl.BlockSpec((1, tt, ODD_PROJ), bt),
                  pl.BlockSpec((1, CONV_WIDTH - 1, LRU_DIM), bs3),
                  pl.BlockSpec((1, 1, LRU_DIM), bs3),
                  vec(GMLP_DIM), vec(GMLP_DIM),
                  pl.BlockSpec((GMLP_HEADS, CHUNK, CHUNK), c3),
                  pl.BlockSpec((CHUNK, GMLP_DIM), c2),
                  pl.BlockSpec((CONV_WIDTH, LRU_DIM), c2), vec(LRU_DIM),
                  pl.BlockSpec((LRU_DIM, LRU_DIM), c2), vec(LRU_DIM),
                  pl.BlockSpec((LRU_DIM, LRU_DIM), c2), vec(LRU_DIM), vec(LRU_DIM)],
        out_specs=[pl.BlockSpec((1, tt, D_MODEL), bt),
                   pl.BlockSpec((1, CONV_WIDTH - 1, LRU_DIM), bs3),
                   pl.BlockSpec((1, 1, LRU_DIM), bs3)],
        out_shape=[jax.ShapeDtypeStruct((B, T, D_MODEL), BF16),
                   jax.ShapeDtypeStruct((B, CONV_WIDTH - 1, LRU_DIM), F32),
                   jax.ShapeDtypeStruct((B, 1, LRU_DIM), F32)],
        scratch_shapes=[pltpu.VMEM((8 + tt, LRU_DIM), F32), pltpu.VMEM((1, LRU_DIM), F32),
                        pltpu.VMEM((tt, GMLP_DIM), F32)],
        compiler_params=pltpu.CompilerParams(dimension_semantics=("arbitrary", "arbitrary"),
                                             vmem_limit_bytes=VMEM_LIMIT),
        name="odd_prompt",
    )(q, st_conv, st_lru, *prm)


def _even_sample_pre_kernel(p_ref, stp_ref, sts_ref,
                            mu_ref, w0_ref, wdec_ref, a0_ref, wa_ref, gw2_ref, kk_ref, ka_ref, rk_ref,
                            eseg_ref, poolw_ref, pools_ref,
                            r_ref, w_ref, kkn_ref, kka_ref, kp_ref, v_ref, g_ref, bonus_ref, ya_ref,
                            opool_ref, oshift_ref, *, T, start):
    prm = (mu_ref[...], w0_ref[...], wdec_ref[...], a0_ref[...], wa_ref[...], gw2_ref[...],
           kk_ref[...], ka_ref[...], rk_ref[...], eseg_ref[...])
    full = [stp_ref[s] for s in range(POOL_BUF)] + [p_ref[t][:, 0:POOL_DIM] for t in range(T)]
    wl = _pool_window_lanes()
    for t in range(T):
        P = p_ref[t][:, POOL_DIM:EVEN_PROJ]
        Pprev = sts_ref[...] if t == 0 else p_ref[t - 1][:, POOL_DIM:EVEN_PROJ]
        r, kp, v, ld, kk, a, g, bonus = _rwkv_pointwise(P, Pprev, prm)
        r_ref[t] = jnp.transpose(r)
        w_ref[t] = jnp.transpose(jnp.exp(ld))
        kkn_ref[t] = jnp.transpose(kk)
        kka_ref[t] = jnp.transpose(kk * a)
        kp_ref[t] = jnp.transpose(kp)
        v_ref[t] = jnp.transpose(v)
        g_ref[t] = g
        bonus_ref[t] = bonus
        e = POOL_BUF + t
        s2 = full[e] + full[e - 1]
        s4 = s2 + full[e - 2] + full[e - 3]
        s8 = s4 + full[e - 4] + full[e - 5] + full[e - 6] + full[e - 7]
        s16 = s8
        for s in range(8, 16):
            s16 = s16 + full[e - s]
        sel = _pool_lane_select(s2, s4, s8, s16)
        cnt = jnp.minimum(wl, start + t + 1).astype(F32)
        d = sel / cnt - full[e]
        ya_ref[t] = _bdot(d, poolw_ref[...]) * pools_ref[...]
    for s in range(POOL_BUF):
        opool_ref[s] = full[T + s]
    oshift_ref[...] = p_ref[T - 1][:, POOL_DIM:EVEN_PROJ]


def _even_sample_pre(p, st_pool, st_shift, prm, start):
    T, B, _ = p.shape
    cm = jax.ShapeDtypeStruct((T, RWKV_DIM, B), F32)
    bm = jax.ShapeDtypeStruct((T, B, RWKV_DIM), F32)
    return pl.pallas_call(
        functools.partial(_even_sample_pre_kernel, T=T, start=start),
        out_shape=[cm] * 6 + [bm] * 2 + [jax.ShapeDtypeStruct((T, B, POOL_DIM), F32),
                                   jax.ShapeDtypeStruct((POOL_BUF, B, POOL_DIM), F32),
                                   jax.ShapeDtypeStruct((B, RWKV_PROJ), F32)],
        compiler_params=pltpu.CompilerParams(vmem_limit_bytes=VMEM_LIMIT),
        name="even_sample_pre",
    )(p, st_pool, st_shift, *prm)


def _wkv_sample_kernel(r_ref, w_ref, kk_ref, kka_ref, kp_ref, v_ref, s_ref, o_ref, so_ref, *, T):
    group = range(WKV_SAMPLE_GROUP)

    def body(ib, carry):
        v0 = pl.multiple_of(ib * WKV_SAMPLE_GROUP, WKV_SAMPLE_GROUP)
        blk = pl.ds(v0, WKV_SAMPLE_GROUP)
        S = [s_ref[0, v0 + u] for u in group]
        for t in range(T):
            kk, w, kka, kp, r = kk_ref[t], w_ref[t], kka_ref[t], kp_ref[t], r_ref[t]
            vv = v_ref[t, blk, :]
            sk = [jnp.sum(S[u] * kk, axis=0, keepdims=True) for u in group]
            S = [S[u] * w - sk[u] * kka + vv[u:u + 1, :] * kp for u in group]
            o_ref[t, blk, :] = jnp.concatenate(
                [jnp.sum(S[u] * r, axis=0, keepdims=True) for u in group], axis=0)
        for u in group:
            so_ref[0, v0 + u] = S[u]
        return carry

    lax.fori_loop(0, HEAD_DIM // WKV_SAMPLE_GROUP, body, 0)


def _wkv_sample(r, w, kk, kka, kp, v, s):
    T, _, B = r.shape
    row_spec = pl.BlockSpec((T, HEAD_DIM, B), lambda h: (0, h, 0))
    st_spec = pl.BlockSpec((1, HEAD_DIM, HEAD_DIM, B), lambda h: (h, 0, 0, 0))
    return pl.pallas_call(
        functools.partial(_wkv_sample_kernel, T=T),
        grid=(RWKV_HEADS,),
        in_specs=[row_spec] * 6 + [st_spec],
        out_specs=[row_spec, st_spec],
        out_shape=[jax.ShapeDtypeStruct((T, RWKV_DIM, B), F32),
                   jax.ShapeDtypeStruct((RWKV_HEADS, HEAD_DIM, HEAD_DIM, B), F32)],
        compiler_params=pltpu.CompilerParams(dimension_semantics=("arbitrary",),
                                             vmem_limit_bytes=VMEM_LIMIT),
        name="wkv_sample",
    )(r, w, kk, kka, kp, v, s)


def _even_sample_post_kernel(o_ref, bonus_ref, g_ref, ya_ref, gng_ref, gnb_ref, eseg_ref, y_ref, *, T):
    for t in range(T):
        o = jnp.transpose(o_ref[t])
        yb = _rwkv_post(o, bonus_ref[t], g_ref[t], gng_ref[...], gnb_ref[...], eseg_ref[...])
        y_ref[t, :, 0:POOL_DIM] = ya_ref[t].astype(BF16)
        y_ref[t, :, POOL_DIM:D_MODEL] = yb.astype(BF16)


def _even_sample_post(o, bonus, g, ya, gn_g, gn_b, e_seg):
    T, _, B = o.shape
    return pl.pallas_call(
        functools.partial(_even_sample_post_kernel, T=T),
        out_shape=jax.ShapeDtypeStruct((T, B, D_MODEL), BF16),
        compiler_params=pltpu.CompilerParams(vmem_limit_bytes=VMEM_LIMIT),
        name="even_sample_post",
    )(o, bonus, g, ya, gn_g, gn_b, e_seg)


def _odd_sample_kernel(q_ref, stc_ref, stl_ref,
                       lng_ref, lnb_ref, wsm_ref, bsm_ref, cw_ref, cb_ref, wx_ref, bx_ref, wa_ref,
                       ba_ref, lam_ref, y_ref, v_ref, oconv_ref, olru_ref, *, T):
    vns = []
    us = []
    for t in range(T):
        u, vn = _gmlp_pre(q_ref[t][:, 0:2 * GMLP_DIM], lng_ref[...], lnb_ref[...])
        us.append(u)
        vns.append(vn)
        v_ref[t] = vn
    full = [stc_ref[s] for s in range(CONV_WIDTH - 1)] + \
           [q_ref[t][:, 2 * GMLP_DIM + LRU_DIM:ODD_PROJ] for t in range(T)]
    h = stl_ref[...]
    for t in range(T):
        mix = bsm_ref[t:t + 1, :]
        for j in range(t + 1):
            mix = mix + wsm_ref[t * T + j:t * T + j + 1, :] * vns[j]
        y_ref[t, :, 0:GMLP_DIM] = (us[t] * mix).astype(BF16)
        xc = full[t + CONV_WIDTH - 1] * cw_ref[CONV_WIDTH - 1:CONV_WIDTH, :] + cb_ref[...]
        for j in range(CONV_WIDTH - 1):
            xc = xc + full[t + j] * cw_ref[j:j + 1, :]
        a, b = _lru_gates(xc, wx_ref[...], bx_ref[...], wa_ref[...], ba_ref[...], lam_ref[...])
        h = a * h + b
        gate_in = q_ref[t][:, 2 * GMLP_DIM:2 * GMLP_DIM + LRU_DIM]
        y_ref[t, :, GMLP_DIM:D_MODEL] = (h * _gelu(gate_in)).astype(BF16)
    for s in range(CONV_WIDTH - 1):
        oconv_ref[s] = full[T + s]
    olru_ref[...] = h


def _odd_sample(q, st_conv, st_lru, prm):
    T, B, _ = q.shape
    return pl.pallas_call(
        functools.partial(_odd_sample_kernel, T=T),
        out_shape=[jax.ShapeDtypeStruct((T, B, D_MODEL), BF16),
                   jax.ShapeDtypeStruct((T, B, GMLP_DIM), F32),
                   jax.ShapeDtypeStruct((CONV_WIDTH - 1, B, LRU_DIM), F32),
                   jax.ShapeDtypeStruct((B, LRU_DIM), F32)],
        compiler_params=pltpu.CompilerParams(vmem_limit_bytes=VMEM_LIMIT),
        name="odd_sample",
    )(q, st_conv, st_lru, *prm)


def _block_diag(w):
    n, c, d = w.shape
    eye = jnp.eye(n, dtype=w.dtype)
    return (eye[:, None, :, None] * w[:, :, None, :]).reshape(n * c, n * d)


def _row(x):
    return x.reshape(1, -1)


def kernel(x_prompt, x_sample, state_pool, state_shift, state_wkv, state_conv, state_lru, ev_norm_g, ev_w_in, pool_w, pool_scale, rwkv_mu, rwkv_w0, rwkv_w_w2, rwkv_a0, rwkv_a_w2, rwkv_g_w2, rwkv_k_k, rwkv_k_a, rwkv_r_k, rwkv_gn_g, rwkv_gn_b, ev_w_out, od_norm_g, od_w_in, gmlp_ln_g, gmlp_ln_b, gmlp_ws, gmlp_bs, lru_conv_w, lru_conv_b, lru_wx, lru_bx, lru_wa, lru_ba, lru_lam, od_w_out, ff_norm_g, ff_w1, ff_w2, final_norm_g):
    B, T, _ = x_prompt.shape
    DB, DT, _ = x_sample.shape
    past_len = 16384

    seg_ids = jnp.arange(SEG_TILE) // HEAD_DIM
    e_seg = (seg_ids[:, None] == seg_ids[None, :]).astype(BF16)
    tri = (jnp.arange(WKV_CHUNK)[None, :] <= jnp.arange(WKV_CHUNK)[:, None]).astype(BF16)
    zlora = jnp.zeros((64, RWKV_DIM), F32)

    ev_common = (_row(rwkv_mu[0]), _row(rwkv_w0[0]),
                 jnp.concatenate([rwkv_w_w2[0], zlora], 0).astype(BF16), _row(rwkv_a0[0]),
                 jnp.concatenate([zlora, rwkv_a_w2[0]], 0).astype(BF16), rwkv_g_w2[0].astype(BF16),
                 _row(rwkv_k_k[0]), _row(rwkv_k_a[0]), _row(rwkv_r_k[0]))
    gn_g, gn_b = _row(rwkv_gn_g[0]), _row(rwkv_gn_b[0])
    pool_bd = _block_diag(pool_w[0]).astype(BF16)
    pool_sc = _row(pool_scale[0])
    w_in0 = ev_w_in[0].astype(BF16)
    g_in0 = _row(ev_norm_g[0])

    xp = x_prompt.reshape(B * T, D_MODEL)
    xs = jnp.transpose(x_sample, (1, 0, 2)).reshape(DT * DB, D_MODEL)

    pp = _inproj(xp, g_in0, w_in0).reshape(B, T, EVEN_PROJ)
    ps = _inproj(xs, g_in0, w_in0).reshape(DT, DB, EVEN_PROJ)

    yp, p_pool, p_shift, p_wkv = _even_prompt(
        pp, jnp.zeros((B, POOL_BUF, POOL_DIM), F32), jnp.zeros((B, 1, RWKV_PROJ), F32),
        jnp.zeros((B, RWKV_HEADS, HEAD_DIM, HEAD_DIM), F32),
        ev_common + (gn_g, gn_b, e_seg, pool_bd, pool_sc, tri), 0)

    pre = _even_sample_pre(ps, jnp.transpose(state_pool[0], (1, 0, 2)), state_shift[0],
                           ev_common + (e_seg, pool_bd, pool_sc), past_len)
    r_s, w_s, kk_s, kka_s, kp_s, v_s, g_s, bonus_s, ya_s, s_pool_tm, s_shift = pre
    o_s, s_wkv_bl = _wkv_sample(r_s, w_s, kk_s, kka_s, kp_s, v_s,
                                jnp.transpose(state_wkv[0], (1, 2, 3, 0)))
    ys = _even_sample_post(o_s, bonus_s, g_s, ya_s, gn_g, gn_b, e_seg)

    w_out0 = ev_w_out[0].astype(BF16)
    ffg = lambda l: _row(ff_norm_g[l])
    gfin = _row(final_norm_g)
    xp = _ffn(xp, yp.reshape(B * T, D_MODEL), w_out0, ffg(0), ff_w1[0].astype(BF16), ff_w2[0].astype(BF16),
              gfin, False)
    xs = _ffn(xs, ys.reshape(DT * DB, D_MODEL), w_out0, ffg(0), ff_w1[0].astype(BF16), ff_w2[0].astype(BF16),
              gfin, False)

    w_in1 = od_w_in[0].astype(BF16)
    g_in1 = _row(od_norm_g[0])
    qp = _inproj(xp, g_in1, w_in1).reshape(B, T, ODD_PROJ)
    qs = _inproj(xs, g_in1, w_in1).reshape(DT, DB, ODD_PROJ)

    lru_common = (lru_conv_w[0], _row(lru_conv_b[0]), _block_diag(lru_wx[0]).astype(BF16), _row(lru_bx[0]),
                  _block_diag(lru_wa[0]).astype(BF16), _row(lru_ba[0]), _row(lru_lam[0]))
    ln = (_row(gmlp_ln_g[0]), _row(gmlp_ln_b[0]))
    bias_full = jnp.repeat(jnp.transpose(gmlp_bs[0]), CHUNK, axis=1)
    yp, p_conv, p_lru = _odd_prompt(
        qp, jnp.zeros((B, CONV_WIDTH - 1, LRU_DIM), F32), jnp.zeros((B, 1, LRU_DIM), F32),
        ln + (gmlp_ws[0], bias_full) + lru_common)

    ws_small = jnp.repeat(jnp.transpose(gmlp_ws[0][:, :DT, :DT], (1, 2, 0)).reshape(DT * DT, GMLP_HEADS),
                          CHUNK, axis=1)
    ys, s_v, s_conv_tm, s_lru = _odd_sample(
        qs, jnp.transpose(state_conv[0], (1, 0, 2)), state_lru[0],
        ln + (ws_small, bias_full[:DT]) + lru_common)

    w_out1 = od_w_out[0].astype(BF16)
    xp = _ffn(xp, yp.reshape(B * T, D_MODEL), w_out1, ffg(1), ff_w1[1].astype(BF16), ff_w2[1].astype(BF16),
              gfin, True)
    xs = _ffn(xs, ys.reshape(DT * DB, D_MODEL), w_out1, ffg(1), ff_w1[1].astype(BF16), ff_w2[1].astype(BF16),
              gfin, True)

    tm2bm = lambda t: jnp.transpose(t, (1, 0, 2))
    y_prompt = xp.reshape(B, T, D_MODEL)
    y_sample = tm2bm(xs.reshape(DT, DB, D_MODEL))
    return (y_prompt, y_sample,
            p_pool[None], p_shift.reshape(1, B, RWKV_PROJ), p_wkv[None],
            p_conv[None], p_lru.reshape(1, B, LRU_DIM),
            tm2bm(s_pool_tm)[None], s_shift[None],
            jnp.transpose(s_wkv_bl, (3, 0, 1, 2))[None],
            tm2bm(s_conv_tm)[None], s_lru[None], tm2bm(s_v)[None])
```

```python
import functools

import jax
import jax.numpy as jnp
from jax import lax
from jax.experimental import pallas as pl
from jax.experimental.pallas import tpu as pltpu

F32 = jnp.float32
BF16 = jnp.bfloat16

D_MODEL = 1024
NORM_EPS = 1e-6
D_FF = 4 * D_MODEL

POOL_WINDOWS = (2, 4, 8, 16)
POOL_GROUP_DIM = 64
POOL_DIM = 256
POOL_BUF = 15

HEAD_DIM = 64
RWKV_DIM = 768
RWKV_HEADS = 12
HEAD_PAIRS = RWKV_HEADS // 2
PAIR_DIM = 2 * HEAD_DIM
RWKV_PROJ = 2560
RWKV_GN_EPS = 64e-5
EXP_NEG_HALF = 0.6065306597126334
EVEN_PROJ = POOL_DIM + RWKV_PROJ
LORA_OFF = 3 * RWKV_DIM
GATE_OFF = LORA_OFF + 128

CHUNK = 128
GMLP_DIM = 512
GMLP_HEADS = 4
LN_EPS = 1e-5
LRU_DIM = 512
CONV_WIDTH = 4
LRU_C = 8.0
ODD_PROJ = 2048

WKV_CHUNK = 64
SEG_TILE = 256
LANES = 128
SCAN_GROUP = 8
PROJ_PIECE = 256
WKV_PREP_CHUNKS = 2
WKV_SAMPLE_GROUP = 8

VMEM_LIMIT = 48 * 1024 * 1024
VMEM_LIMIT_FFN = 56 * 1024 * 1024


def _bdot(a, b):
    return jnp.dot(a.astype(BF16), b.astype(BF16), preferred_element_type=F32)


def _bdot_nt(a, b):
    return lax.dot_general(a.astype(BF16), b.astype(BF16), (((1,), (1,)), ((), ())),
                           preferred_element_type=F32)


def _split3(x):
    hi = x.astype(BF16)
    r1 = x - hi.astype(F32)
    mid = r1.astype(BF16)
    lo = (r1 - mid.astype(F32)).astype(BF16)
    return hi, mid, lo


def _exact_dot_rhs01(x, e):
    hi = x.astype(BF16)
    lo = (x - hi.astype(F32)).astype(BF16)
    d = lambda t: jnp.dot(t, e, preferred_element_type=F32)
    return d(hi) + d(lo)


def _exact_dot_lhs01(e, x):
    hi, mid, lo = _split3(x)
    d = lambda t: jnp.dot(e, t, preferred_element_type=F32)
    return d(hi) + d(mid) + d(lo)


def _segsum(x, e_seg):
    parts = [_exact_dot_rhs01(x[:, g * SEG_TILE:(g + 1) * SEG_TILE], e_seg)
             for g in range(RWKV_DIM // SEG_TILE)]
    return jnp.concatenate(parts, axis=1)


def _softplus(z):
    return jnp.maximum(z, 0.0) + jnp.log(1.0 + jnp.exp(-jnp.abs(z)))


def _sigmoid(z):
    return 0.5 * jnp.tanh(0.5 * z) + 0.5


def _gelu(z):
    return 0.5 * z * (1.0 + jnp.tanh(0.7978845608028654 * (z + 0.044715 * (z * z * z))))


def _rmsnorm(x, g):
    ms = jnp.mean(x * x, axis=-1, keepdims=True)
    return x * lax.rsqrt(ms + NORM_EPS) * g


def _run_interleaved(*stages):
    live = list(stages)
    while live:
        for gen in list(live):
            if next(gen, StopIteration) is StopIteration:
                live.remove(gen)


_POINTWISE_KEYS = ("r", "kp", "v", "ld", "kk", "a", "g", "bonus")


def _rwkv_pointwise_stages(P, Pprev, prm, out):
    (mu, w0, wdec, a0, wa, gw2, k_k, k_a, r_k, e_seg) = prm
    xs = P + (Pprev - P) * mu
    r = xs[:, 0:RWKV_DIM]
    k = xs[:, RWKV_DIM:2 * RWKV_DIM]
    v = xs[:, 2 * RWKV_DIM:3 * RWKV_DIM]
    c_wa = xs[:, LORA_OFF:GATE_OFF]
    cg = xs[:, GATE_OFF:RWKV_PROJ]
    yield
    ld = -EXP_NEG_HALF * _sigmoid(w0 + _bdot(jnp.tanh(c_wa), wdec))
    yield
    a = _sigmoid(a0 + _bdot(c_wa, wa))
    yield
    g = _bdot(_sigmoid(cg), gw2)
    yield
    kk = k * k_k
    kk = kk * lax.rsqrt(jnp.maximum(_segsum(kk * kk, e_seg), 1e-24))
    yield
    kp = k * (1.0 + (a - 1.0) * k_a)
    yield
    bonus = _segsum(r * kp * r_k, e_seg) * v
    out.update(r=r, kp=kp, v=v, ld=ld, kk=kk, a=a, g=g, bonus=bonus)
    yield


def _rwkv_pointwise(P, Pprev, prm):
    out = {}
    for _ in _rwkv_pointwise_stages(P, Pprev, prm, out):
        pass
    return tuple(out[key] for key in _POINTWISE_KEYS)


def _rwkv_post(o, bonus, g, gn_g, gn_b, e_seg):
    m = _segsum(o, e_seg) * (1.0 / HEAD_DIM)
    d = o - m
    var = _segsum(d * d, e_seg) * (1.0 / HEAD_DIM)
    on = d * lax.rsqrt(var + RWKV_GN_EPS) * gn_g + gn_b
    return (on + bonus) * g


def _pool_lane_select(s2, s4, s8, s16):
    lane = lax.broadcasted_iota(jnp.int32, (1, POOL_DIM), 1)
    return jnp.where(lane < 64, s2, jnp.where(lane < 128, s4, jnp.where(lane < 192, s8, s16)))


def _pool_window_lanes():
    lane = lax.broadcasted_iota(jnp.int32, (1, POOL_DIM), 1)
    return jnp.where(lane < 64, 2, jnp.where(lane < 128, 4, jnp.where(lane < 192, 8, 16)))


def _lru_gates(xc, wx, bx, wa, ba, lam):
    gx = _sigmoid(_bdot(xc, wx) + bx)
    ga = _sigmoid(_bdot(xc, wa) + ba)
    log_a = -LRU_C * ga * _softplus(-lam)
    a = jnp.exp(log_a)
    b = jnp.sqrt(-jnp.tanh(log_a) * (a * a + 1.0)) * gx * xc
    return a, b


def _gmlp_pre(zq, ln_g, ln_b):
    z = _gelu(zq)
    u = z[:, :GMLP_DIM]
    v = z[:, GMLP_DIM:]
    m = jnp.mean(v, axis=-1, keepdims=True)
    d = v - m
    var = jnp.mean(d * d, axis=-1, keepdims=True)
    return u, d * lax.rsqrt(var + LN_EPS) * ln_g + ln_b


def _inproj_kernel(x_ref, g_ref, w_ref, o_ref):
    h = _rmsnorm(x_ref[...], g_ref[...])
    o_ref[...] = jnp.dot(h.astype(BF16), w_ref[...], preferred_element_type=F32)


def _inproj(x, g, w):
    m, n = x.shape[0], w.shape[1]
    tm = min(1024, m)
    return pl.pallas_call(
        _inproj_kernel,
        grid=(m // tm,),
        in_specs=[pl.BlockSpec((tm, D_MODEL), lambda i: (i, 0)),
                  pl.BlockSpec((1, D_MODEL), lambda i: (0, 0)),
                  pl.BlockSpec((D_MODEL, n), lambda i: (0, 0), pipeline_mode=pl.Buffered(1))],
        out_specs=pl.BlockSpec((tm, n), lambda i: (i, 0)),
        out_shape=jax.ShapeDtypeStruct((m, n), F32),
        compiler_params=pltpu.CompilerParams(dimension_semantics=("arbitrary",),
                                             vmem_limit_bytes=VMEM_LIMIT),
        name="inproj",
    )(x, g, w)


def _ffn_kernel(x_ref, y_ref, wo_ref, g_ref, w1_ref, w2_ref, gf_ref, o_ref, *, final):
    x1 = x_ref[...] + jnp.dot(y_ref[...], wo_ref[...], preferred_element_type=F32)
    hf = _rmsnorm(x1, g_ref[...]).astype(BF16)
    acc = x1
    fc = 1024
    for c in range(D_FF // fc):
        h = jnp.dot(hf, w1_ref[:, c * fc:(c + 1) * fc], preferred_element_type=F32)
        h = jnp.square(jnp.maximum(h, 0.0)).astype(BF16)
        acc = acc + jnp.dot(h, w2_ref[c * fc:(c + 1) * fc, :], preferred_element_type=F32)
    if final:
        acc = _rmsnorm(acc, gf_ref[...])
    o_ref[...] = acc


def _ffn(x, y, wo, g, w1, w2, gf, final):
    m = x.shape[0]
    tm = min(1024, m)
    const = lambda i: (0, 0)
    return pl.pallas_call(
        functools.partial(_ffn_kernel, final=final),
        grid=(m // tm,),
        in_specs=[pl.BlockSpec((tm, D_MODEL), lambda i: (i, 0)),
                  pl.BlockSpec((tm, D_MODEL), lambda i: (i, 0)),
                  pl.BlockSpec((D_MODEL, D_MODEL), const, pipeline_mode=pl.Buffered(1)),
                  pl.BlockSpec((1, D_MODEL), const),
                  pl.BlockSpec((D_MODEL, D_FF), const, pipeline_mode=pl.Buffered(1)),
                  pl.BlockSpec((D_FF, D_MODEL), const, pipeline_mode=pl.Buffered(1)),
                  pl.BlockSpec((1, D_MODEL), const)],
        out_specs=pl.BlockSpec((tm, D_MODEL), lambda i: (i, 0)),
        out_shape=jax.ShapeDtypeStruct((m, D_MODEL), F32),
        compiler_params=pltpu.CompilerParams(dimension_semantics=("arbitrary",),
                                             vmem_limit_bytes=VMEM_LIMIT_FFN),
        name="ffn",
    )(x, y, wo, g, w1, w2, gf)


def _even_prompt_kernel(x0_ref, xn_ref, gin_ref, win_ref, stp_ref, sts_ref, stw_ref,
                        mu_ref, w0_ref, wdec_ref, a0_ref, wa_ref, gw2_ref, kk_ref, ka_ref, rk_ref,
                        gng_ref, gnb_ref, eseg_ref, poolw_ref, pools_ref, tri_ref,
                        y_ref, opool_ref, oshift_ref, owkv_ref,
                        p_s, hpool, hshift, S, r_s, kp_s, v_s, ld_s, kk_s, a_s, o_s,
                        lhs_b, add_b, vk_b, bend_b, pend_b, *, tt, start):
    i = pl.program_id(1)
    nt = pl.num_programs(1)
    C = WKV_CHUNK

    @pl.when(i == 0)
    def _init():
        h0 = _rmsnorm(x0_ref[0], gin_ref[...]).astype(BF16)
        p_s[...] = jnp.dot(h0, win_ref[...], preferred_element_type=F32)
        hpool[0:1, :] = jnp.zeros((1, POOL_DIM), F32)
        hpool[1:16, :] = stp_ref[0]
        hshift[...] = sts_ref[0]
        S[...] = jnp.zeros(S.shape, F32)
        for j in range(HEAD_PAIRS):
            S[j, 0:HEAD_DIM, 0:HEAD_DIM] = stw_ref[0, 2 * j]
            S[j, HEAD_DIM:PAIR_DIM, HEAD_DIM:PAIR_DIM] = stw_ref[0, 2 * j + 1]
        for j in range(HEAD_PAIRS):
            S[j] = jnp.transpose(S[j])

    p = p_s[...]
    rows = lax.broadcasted_iota(jnp.int32, (tt, 1), 0)
    e_seg = eseg_ref[...]
    prm = (mu_ref[...], w0_ref[...], wdec_ref[...], a0_ref[...], wa_ref[...], gw2_ref[...],
           kk_ref[...], ka_ref[...], rk_ref[...], e_seg)
    pw = {}

    def project_next():
        hb = _rmsnorm(xn_ref[0], gin_ref[...]).astype(BF16)
        yield
        for c0 in range(0, EVEN_PROJ, PROJ_PIECE):
            p_s[:, c0:c0 + PROJ_PIECE] = jnp.dot(hb, win_ref[:, c0:c0 + PROJ_PIECE],
                                                 preferred_element_type=F32)
            yield

    def pointwise():
        u = p[:, 0:POOL_DIM]
        ext = jnp.concatenate([hpool[...], u], axis=0)
        s2 = ext + pltpu.roll(ext, 1, 0)
        s4 = s2 + pltpu.roll(s2, 2, 0)
        s8 = s4 + pltpu.roll(s4, 4, 0)
        s16 = s8 + pltpu.roll(s8, 8, 0)
        sel = _pool_lane_select(s2, s4, s8, s16)[16:, :]
        pos = start + i * tt + rows
        cnt = jnp.minimum(_pool_window_lanes(), pos + 1).astype(F32)
        d = sel / cnt - u
        y_ref[0, :, 0:POOL_DIM] = (_bdot(d, poolw_ref[...]) * pools_ref[...]).astype(BF16)
        hpool[...] = ext[tt:tt + 16, :]
        yield
        P = p[:, POOL_DIM:EVEN_PROJ]
        Pprev = jnp.where(rows == 0, hshift[...], pltpu.roll(P, 1, 0))
        hshift[...] = P[tt - 1:tt, :]
        yield
        yield from _rwkv_pointwise_stages(P, Pprev, prm, pw)
        r_s[...] = pw["r"]
        kp_s[...] = pw["kp"]
        v_s[...] = pw["v"]
        ld_s[...] = pw["ld"]
        kk_s[...] = pw["kk"]
        a_s[...] = pw["a"]
        yield

    _run_interleaved(pointwise(), project_next())
    g, bonus = pw["g"], pw["bonus"]

    lane_c = lax.broadcasted_iota(jnp.int32, (C, PAIR_DIM), 1)
    row_c = lax.broadcasted_iota(jnp.int32, (C, PAIR_DIM), 0)
    head0 = lane_c < HEAD_DIM
    left = lane_c < C
    lo_strict = left & (lane_c < row_c)
    lo_incl = left & (lane_c <= row_c)
    hi_strict = jnp.logical_not(left) & (lane_c - C < row_c)
    hi_incl = jnp.logical_not(left) & (lane_c - C <= row_c)
    eye_r = (lane_c - C == row_c).astype(F32)
    zb = jnp.zeros((C, PAIR_DIM), BF16)
    zbw = jnp.zeros((C, 2 * PAIR_DIM), BF16)
    tri = tri_ref[...]
    pairs = range(HEAD_PAIRS)

    def stack_heads(x):
        z = jnp.zeros_like(x)
        return jnp.concatenate([jnp.where(head0, x, z), jnp.where(head0, z, x)], axis=0)

    def prepare(cg):
        qa_sm, qr_sm, v_sm, rhs_g, kb_src, slot = [], [], [], [], [], []
        for ci in range(WKV_PREP_CHUNKS):
            c = cg * WKV_PREP_CHUNKS + ci
            sl = slice(c * C, (c + 1) * C)
            R = r_s[sl, :]
            K = kp_s[sl, :]
            V = v_s[sl, :]
            LD = ld_s[sl, :]
            KK = kk_s[sl, :]
            KA = KK * a_s[sl, :]
            L = _exact_dot_lhs01(tri, LD)
            Lend = L[C - 1:C, :]
            enL = jnp.exp(-L)
            eE = jnp.exp(Lend - L)
            Qr = R * jnp.exp(L)
            Qa = KK * jnp.exp(L - LD)
            Kt = K * enL
            Bt = KA * enL
            Kend = K * eE
            Bend = KA * eE
            Pend = jnp.exp(Lend)
            for j in pairs:
                ls = slice(j * PAIR_DIM, (j + 1) * PAIR_DIM)
                qa_sm.append(stack_heads(Qa[:, ls]))
                qr_sm.append(stack_heads(Qr[:, ls]))
                v_sm.append(stack_heads(V[:, ls]).astype(BF16))
                bt = Bt[:, ls].astype(BF16)
                kt = Kt[:, ls].astype(BF16)
                rhs_g.append(jnp.concatenate([bt, kt], axis=0))
                kb_src.append((Kend[:, ls], Bend[:, ls], Pend[:, ls]))
                slot.append(c * HEAD_PAIRS + j)
        units = range(len(slot))
        heads = [(u, h) for u in units for h in range(2)]
        hrows = lambda x, h: x[h * C:(h + 1) * C]
        G = [_bdot_nt(jnp.concatenate([qa_sm[u], qr_sm[u]], axis=0), rhs_g[u]) for u in units]
        yield
        GA = [hrows(G[u], h) for u, h in heads]
        GR = [hrows(G[u], 2 + h) for u, h in heads]
        R = [jnp.where(lo_strict, -GA[k], eye_r) for k in range(len(heads))]
        level = 1
        while level < C:
            for k in range(len(heads)):
                rb = R[k].astype(BF16)
                P2 = jnp.dot(rb, jnp.concatenate([rb, zb], axis=0),
                             preferred_element_type=F32)
                R[k] = P2 + jnp.where(left, 0.0, R[k])
            level *= 2
            yield
        AV = []
        for k, (u, h) in enumerate(heads):
            lhs = jnp.concatenate([jnp.where(hi_strict, GA[k], 0.0), jnp.where(hi_incl, GR[k], 0.0)], axis=0)
            AV.append(jnp.dot(lhs.astype(BF16), jnp.concatenate([zb, hrows(v_sm[u], h)], axis=0),
                              preferred_element_type=F32))
        yield
        TQ = []
        for k, (u, h) in enumerate(heads):
            rhs = jnp.concatenate([hrows(qa_sm[u], h), AV[k][0:C]], axis=1).astype(BF16)
            TQ.append(jnp.dot(R[k].astype(BF16), jnp.concatenate([zbw, rhs], axis=0),
                              preferred_element_type=F32))
        yield
        AT = []
        for k, (u, h) in enumerate(heads):
            AT.append(jnp.dot(jnp.where(lo_incl, GR[k], 0.0).astype(BF16),
                              jnp.concatenate([TQ[k].astype(BF16), zbw], axis=0), preferred_element_type=F32))
        yield
        for u in units:
            kend, bend, pend = kb_src[u]
            kb_t = jnp.transpose(jnp.concatenate(
                [stack_heads(kend), stack_heads(bend), jnp.broadcast_to(pend, (PAIR_DIM, PAIR_DIM))],
                axis=1))
            for h in range(2):
                k = 2 * u + h
                lhs_b[slot[u], h * C:(h + 1) * C, :] = TQ[k][:, 0:PAIR_DIM].astype(BF16)
                lhs_b[slot[u], (2 + h) * C:(3 + h) * C, :] = (
                    hrows(qr_sm[u], h) - AT[k][:, 0:PAIR_DIM]).astype(BF16)
                add_b[slot[u], h * C:(h + 1) * C, :] = TQ[k][:, PAIR_DIM:2 * PAIR_DIM]
                add_b[slot[u], (2 + h) * C:(3 + h) * C, :] = AV[k][C:2 * C] - AT[k][:, PAIR_DIM:2 * PAIR_DIM]
            vk_b[slot[u]] = _bdot(kb_t[0:PAIR_DIM], v_sm[u])
            bend_b[slot[u]] = kb_t[PAIR_DIM:2 * PAIR_DIM].astype(BF16)
            pend_b[slot[u]] = kb_t[2 * PAIR_DIM:3 * PAIR_DIM]

    def advance(cg):
        for ci in range(WKV_PREP_CHUNKS):
            c = cg * WKV_PREP_CHUNKS + ci
            UO = [jnp.dot(lhs_b[c * HEAD_PAIRS + j], S[j].astype(BF16), preferred_element_type=F32)
                  + add_b[c * HEAD_PAIRS + j] for j in pairs]
            yield
            for j in pairs:
                u = c * HEAD_PAIRS + j
                S[j] = pend_b[u] * S[j] + vk_b[u] - jnp.dot(bend_b[u], UO[j][0:2 * C].astype(BF16),
                                                            preferred_element_type=F32)
                o_s[c * C:(c + 1) * C, j * PAIR_DIM:(j + 1) * PAIR_DIM] = UO[j][2 * C:3 * C] + UO[j][3 * C:4 * C]
            yield

    n_groups = tt // (C * WKV_PREP_CHUNKS)
    _run_interleaved(prepare(0))
    for cg in range(1, n_groups):
        _run_interleaved(prepare(cg), advance(cg - 1))
    _run_interleaved(advance(n_groups - 1))

    yb = _rwkv_post(o_s[...], bonus, g, gng_ref[...], gnb_ref[...], e_seg)
    y_ref[0, :, POOL_DIM:D_MODEL] = yb.astype(BF16)

    @pl.when(i == nt - 1)
    def _fin():
        opool_ref[0] = hpool[1:16, :]
        oshift_ref[0] = hshift[...]
        for j in range(HEAD_PAIRS):
            S[j] = jnp.transpose(S[j])
        for j in range(HEAD_PAIRS):
            owkv_ref[0, 2 * j] = S[j, 0:HEAD_DIM, 0:HEAD_DIM]
            owkv_ref[0, 2 * j + 1] = S[j, HEAD_DIM:PAIR_DIM, HEAD_DIM:PAIR_DIM]


def _even_prompt(x, g_in, w_in, st_pool, st_shift, st_wkv, prm, start):
    B, T, _ = x.shape
    tt = 256
    nt = T // tt
    bt = lambda b, i: (b, i, 0)
    bs3 = lambda b, i: (b, 0, 0)
    bs4 = lambda b, i: (b, 0, 0, 0)
    c2 = lambda b, i: (0, 0)
    vec = lambda n: pl.BlockSpec((1, n), c2)
    scr = lambda: pltpu.VMEM((tt, RWKV_DIM), F32)
    n_units = (tt // WKV_CHUNK) * HEAD_PAIRS
    return pl.pallas_call(
        functools.partial(_even_prompt_kernel, tt=tt, start=start),
        grid=(B, nt),
        in_specs=[pl.BlockSpec((1, tt, D_MODEL), bs3),
                  pl.BlockSpec((1, tt, D_MODEL), lambda b, i: (b, jnp.minimum(i + 1, nt - 1), 0)),
                  vec(D_MODEL),
                  pl.BlockSpec((D_MODEL, EVEN_PROJ), c2, pipeline_mode=pl.Buffered(1)),
                  pl.BlockSpec((1, POOL_BUF, POOL_DIM), bs3),
                  pl.BlockSpec((1, 1, RWKV_PROJ), bs3),
                  pl.BlockSpec((1, RWKV_HEADS, HEAD_DIM, HEAD_DIM), bs4),
                  vec(RWKV_PROJ), vec(RWKV_DIM), pl.BlockSpec((128, RWKV_DIM), c2), vec(RWKV_DIM),
                  pl.BlockSpec((128, RWKV_DIM), c2), pl.BlockSpec((128, RWKV_DIM), c2),
                  vec(RWKV_DIM), vec(RWKV_DIM), vec(RWKV_DIM), vec(RWKV_DIM), vec(RWKV_DIM),
                  pl.BlockSpec((SEG_TILE, SEG_TILE), c2), pl.BlockSpec((POOL_DIM, POOL_DIM), c2),
                  vec(POOL_DIM), pl.BlockSpec((WKV_CHUNK, WKV_CHUNK), c2)],
        out_specs=[pl.BlockSpec((1, tt, D_MODEL), bt),
                   pl.BlockSpec((1, POOL_BUF, POOL_DIM), bs3),
                   pl.BlockSpec((1, 1, RWKV_PROJ), bs3),
                   pl.BlockSpec((1, RWKV_HEADS, HEAD_DIM, HEAD_DIM), bs4)],
        out_shape=[jax.ShapeDtypeStruct((B, T, D_MODEL), BF16),
                   jax.ShapeDtypeStruct((B, POOL_BUF, POOL_DIM), F32),
                   jax.ShapeDtypeStruct((B, 1, RWKV_PROJ), F32),
                   jax.ShapeDtypeStruct((B, RWKV_HEADS, HEAD_DIM, HEAD_DIM), F32)],
        scratch_shapes=[pltpu.VMEM((tt, EVEN_PROJ), F32),
                        pltpu.VMEM((16, POOL_DIM), F32), pltpu.VMEM((1, RWKV_PROJ), F32),
                        pltpu.VMEM((HEAD_PAIRS, PAIR_DIM, PAIR_DIM), F32),
                        scr(), scr(), scr(), scr(), scr(), scr(), scr(),
                        pltpu.VMEM((n_units, 2 * PAIR_DIM, PAIR_DIM), BF16),
                        pltpu.VMEM((n_units, 2 * PAIR_DIM, PAIR_DIM), F32),
                        pltpu.VMEM((n_units, PAIR_DIM, PAIR_DIM), F32),
                        pltpu.VMEM((n_units, PAIR_DIM, PAIR_DIM), BF16),
                        pltpu.VMEM((n_units, PAIR_DIM, PAIR_DIM), F32)],
        compiler_params=pltpu.CompilerParams(dimension_semantics=("arbitrary", "arbitrary"),
                                             vmem_limit_bytes=VMEM_LIMIT),
        name="even_prompt",
    )(x, x, g_in, w_in, st_pool, st_shift, st_wkv, *prm)


def _odd_prompt_kernel(x0_ref, xn_ref, gin_ref, win_ref, stc_ref, stl_ref,
                       lng_ref, lnb_ref, ws_ref, bias_ref, cw_ref, cb_ref, wx_ref, bx_ref, wa_ref,
                       ba_ref, lam_ref, y_ref, oconv_ref, olru_ref, q_s, hconv, hl, mix_s, *, tt):
    i = pl.program_id(1)
    nt = pl.num_programs(1)

    def project(x_ref):
        h = _rmsnorm(x_ref[0], gin_ref[...])
        return jnp.dot(h.astype(BF16), win_ref[...], preferred_element_type=F32)

    @pl.when(i == 0)
    def _init():
        hconv[0:5, :] = jnp.zeros((5, LRU_DIM), F32)
        hconv[5:8, :] = stc_ref[0]
        hl[...] = stl_ref[0]
        q_s[...] = project(x0_ref)

    q = q_s[...]

    def project_next():
        hb = _rmsnorm(xn_ref[0], gin_ref[...]).astype(BF16)
        yield
        for c0 in range(0, ODD_PROJ, PROJ_PIECE):
            q_s[:, c0:c0 + PROJ_PIECE] = jnp.dot(hb, win_ref[:, c0:c0 + PROJ_PIECE],
                                                 preferred_element_type=F32)
            yield

    def mixers():
        u, vn = _gmlp_pre(q[:, 0:2 * GMLP_DIM], lng_ref[...], lnb_ref[...])
        yield
        rr = lax.broadcasted_iota(jnp.int32, (CHUNK, CHUNK), 0)
        cc = lax.broadcasted_iota(jnp.int32, (CHUNK, CHUNK), 1)
        causal = cc <= rr
        for h in range(GMLP_HEADS):
            wm = jnp.where(causal, ws_ref[h], 0.0).astype(BF16)
            ls = slice(h * CHUNK, (h + 1) * CHUNK)
            for c in range(tt // CHUNK):
                rs = slice(c * CHUNK, (c + 1) * CHUNK)
                mix_s[rs, ls] = (jnp.dot(wm, vn[rs, ls].astype(BF16), preferred_element_type=F32)
                                 + bias_ref[:, ls])
        y_ref[0, :, 0:GMLP_DIM] = (u * mix_s[...]).astype(BF16)
        yield

        gate = _gelu(q[:, 2 * GMLP_DIM:2 * GMLP_DIM + LRU_DIM])
        yield
        xr = q[:, 2 * GMLP_DIM + LRU_DIM:ODD_PROJ]
        hconv[8:8 + tt, :] = xr
        xc = xr * cw_ref[3:4, :] + cb_ref[...]
        for j in range(1, CONV_WIDTH):
            xc = xc + hconv[8 - j:8 - j + tt, :] * cw_ref[3 - j:4 - j, :]
        hconv[0:8, :] = hconv[tt:tt + 8, :]
        yield
        a, b = _lru_gates(xc, wx_ref[...], bx_ref[...], wa_ref[...], ba_ref[...], lam_ref[...])
        yield
        n_groups = tt // SCAN_GROUP
        a = a.reshape(n_groups, SCAN_GROUP, LRU_DIM)
        b = b.reshape(n_groups, SCAN_GROUP, LRU_DIM)
        in_group = lax.broadcasted_iota(jnp.int32, (1, SCAN_GROUP, 1), 1)
        dist = 1
        while dist < SCAN_GROUP:
            keep = in_group >= dist
            a_sh = jnp.where(keep, pltpu.roll(a, dist, 1), 1.0)
            b_sh = jnp.where(keep, pltpu.roll(b, dist, 1), 0.0)
            b = a * b_sh + b
            a = a * a_sh
            dist *= 2
            yield
        carry = hl[...]
        groups = []
        for gi in range(n_groups):
            hg = a[gi] * carry + b[gi]
            groups.append(hg)
            carry = hg[SCAN_GROUP - 1:SCAN_GROUP, :]
        h = jnp.concatenate(groups, axis=0)
        hl[...] = carry
        y_ref[0, :, GMLP_DIM:D_MODEL] = (h * gate).astype(BF16)
        yield

    _run_interleaved(mixers(), project_next())

    @pl.when(i == nt - 1)
    def _fin():
        oconv_ref[0] = hconv[5:8, :]
        olru_ref[0] = hl[...]


def _odd_prompt(x, g_in, w_in, st_conv, st_lru, prm):
    B, T, _ = x.shape
    tt = 256
    nt = T // tt
    bt = lambda b, i: (b, i, 0)
    bs3 = lambda b, i: (b, 0, 0)
    c2 = lambda b, i: (0, 0)
    c3 = lambda b, i: (0, 0, 0)
    vec = lambda n: pl.BlockSpec((1, n), c2)
    return pl.pallas_call(
        functools.partial(_odd_prompt_kernel, tt=tt),
        grid=(B, nt),
        in_specs=[pl.BlockSpec((1, tt, D_MODEL), bs3),
                  pl.BlockSpec((1, tt, D_MODEL), lambda b, i: (b, jnp.minimum(i + 1, nt - 1), 0)),
                  vec(D_MODEL),
                  pl.BlockSpec((D_MODEL, ODD_PROJ), c2, pipeline_mode=pl.Buffered(1)),
                  pl.BlockSpec((1, CONV_WIDTH - 1, LRU_DIM), bs3),
                  pl.BlockSpec((1, 1, LRU_DIM), bs3),
                  vec(GMLP_DIM), vec(GMLP_DIM),
                  pl.BlockSpec((GMLP_HEADS, CHUNK, CHUNK), c3),
                  pl.BlockSpec((CHUNK, GMLP_DIM), c2),
                  pl.BlockSpec((CONV_WIDTH, LRU_DIM), c2), vec(LRU_DIM),
                  pl.BlockSpec((LRU_DIM, LRU_DIM), c2), vec(LRU_DIM),
                  pl.BlockSpec((LRU_DIM, LRU_DIM), c2), vec(LRU_DIM), vec(LRU_DIM)],
        out_specs=[pl.BlockSpec((1, tt, D_MODEL), bt),
                   pl.BlockSpec((1, CONV_WIDTH - 1, LRU_DIM), bs3),
                   pl.BlockSpec((1, 1, LRU_DIM), bs3)],
        out_shape=[jax.ShapeDtypeStruct((B, T, D_MODEL), BF16),
                   jax.ShapeDtypeStruct((B, CONV_WIDTH - 1, LRU_DIM), F32),
                   jax.ShapeDtypeStruct((B, 1, LRU_DIM), F32)],
        scratch_shapes=[pltpu.VMEM((tt, ODD_PROJ), F32),
                        pltpu.VMEM((8 + tt, LRU_DIM), F32), pltpu.VMEM((1, LRU_DIM), F32),
                        pltpu.VMEM((tt, GMLP_DIM), F32)],
        compiler_params=pltpu.CompilerParams(dimension_semantics=("arbitrary", "arbitrary"),
                                             vmem_limit_bytes=VMEM_LIMIT),
        name="odd_prompt",
    )(x, x, g_in, w_in, st_conv, st_lru, *prm)


def _even_sample_pre_kernel(p_ref, stp_ref, sts_ref,
                            mu_ref, w0_ref, wdec_ref, a0_ref, wa_ref, gw2_ref, kk_ref, ka_ref, rk_ref,
                            eseg_ref, poolw_ref, pools_ref,
                            r_ref, w_ref, kkn_ref, kka_ref, kp_ref, v_ref, g_ref, bonus_ref, ya_ref,
                            opool_ref, oshift_ref, *, T, start):
    prm = (mu_ref[...], w0_ref[...], wdec_ref[...], a0_ref[...], wa_ref[...], gw2_ref[...],
           kk_ref[...], ka_ref[...], rk_ref[...], eseg_ref[...])
    full = [stp_ref[s] for s in range(POOL_BUF)] + [p_ref[t][:, 0:POOL_DIM] for t in range(T)]
    wl = _pool_window_lanes()
    for t in range(T):
        P = p_ref[t][:, POOL_DIM:EVEN_PROJ]
        Pprev = sts_ref[...] if t == 0 else p_ref[t - 1][:, POOL_DIM:EVEN_PROJ]
        r, kp, v, ld, kk, a, g, bonus = _rwkv_pointwise(P, Pprev, prm)
        r_ref[t] = jnp.transpose(r)
        w_ref[t] = jnp.transpose(jnp.exp(ld))
        kkn_ref[t] = jnp.transpose(kk)
        kka_ref[t] = jnp.transpose(kk * a)
        kp_ref[t] = jnp.transpose(kp)
        v_ref[t] = jnp.transpose(v)
        g_ref[t] = g
        bonus_ref[t] = bonus
        e = POOL_BUF + t
        s2 = full[e] + full[e - 1]
        s4 = s2 + full[e - 2] + full[e - 3]
        s8 = s4 + full[e - 4] + full[e - 5] + full[e - 6] + full[e - 7]
        s16 = s8
        for s in range(8, 16):
            s16 = s16 + full[e - s]
        sel = _pool_lane_select(s2, s4, s8, s16)
        cnt = jnp.minimum(wl, start + t + 1).astype(F32)
        d = sel / cnt - full[e]
        ya_ref[t] = _bdot(d, poolw_ref[...]) * pools_ref[...]
    for s in range(POOL_BUF):
        opool_ref[s] = full[T + s]
    oshift_ref[...] = p_ref[T - 1][:, POOL_DIM:EVEN_PROJ]


def _even_sample_pre(p, st_pool, st_shift, prm, start):
    T, B, _ = p.shape
    cm = jax.ShapeDtypeStruct((T, RWKV_DIM, B), F32)
    bm = jax.ShapeDtypeStruct((T, B, RWKV_DIM), F32)
    return pl.pallas_call(
        functools.partial(_even_sample_pre_kernel, T=T, start=start),
        out_shape=[cm] * 6 + [bm] * 2 + [jax.ShapeDtypeStruct((T, B, POOL_DIM), F32),
                                   jax.ShapeDtypeStruct((POOL_BUF, B, POOL_DIM), F32),
                                   jax.ShapeDtypeStruct((B, RWKV_PROJ), F32)],
        compiler_params=pltpu.CompilerParams(vmem_limit_bytes=VMEM_LIMIT),
        name="even_sample_pre",
    )(p, st_pool, st_shift, *prm)


def _wkv_sample_kernel(r_ref, w_ref, kk_ref, kka_ref, kp_ref, v_ref, s_ref, o_ref, so_ref, *, T):
    group = range(WKV_SAMPLE_GROUP)

    def body(ib, carry):
        v0 = pl.multiple_of(ib * WKV_SAMPLE_GROUP, WKV_SAMPLE_GROUP)
        blk = pl.ds(v0, WKV_SAMPLE_GROUP)
        S = [s_ref[0, v0 + u] for u in group]
        for t in range(T):
            kk, w, kka, kp, r = kk_ref[t], w_ref[t], kka_ref[t], kp_ref[t], r_ref[t]
            vv = v_ref[t, blk, :]
            sk = [jnp.sum(S[u] * kk, axis=0, keepdims=True) for u in group]
            S = [S[u] * w - sk[u] * kka + vv[u:u + 1, :] * kp for u in group]
            o_ref[t, blk, :] = jnp.concatenate(
                [jnp.sum(S[u] * r, axis=0, keepdims=True) for u in group], axis=0)
        for u in group:
            so_ref[0, v0 + u] = S[u]
        return carry

    lax.fori_loop(0, HEAD_DIM // WKV_SAMPLE_GROUP, body, 0)


def _wkv_sample(r, w, kk, kka, kp, v, s):
    T, _, B = r.shape
    row_spec = pl.BlockSpec((T, HEAD_DIM, B), lambda h: (0, h, 0))
    st_spec = pl.BlockSpec((1, HEAD_DIM, HEAD_DIM, B), lambda h: (h, 0, 0, 0))
    return pl.pallas_call(
        functools.partial(_wkv_sample_kernel, T=T),
        grid=(RWKV_HEADS,),
        in_specs=[row_spec] * 6 + [st_spec],
        out_specs=[row_spec, st_spec],
        out_shape=[jax.ShapeDtypeStruct((T, RWKV_DIM, B), F32),
                   jax.ShapeDtypeStruct((RWKV_HEADS, HEAD_DIM, HEAD_DIM, B), F32)],
        compiler_params=pltpu.CompilerParams(dimension_semantics=("arbitrary",),
                                             vmem_limit_bytes=VMEM_LIMIT),
        name="wkv_sample",
    )(r, w, kk, kka, kp, v, s)


def _even_sample_post_kernel(o_ref, bonus_ref, g_ref, ya_ref, gng_ref, gnb_ref, eseg_ref, y_ref, *, T):
    for t in range(T):
        o = jnp.transpose(o_ref[t])
        yb = _rwkv_post(o, bonus_ref[t], g_ref[t], gng_ref[...], gnb_ref[...], eseg_ref[...])
        y_ref[t, :, 0:POOL_DIM] = ya_ref[t].astype(BF16)
        y_ref[t, :, POOL_DIM:D_MODEL] = yb.astype(BF16)


def _even_sample_post(o, bonus, g, ya, gn_g, gn_b, e_seg):
    T, _, B = o.shape
    return pl.pallas_call(
        functools.partial(_even_sample_post_kernel, T=T),
        out_shape=jax.ShapeDtypeStruct((T, B, D_MODEL), BF16),
        compiler_params=pltpu.CompilerParams(vmem_limit_bytes=VMEM_LIMIT),
        name="even_sample_post",
    )(o, bonus, g, ya, gn_g, gn_b, e_seg)


def _odd_sample_kernel(q_ref, stc_ref, stl_ref,
                       lng_ref, lnb_ref, wsm_ref, bsm_ref, cw_ref, cb_ref, wx_ref, bx_ref, wa_ref,
                       ba_ref, lam_ref, y_ref, v_ref, oconv_ref, olru_ref, *, T):
    vns = []
    us = []
    for t in range(T):
        u, vn = _gmlp_pre(q_ref[t][:, 0:2 * GMLP_DIM], lng_ref[...], lnb_ref[...])
        us.append(u)
        vns.append(vn)
        v_ref[t] = vn
    full = [stc_ref[s] for s in range(CONV_WIDTH - 1)] + \
           [q_ref[t][:, 2 * GMLP_DIM + LRU_DIM:ODD_PROJ] for t in range(T)]
    h = stl_ref[...]
    for t in range(T):
        mix = bsm_ref[t:t + 1, :]
        for j in range(t + 1):
            mix = mix + wsm_ref[t * T + j:t * T + j + 1, :] * vns[j]
        y_ref[t, :, 0:GMLP_DIM] = (us[t] * mix).astype(BF16)
        xc = full[t + CONV_WIDTH - 1] * cw_ref[CONV_WIDTH - 1:CONV_WIDTH, :] + cb_ref[...]
        for j in range(CONV_WIDTH - 1):
            xc = xc + full[t + j] * cw_ref[j:j + 1, :]
        a, b = _lru_gates(xc, wx_ref[...], bx_ref[...], wa_ref[...], ba_ref[...], lam_ref[...])
        h = a * h + b
        gate_in = q_ref[t][:, 2 * GMLP_DIM:2 * GMLP_DIM + LRU_DIM]
        y_ref[t, :, GMLP_DIM:D_MODEL] = (h * _gelu(gate_in)).astype(BF16)
    for s in range(CONV_WIDTH - 1):
        oconv_ref[s] = full[T + s]
    olru_ref[...] = h


def _odd_sample(q, st_conv, st_lru, prm):
    T, B, _ = q.shape
    return pl.pallas_call(
        functools.partial(_odd_sample_kernel, T=T),
        out_shape=[jax.ShapeDtypeStruct((T, B, D_MODEL), BF16),
                   jax.ShapeDtypeStruct((T, B, GMLP_DIM), F32),
                   jax.ShapeDtypeStruct((CONV_WIDTH - 1, B, LRU_DIM), F32),
                   jax.ShapeDtypeStruct((B, LRU_DIM), F32)],
        compiler_params=pltpu.CompilerParams(vmem_limit_bytes=VMEM_LIMIT),
        name="odd_sample",
    )(q, st_conv, st_lru, *prm)


def _block_diag(w):
    n, c, d = w.shape
    eye = jnp.eye(n, dtype=w.dtype)
    return (eye[:, None, :, None] * w[:, :, None, :]).reshape(n * c, n * d)


def _row(x):
    return x.reshape(1, -1)


def kernel(x_prompt, x_sample, state_pool, state_shift, state_wkv, state_conv, state_lru, ev_norm_g, ev_w_in, pool_w, pool_scale, rwkv_mu, rwkv_w0, rwkv_w_w2, rwkv_a0, rwkv_a_w2, rwkv_g_w2, rwkv_k_k, rwkv_k_a, rwkv_r_k, rwkv_gn_g, rwkv_gn_b, ev_w_out, od_norm_g, od_w_in, gmlp_ln_g, gmlp_ln_b, gmlp_ws, gmlp_bs, lru_conv_w, lru_conv_b, lru_wx, lru_bx, lru_wa, lru_ba, lru_lam, od_w_out, ff_norm_g, ff_w1, ff_w2, final_norm_g):
    B, T, _ = x_prompt.shape
    DB, DT, _ = x_sample.shape
    past_len = 16384

    seg_ids = jnp.arange(SEG_TILE) // HEAD_DIM
    e_seg = (seg_ids[:, None] == seg_ids[None, :]).astype(BF16)
    tri = (jnp.arange(WKV_CHUNK)[None, :] <= jnp.arange(WKV_CHUNK)[:, None]).astype(BF16)
    zlora = jnp.zeros((64, RWKV_DIM), F32)

    ev_common = (_row(rwkv_mu[0]), _row(rwkv_w0[0]),
                 jnp.concatenate([rwkv_w_w2[0], zlora], 0).astype(BF16), _row(rwkv_a0[0]),
                 jnp.concatenate([zlora, rwkv_a_w2[0]], 0).astype(BF16), rwkv_g_w2[0].astype(BF16),
                 _row(rwkv_k_k[0]), _row(rwkv_k_a[0]), _row(rwkv_r_k[0]))
    gn_g, gn_b = _row(rwkv_gn_g[0]), _row(rwkv_gn_b[0])
    pool_bd = _block_diag(pool_w[0]).astype(BF16)
    pool_sc = _row(pool_scale[0])
    w_in0 = ev_w_in[0].astype(BF16)
    g_in0 = _row(ev_norm_g[0])

    xp = x_prompt.reshape(B * T, D_MODEL)
    xs = jnp.transpose(x_sample, (1, 0, 2)).reshape(DT * DB, D_MODEL)

    ps = _inproj(xs, g_in0, w_in0).reshape(DT, DB, EVEN_PROJ)

    yp, p_pool, p_shift, p_wkv = _even_prompt(
        x_prompt, g_in0, w_in0,
        jnp.zeros((B, POOL_BUF, POOL_DIM), F32), jnp.zeros((B, 1, RWKV_PROJ), F32),
        jnp.zeros((B, RWKV_HEADS, HEAD_DIM, HEAD_DIM), F32),
        ev_common + (gn_g, gn_b, e_seg, pool_bd, pool_sc, tri), 0)

    pre = _even_sample_pre(ps, jnp.transpose(state_pool[0], (1, 0, 2)), state_shift[0],
                           ev_common + (e_seg, pool_bd, pool_sc), past_len)
    r_s, w_s, kk_s, kka_s, kp_s, v_s, g_s, bonus_s, ya_s, s_pool_tm, s_shift = pre
    o_s, s_wkv_bl = _wkv_sample(r_s, w_s, kk_s, kka_s, kp_s, v_s,
                                jnp.transpose(state_wkv[0], (1, 2, 3, 0)))
    ys = _even_sample_post(o_s, bonus_s, g_s, ya_s, gn_g, gn_b, e_seg)

    w_out0 = ev_w_out[0].astype(BF16)
    ffg = lambda l: _row(ff_norm_g[l])
    gfin = _row(final_norm_g)
    xp = _ffn(xp, yp.reshape(B * T, D_MODEL), w_out0, ffg(0), ff_w1[0].astype(BF16), ff_w2[0].astype(BF16),
              gfin, False)
    xs = _ffn(xs, ys.reshape(DT * DB, D_MODEL), w_out0, ffg(0), ff_w1[0].astype(BF16), ff_w2[0].astype(BF16),
              gfin, False)

    w_in1 = od_w_in[0].astype(BF16)
    g_in1 = _row(od_norm_g[0])
    qs = _inproj(xs, g_in1, w_in1).reshape(DT, DB, ODD_PROJ)

    lru_common = (lru_conv_w[0], _row(lru_conv_b[0]), _block_diag(lru_wx[0]).astype(BF16), _row(lru_bx[0]),
                  _block_diag(lru_wa[0]).astype(BF16), _row(lru_ba[0]), _row(lru_lam[0]))
    ln = (_row(gmlp_ln_g[0]), _row(gmlp_ln_b[0]))
    bias_full = jnp.repeat(jnp.transpose(gmlp_bs[0]), CHUNK, axis=1)
    yp, p_conv, p_lru = _odd_prompt(
        xp.reshape(B, T, D_MODEL), g_in1, w_in1,
        jnp.zeros((B, CONV_WIDTH - 1, LRU_DIM), F32), jnp.zeros((B, 1, LRU_DIM), F32),
        ln + (gmlp_ws[0], bias_full) + lru_common)

    ws_small = jnp.repeat(jnp.transpose(gmlp_ws[0][:, :DT, :DT], (1, 2, 0)).reshape(DT * DT, GMLP_HEADS),
                          CHUNK, axis=1)
    ys, s_v, s_conv_tm, s_lru = _odd_sample(
        qs, jnp.transpose(state_conv[0], (1, 0, 2)), state_lru[0],
        ln + (ws_small, bias_full[:DT]) + lru_common)

    w_out1 = od_w_out[0].astype(BF16)
    xp = _ffn(xp, yp.reshape(B * T, D_MODEL), w_out1, ffg(1), ff_w1[1].astype(BF16), ff_w2[1].astype(BF16),
              gfin, True)
    xs = _ffn(xs, ys.reshape(DT * DB, D_MODEL), w_out1, ffg(1), ff_w1[1].astype(BF16), ff_w2[1].astype(BF16),
              gfin, True)

    tm2bm = lambda t: jnp.transpose(t, (1, 0, 2))
    y_prompt = xp.reshape(B, T, D_MODEL)
    y_sample = tm2bm(xs.reshape(DT, DB, D_MODEL))
    return (y_prompt, y_sample,
            p_pool[None], p_shift.reshape(1, B, RWKV_PROJ), p_wkv[None],
            p_conv[None], p_lru.reshape(1, B, LRU_DIM),
            tm2bm(s_pool_tm)[None], s_shift[None],
            jnp.transpose(s_wkv_bl, (3, 0, 1, 2))[None],
            tm2bm(s_conv_tm)[None], s_lru[None], tm2bm(s_v)[None])
```

```python
import functools

import jax
import jax.numpy as jnp
from jax import lax
from jax.experimental import pallas as pl
from jax.experimental.pallas import tpu as pltpu

F32 = jnp.float32
BF16 = jnp.bfloat16

D_MODEL = 1024
NORM_EPS = 1e-6
D_FF = 4 * D_MODEL

POOL_WINDOWS = (2, 4, 8, 16)
POOL_GROUP_DIM = 64
POOL_DIM = 256
POOL_BUF = 15

HEAD_DIM = 64
RWKV_DIM = 768
RWKV_HEADS = 12
HEAD_PAIRS = RWKV_HEADS // 2
PAIR_DIM = 2 * HEAD_DIM
RWKV_PROJ = 2560
RWKV_GN_EPS = 64e-5
EXP_NEG_HALF = 0.6065306597126334
EVEN_PROJ = POOL_DIM + RWKV_PROJ
LORA_OFF = 3 * RWKV_DIM
GATE_OFF = LORA_OFF + 128

CHUNK = 128
GMLP_DIM = 512
GMLP_HEADS = 4
LN_EPS = 1e-5
GELU_C = 0.7978845608028654
LRU_DIM = 512
CONV_WIDTH = 4
LRU_C = 8.0
ODD_PROJ = 2048

WKV_CHUNK = 64
SEG_TILE = 256
LANES = 128
SCAN_GROUP = 8
PROJ_PIECE = 256
WKV_PREP_CHUNKS = 2
WKV_SAMPLE_GROUP = 8

VMEM_LIMIT = 48 * 1024 * 1024
VMEM_LIMIT_FFN = 56 * 1024 * 1024


def _bdot(a, b):
    return jnp.dot(a.astype(BF16), b.astype(BF16), preferred_element_type=F32)


def _bdot_nt(a, b):
    return lax.dot_general(a.astype(BF16), b.astype(BF16), (((1,), (1,)), ((), ())),
                           preferred_element_type=F32)


def _split3(x):
    hi = x.astype(BF16)
    r1 = x - hi.astype(F32)
    mid = r1.astype(BF16)
    lo = (r1 - mid.astype(F32)).astype(BF16)
    return hi, mid, lo


def _exact_dot_rhs01(x, e):
    hi = x.astype(BF16)
    lo = (x - hi.astype(F32)).astype(BF16)
    d = lambda t: jnp.dot(t, e, preferred_element_type=F32)
    return d(hi) + d(lo)


def _exact_dot_lhs01(e, x):
    hi, mid, lo = _split3(x)
    d = lambda t: jnp.dot(e, t, preferred_element_type=F32)
    return d(hi) + d(mid) + d(lo)


def _segsum(x, e_seg):
    parts = [_exact_dot_rhs01(x[:, g * SEG_TILE:(g + 1) * SEG_TILE], e_seg)
             for g in range(RWKV_DIM // SEG_TILE)]
    return jnp.concatenate(parts, axis=1)


def _softplus(z):
    return jnp.maximum(z, 0.0) + jnp.log(1.0 + jnp.exp(-jnp.abs(z)))


def _sigmoid(z):
    return 0.5 * jnp.tanh(0.5 * z) + 0.5


def _gelu(z):
    hz = 0.5 * z
    return hz + hz * jnp.tanh(z * (GELU_C + (GELU_C * 0.044715) * (z * z)))


def _rmsnorm(x, g):
    ms = jnp.mean(x * x, axis=-1, keepdims=True)
    return x * lax.rsqrt(ms + NORM_EPS) * g


def _next_tile(b, i, *, nt, n_tiles):
    n = jnp.minimum(b * nt + i + 1, n_tiles - 1)
    return (n // nt, n % nt, 0)


def _run_interleaved(*stages):
    live = list(stages)
    while live:
        for gen in list(live):
            if next(gen, StopIteration) is StopIteration:
                live.remove(gen)


_POINTWISE_KEYS = ("r", "kp", "v", "ld", "kk", "a", "g", "bonus")


def _rwkv_pointwise_stages(P, Pprev, prm, out):
    (mu, w0, wdec, a0, wa, gw2, k_k, k_a, r_k, e_seg) = prm
    xs = P + (Pprev - P) * mu
    r = xs[:, 0:RWKV_DIM]
    k = xs[:, RWKV_DIM:2 * RWKV_DIM]
    v = xs[:, 2 * RWKV_DIM:3 * RWKV_DIM]
    c_wa = xs[:, LORA_OFF:GATE_OFF]
    cg = xs[:, GATE_OFF:RWKV_PROJ]
    yield
    ld = -EXP_NEG_HALF * _sigmoid(w0 + _bdot(jnp.tanh(c_wa), wdec))
    yield
    a = _sigmoid(a0 + _bdot(c_wa, wa))
    yield
    g = _bdot(_sigmoid(cg), gw2)
    yield
    kk = k * k_k
    kk = kk * lax.rsqrt(jnp.maximum(_segsum(kk * kk, e_seg), 1e-24))
    yield
    kp = k * (1.0 + (a - 1.0) * k_a)
    yield
    bonus = _segsum(r * kp * r_k, e_seg) * v
    out.update(r=r, kp=kp, v=v, ld=ld, kk=kk, a=a, g=g, bonus=bonus)
    yield


def _rwkv_pointwise(P, Pprev, prm):
    out = {}
    for _ in _rwkv_pointwise_stages(P, Pprev, prm, out):
        pass
    return tuple(out[key] for key in _POINTWISE_KEYS)


def _rwkv_post(o, bonus, g, gn_g, gn_b, e_seg):
    m = _segsum(o, e_seg) * (1.0 / HEAD_DIM)
    d = o - m
    var = _segsum(d * d, e_seg) * (1.0 / HEAD_DIM)
    on = d * lax.rsqrt(var + RWKV_GN_EPS) * gn_g + gn_b
    return (on + bonus) * g


def _pool_lane_select(s2, s4, s8, s16):
    lane = lax.broadcasted_iota(jnp.int32, (1, POOL_DIM), 1)
    return jnp.where(lane < 64, s2, jnp.where(lane < 128, s4, jnp.where(lane < 192, s8, s16)))


def _pool_window_lanes():
    lane = lax.broadcasted_iota(jnp.int32, (1, POOL_DIM), 1)
    return jnp.where(lane < 64, 2, jnp.where(lane < 128, 4, jnp.where(lane < 192, 8, 16)))


def _lru_gates(xc, wx, bx, wa, ba, lam):
    gx = _sigmoid(_bdot(xc, wx) + bx)
    ga = _sigmoid(_bdot(xc, wa) + ba)
    log_a = -LRU_C * ga * _softplus(-lam)
    a = jnp.exp(log_a)
    b = jnp.sqrt(-jnp.tanh(log_a) * (a * a + 1.0)) * gx * xc
    return a, b


def _gmlp_pre(zq, ln_g, ln_b):
    z = _gelu(zq)
    u = z[:, :GMLP_DIM]
    v = z[:, GMLP_DIM:]
    m = jnp.mean(v, axis=-1, keepdims=True)
    d = v - m
    var = jnp.mean(d * d, axis=-1, keepdims=True)
    return u, d * lax.rsqrt(var + LN_EPS) * ln_g + ln_b


def _inproj_kernel(x_ref, g_ref, w_ref, o_ref):
    h = _rmsnorm(x_ref[...], g_ref[...])
    o_ref[...] = jnp.dot(h.astype(BF16), w_ref[...], preferred_element_type=F32)


def _inproj(x, g, w):
    m, n = x.shape[0], w.shape[1]
    tm = min(1024, m)
    return pl.pallas_call(
        _inproj_kernel,
        grid=(m // tm,),
        in_specs=[pl.BlockSpec((tm, D_MODEL), lambda i: (i, 0)),
                  pl.BlockSpec((1, D_MODEL), lambda i: (0, 0)),
                  pl.BlockSpec((D_MODEL, n), lambda i: (0, 0), pipeline_mode=pl.Buffered(1))],
        out_specs=pl.BlockSpec((tm, n), lambda i: (i, 0)),
        out_shape=jax.ShapeDtypeStruct((m, n), F32),
        compiler_params=pltpu.CompilerParams(dimension_semantics=("arbitrary",),
                                             vmem_limit_bytes=VMEM_LIMIT),
        name="inproj",
    )(x, g, w)


def _ffn_kernel(x_ref, y_ref, wo_ref, g_ref, w1_ref, w2_ref, gf_ref, o_ref, *, final):
    x1 = x_ref[...] + jnp.dot(y_ref[...], wo_ref[...], preferred_element_type=F32)
    hf = _rmsnorm(x1, g_ref[...]).astype(BF16)
    acc = x1
    fc = 1024
    for c in range(D_FF // fc):
        h = jnp.dot(hf, w1_ref[:, c * fc:(c + 1) * fc], preferred_element_type=F32)
        h = jnp.square(jnp.maximum(h, 0.0)).astype(BF16)
        acc = acc + jnp.dot(h, w2_ref[c * fc:(c + 1) * fc, :], preferred_element_type=F32)
    if final:
        acc = _rmsnorm(acc, gf_ref[...])
    o_ref[...] = acc


def _ffn(x, y, wo, g, w1, w2, gf, layer, final):
    m = x.shape[0]
    tm = min(1024, m)
    const = lambda i: (0, 0)
    pick = lambda i: (layer, 0, 0)
    return pl.pallas_call(
        functools.partial(_ffn_kernel, final=final),
        grid=(m // tm,),
        in_specs=[pl.BlockSpec((tm, D_MODEL), lambda i: (i, 0)),
                  pl.BlockSpec((tm, D_MODEL), lambda i: (i, 0)),
                  pl.BlockSpec((D_MODEL, D_MODEL), const, pipeline_mode=pl.Buffered(1)),
                  pl.BlockSpec((1, D_MODEL), const),
                  pl.BlockSpec((None, D_MODEL, D_FF), pick, pipeline_mode=pl.Buffered(1)),
                  pl.BlockSpec((None, D_FF, D_MODEL), pick, pipeline_mode=pl.Buffered(1)),
                  pl.BlockSpec((1, D_MODEL), const)],
        out_specs=pl.BlockSpec((tm, D_MODEL), lambda i: (i, 0)),
        out_shape=jax.ShapeDtypeStruct((m, D_MODEL), F32),
        compiler_params=pltpu.CompilerParams(dimension_semantics=("arbitrary",),
                                             vmem_limit_bytes=VMEM_LIMIT_FFN),
        name="ffn",
    )(x, y, wo, g, w1, w2, gf)


def _even_prompt_kernel(x0_ref, xn_ref, gin_ref, win_ref, stp_ref, sts_ref, stw_ref,
                        mu_ref, w0_ref, wdec_ref, a0_ref, wa_ref, gw2_ref, kk_ref, ka_ref, rk_ref,
                        gng_ref, gnb_ref, eseg_ref, poolw_ref, pools_ref, tri_ref,
                        y_ref, opool_ref, oshift_ref, owkv_ref,
                        p_s, hpool, hshift, S, r_s, kp_s, v_s, ld_s, kk_s, a_s, o_s,
                        lhs_b, add_b, vk_b, bend_b, pend_b, *, tt, start):
    i = pl.program_id(1)
    nt = pl.num_programs(1)
    C = WKV_CHUNK

    @pl.when(jnp.logical_and(i == 0, pl.program_id(0) == 0))
    def _first_projection():
        h0 = _rmsnorm(x0_ref[0], gin_ref[...]).astype(BF16)
        p_s[...] = jnp.dot(h0, win_ref[...], preferred_element_type=F32)

    @pl.when(i == 0)
    def _init():
        hpool[0:1, :] = jnp.zeros((1, POOL_DIM), F32)
        hpool[1:16, :] = stp_ref[0]
        hshift[...] = sts_ref[0]
        S[...] = jnp.zeros(S.shape, F32)
        for j in range(HEAD_PAIRS):
            S[j, 0:HEAD_DIM, 0:HEAD_DIM] = stw_ref[0, 2 * j]
            S[j, HEAD_DIM:PAIR_DIM, HEAD_DIM:PAIR_DIM] = stw_ref[0, 2 * j + 1]
        for j in range(HEAD_PAIRS):
            S[j] = jnp.transpose(S[j])

    p = p_s[...]
    rows = lax.broadcasted_iota(jnp.int32, (tt, 1), 0)
    e_seg = eseg_ref[...]
    prm = (mu_ref[...], w0_ref[...], wdec_ref[...], a0_ref[...], wa_ref[...], gw2_ref[...],
           kk_ref[...], ka_ref[...], rk_ref[...], e_seg)
    pw = {}

    def project_next():
        hb = _rmsnorm(xn_ref[0], gin_ref[...]).astype(BF16)
        yield
        for c0 in range(0, EVEN_PROJ, PROJ_PIECE):
            p_s[:, c0:c0 + PROJ_PIECE] = jnp.dot(hb, win_ref[:, c0:c0 + PROJ_PIECE],
                                                 preferred_element_type=F32)
            yield

    def pointwise():
        u = p[:, 0:POOL_DIM]
        ext = jnp.concatenate([hpool[...], u], axis=0)
        s2 = ext + pltpu.roll(ext, 1, 0)
        s4 = s2 + pltpu.roll(s2, 2, 0)
        s8 = s4 + pltpu.roll(s4, 4, 0)
        s16 = s8 + pltpu.roll(s8, 8, 0)
        sel = _pool_lane_select(s2, s4, s8, s16)[16:, :]
        pos = start + i * tt + rows
        cnt = jnp.minimum(_pool_window_lanes(), pos + 1).astype(F32)
        d = sel / cnt - u
        y_ref[0, :, 0:POOL_DIM] = (_bdot(d, poolw_ref[...]) * pools_ref[...]).astype(BF16)
        hpool[...] = ext[tt:tt + 16, :]
        yield
        P = p[:, POOL_DIM:EVEN_PROJ]
        Pprev = jnp.where(rows == 0, hshift[...], pltpu.roll(P, 1, 0))
        hshift[...] = P[tt - 1:tt, :]
        yield
        yield from _rwkv_pointwise_stages(P, Pprev, prm, pw)
        r_s[...] = pw["r"]
        kp_s[...] = pw["kp"]
        v_s[...] = pw["v"]
        ld_s[...] = pw["ld"]
        kk_s[...] = pw["kk"]
        a_s[...] = pw["a"]
        yield

    _run_interleaved(pointwise(), project_next())
    g, bonus = pw["g"], pw["bonus"]

    lane_c = lax.broadcasted_iota(jnp.int32, (C, PAIR_DIM), 1)
    row_c = lax.broadcasted_iota(jnp.int32, (C, PAIR_DIM), 0)
    head0 = lane_c < HEAD_DIM
    left = lane_c < C
    lo_strict = left & (lane_c < row_c)
    lo_incl = left & (lane_c <= row_c)
    hi_strict = jnp.logical_not(left) & (lane_c - C < row_c)
    hi_incl = jnp.logical_not(left) & (lane_c - C <= row_c)
    eye_r = (lane_c - C == row_c).astype(F32)
    zb = jnp.zeros((C, PAIR_DIM), BF16)
    zbw = jnp.zeros((C, 2 * PAIR_DIM), BF16)
    tri = tri_ref[...]
    pairs = range(HEAD_PAIRS)

    def stack_heads(x):
        z = jnp.zeros_like(x)
        return jnp.concatenate([jnp.where(head0, x, z), jnp.where(head0, z, x)], axis=0)

    def prepare(cg):
        qa_sm, qr_sm, v_sm, rhs_g, kb_src, slot = [], [], [], [], [], []
        for ci in range(WKV_PREP_CHUNKS):
            c = cg * WKV_PREP_CHUNKS + ci
            sl = slice(c * C, (c + 1) * C)
            R = r_s[sl, :]
            K = kp_s[sl, :]
            V = v_s[sl, :]
            LD = ld_s[sl, :]
            KK = kk_s[sl, :]
            KA = KK * a_s[sl, :]
            L = _exact_dot_lhs01(tri, LD)
            Lend = L[C - 1:C, :]
            enL = jnp.exp(-L)
            eE = jnp.exp(Lend - L)
            Qr = R * jnp.exp(L)
            Qa = KK * jnp.exp(L - LD)
            Kt = K * enL
            Bt = KA * enL
            Kend = K * eE
            Bend = KA * eE
            Pend = jnp.exp(Lend)
            for j in pairs:
                ls = slice(j * PAIR_DIM, (j + 1) * PAIR_DIM)
                qa_sm.append(stack_heads(Qa[:, ls]))
                qr_sm.append(stack_heads(Qr[:, ls]))
                v_sm.append(stack_heads(V[:, ls]).astype(BF16))
                bt = Bt[:, ls].astype(BF16)
                kt = Kt[:, ls].astype(BF16)
                rhs_g.append(jnp.concatenate([bt, kt], axis=0))
                kb_src.append((Kend[:, ls], Bend[:, ls], Pend[:, ls]))
                slot.append(c * HEAD_PAIRS + j)
        units = range(len(slot))
        heads = [(u, h) for u in units for h in range(2)]
        hrows = lambda x, h: x[h * C:(h + 1) * C]
        G = [_bdot_nt(jnp.concatenate([qa_sm[u], qr_sm[u]], axis=0), rhs_g[u]) for u in units]
        yield
        GA = [hrows(G[u], h) for u, h in heads]
        GR = [hrows(G[u], 2 + h) for u, h in heads]
        R = [jnp.where(lo_strict, -GA[k], eye_r) for k in range(len(heads))]
        level = 1
        while level < C:
            for k in range(len(heads)):
                rb = R[k].astype(BF16)
                P2 = jnp.dot(rb, jnp.concatenate([rb, zb], axis=0),
                             preferred_element_type=F32)
                R[k] = P2 + jnp.where(left, 0.0, R[k])
            level *= 2
            yield
        AV = []
        for k, (u, h) in enumerate(heads):
            lhs = jnp.concatenate([jnp.where(hi_strict, GA[k], 0.0), jnp.where(hi_incl, GR[k], 0.0)], axis=0)
            AV.append(jnp.dot(lhs.astype(BF16), jnp.concatenate([zb, hrows(v_sm[u], h)], axis=0),
                              preferred_element_type=F32))
        yield
        TQ = []
        for k, (u, h) in enumerate(heads):
            rhs = jnp.concatenate([hrows(qa_sm[u], h), AV[k][0:C]], axis=1).astype(BF16)
            TQ.append(jnp.dot(R[k].astype(BF16), jnp.concatenate([zbw, rhs], axis=0),
                              preferred_element_type=F32))
        yield
        AT = []
        for k, (u, h) in enumerate(heads):
            AT.append(jnp.dot(jnp.where(lo_incl, GR[k], 0.0).astype(BF16),
                              jnp.concatenate([TQ[k].astype(BF16), zbw], axis=0), preferred_element_type=F32))
        yield
        for u in units:
            kend, bend, pend = kb_src[u]
            kb_t = jnp.transpose(jnp.concatenate(
                [stack_heads(kend), stack_heads(bend), jnp.broadcast_to(pend, (PAIR_DIM, PAIR_DIM))],
                axis=1))
            for h in range(2):
                k = 2 * u + h
                lhs_b[slot[u], h * C:(h + 1) * C, :] = TQ[k][:, 0:PAIR_DIM].astype(BF16)
                lhs_b[slot[u], (2 + h) * C:(3 + h) * C, :] = (
                    hrows(qr_sm[u], h) - AT[k][:, 0:PAIR_DIM]).astype(BF16)
                add_b[slot[u], h * C:(h + 1) * C, :] = TQ[k][:, PAIR_DIM:2 * PAIR_DIM]
                add_b[slot[u], (2 + h) * C:(3 + h) * C, :] = AV[k][C:2 * C] - AT[k][:, PAIR_DIM:2 * PAIR_DIM]
            vk_b[slot[u]] = _bdot(kb_t[0:PAIR_DIM], v_sm[u])
            bend_b[slot[u]] = kb_t[PAIR_DIM:2 * PAIR_DIM].astype(BF16)
            pend_b[slot[u]] = kb_t[2 * PAIR_DIM:3 * PAIR_DIM]

    def advance(cg):
        for ci in range(WKV_PREP_CHUNKS):
            c = cg * WKV_PREP_CHUNKS + ci
            UO = [jnp.dot(lhs_b[c * HEAD_PAIRS + j], S[j].astype(BF16), preferred_element_type=F32)
                  + add_b[c * HEAD_PAIRS + j] for j in pairs]
            yield
            for j in pairs:
                u = c * HEAD_PAIRS + j
                S[j] = pend_b[u] * S[j] + vk_b[u] - jnp.dot(bend_b[u], UO[j][0:2 * C].astype(BF16),
                                                            preferred_element_type=F32)
                o_s[c * C:(c + 1) * C, j * PAIR_DIM:(j + 1) * PAIR_DIM] = UO[j][2 * C:3 * C] + UO[j][3 * C:4 * C]
            yield

    n_groups = tt // (C * WKV_PREP_CHUNKS)
    _run_interleaved(prepare(0))
    for cg in range(1, n_groups):
        _run_interleaved(prepare(cg), advance(cg - 1))
    _run_interleaved(advance(n_groups - 1))

    yb = _rwkv_post(o_s[...], bonus, g, gng_ref[...], gnb_ref[...], e_seg)
    y_ref[0, :, POOL_DIM:D_MODEL] = yb.astype(BF16)

    @pl.when(i == nt - 1)
    def _fin():
        opool_ref[0] = hpool[1:16, :]
        oshift_ref[0] = hshift[...]
        for j in range(HEAD_PAIRS):
            S[j] = jnp.transpose(S[j])
        for j in range(HEAD_PAIRS):
            owkv_ref[0, 2 * j] = S[j, 0:HEAD_DIM, 0:HEAD_DIM]
            owkv_ref[0, 2 * j + 1] = S[j, HEAD_DIM:PAIR_DIM, HEAD_DIM:PAIR_DIM]


def _even_prompt(x, g_in, w_in, st_pool, st_shift, st_wkv, prm, start):
    B, T, _ = x.shape
    tt = 256
    nt = T // tt
    bt = lambda b, i: (b, i, 0)
    bs3 = lambda b, i: (b, 0, 0)
    bs4 = lambda b, i: (b, 0, 0, 0)
    c2 = lambda b, i: (0, 0)
    vec = lambda n: pl.BlockSpec((1, n), c2)
    scr = lambda: pltpu.VMEM((tt, RWKV_DIM), F32)
    n_units = (tt // WKV_CHUNK) * HEAD_PAIRS
    return pl.pallas_call(
        functools.partial(_even_prompt_kernel, tt=tt, start=start),
        grid=(B, nt),
        in_specs=[pl.BlockSpec((1, tt, D_MODEL), lambda b, i: (0, 0, 0)),
                  pl.BlockSpec((1, tt, D_MODEL), functools.partial(_next_tile, nt=nt, n_tiles=B * nt)),
                  vec(D_MODEL),
                  pl.BlockSpec((D_MODEL, EVEN_PROJ), c2, pipeline_mode=pl.Buffered(1)),
                  pl.BlockSpec((1, POOL_BUF, POOL_DIM), bs3),
                  pl.BlockSpec((1, 1, RWKV_PROJ), bs3),
                  pl.BlockSpec((1, RWKV_HEADS, HEAD_DIM, HEAD_DIM), bs4),
                  vec(RWKV_PROJ), vec(RWKV_DIM), pl.BlockSpec((128, RWKV_DIM), c2), vec(RWKV_DIM),
                  pl.BlockSpec((128, RWKV_DIM), c2), pl.BlockSpec((128, RWKV_DIM), c2),
                  vec(RWKV_DIM), vec(RWKV_DIM), vec(RWKV_DIM), vec(RWKV_DIM), vec(RWKV_DIM),
                  pl.BlockSpec((SEG_TILE, SEG_TILE), c2), pl.BlockSpec((POOL_DIM, POOL_DIM), c2),
                  vec(POOL_DIM), pl.BlockSpec((WKV_CHUNK, WKV_CHUNK), c2)],
        out_specs=[pl.BlockSpec((1, tt, D_MODEL), bt),
                   pl.BlockSpec((1, POOL_BUF, POOL_DIM), bs3),
                   pl.BlockSpec((1, 1, RWKV_PROJ), bs3),
                   pl.BlockSpec((1, RWKV_HEADS, HEAD_DIM, HEAD_DIM), bs4)],
        out_shape=[jax.ShapeDtypeStruct((B, T, D_MODEL), BF16),
                   jax.ShapeDtypeStruct((B, POOL_BUF, POOL_DIM), F32),
                   jax.ShapeDtypeStruct((B, 1, RWKV_PROJ), F32),
                   jax.ShapeDtypeStruct((B, RWKV_HEADS, HEAD_DIM, HEAD_DIM), F32)],
        scratch_shapes=[pltpu.VMEM((tt, EVEN_PROJ), F32),
                        pltpu.VMEM((16, POOL_DIM), F32), pltpu.VMEM((1, RWKV_PROJ), F32),
                        pltpu.VMEM((HEAD_PAIRS, PAIR_DIM, PAIR_DIM), F32),
                        scr(), scr(), scr(), scr(), scr(), scr(), scr(),
                        pltpu.VMEM((n_units, 2 * PAIR_DIM, PAIR_DIM), BF16),
                        pltpu.VMEM((n_units, 2 * PAIR_DIM, PAIR_DIM), F32),
                        pltpu.VMEM((n_units, PAIR_DIM, PAIR_DIM), F32),
                        pltpu.VMEM((n_units, PAIR_DIM, PAIR_DIM), BF16),
                        pltpu.VMEM((n_units, PAIR_DIM, PAIR_DIM), F32)],
        compiler_params=pltpu.CompilerParams(dimension_semantics=("arbitrary", "arbitrary"),
                                             vmem_limit_bytes=VMEM_LIMIT),
        name="even_prompt",
    )(x, x, g_in, w_in, st_pool, st_shift, st_wkv, *prm)


def _odd_prompt_kernel(x0_ref, xn_ref, gin_ref, win_ref, stc_ref, stl_ref,
                       lng_ref, lnb_ref, ws_ref, bias_ref, cw_ref, cb_ref, wx_ref, bx_ref, wa_ref,
                       ba_ref, lam_ref, y_ref, oconv_ref, olru_ref, q_s, hconv, hl, mix_s, *, tt):
    i = pl.program_id(1)
    nt = pl.num_programs(1)

    @pl.when(jnp.logical_and(i == 0, pl.program_id(0) == 0))
    def _first_projection():
        h0 = _rmsnorm(x0_ref[0], gin_ref[...]).astype(BF16)
        q_s[...] = jnp.dot(h0, win_ref[...], preferred_element_type=F32)

    @pl.when(i == 0)
    def _init():
        hconv[0:5, :] = jnp.zeros((5, LRU_DIM), F32)
        hconv[5:8, :] = stc_ref[0]
        hl[...] = stl_ref[0]

    q = q_s[...]

    def project_next():
        hb = _rmsnorm(xn_ref[0], gin_ref[...]).astype(BF16)
        yield
        for c0 in range(0, ODD_PROJ, PROJ_PIECE):
            q_s[:, c0:c0 + PROJ_PIECE] = jnp.dot(hb, win_ref[:, c0:c0 + PROJ_PIECE],
                                                 preferred_element_type=F32)
            yield

    def mixers():
        u, vn = _gmlp_pre(q[:, 0:2 * GMLP_DIM], lng_ref[...], lnb_ref[...])
        yield
        rr = lax.broadcasted_iota(jnp.int32, (CHUNK, CHUNK), 0)
        cc = lax.broadcasted_iota(jnp.int32, (CHUNK, CHUNK), 1)
        causal = cc <= rr
        for h in range(GMLP_HEADS):
            wm = jnp.where(causal, ws_ref[h], 0.0).astype(BF16)
            ls = slice(h * CHUNK, (h + 1) * CHUNK)
            for c in range(tt // CHUNK):
                rs = slice(c * CHUNK, (c + 1) * CHUNK)
                mix_s[rs, ls] = (jnp.dot(wm, vn[rs, ls].astype(BF16), preferred_element_type=F32)
                                 + bias_ref[:, ls])
        y_ref[0, :, 0:GMLP_DIM] = (u * mix_s[...]).astype(BF16)
        yield

        gate = _gelu(q[:, 2 * GMLP_DIM:2 * GMLP_DIM + LRU_DIM])
        yield
        xr = q[:, 2 * GMLP_DIM + LRU_DIM:ODD_PROJ]
        hconv[8:8 + tt, :] = xr
        xc = xr * cw_ref[3:4, :] + cb_ref[...]
        for j in range(1, CONV_WIDTH):
            xc = xc + hconv[8 - j:8 - j + tt, :] * cw_ref[3 - j:4 - j, :]
        hconv[0:8, :] = hconv[tt:tt + 8, :]
        yield
        a, b = _lru_gates(xc, wx_ref[...], bx_ref[...], wa_ref[...], ba_ref[...], lam_ref[...])
        yield
        n_groups = tt // SCAN_GROUP
        a = a.reshape(n_groups, SCAN_GROUP, LRU_DIM)
        b = b.reshape(n_groups, SCAN_GROUP, LRU_DIM)
        in_group = lax.broadcasted_iota(jnp.int32, (1, SCAN_GROUP, 1), 1)
        dist = 1
        while dist < SCAN_GROUP:
            keep = in_group >= dist
            a_sh = jnp.where(keep, pltpu.roll(a, dist, 1), 1.0)
            b_sh = jnp.where(keep, pltpu.roll(b, dist, 1), 0.0)
            b = a * b_sh + b
            a = a * a_sh
            dist *= 2
            yield
        carry = hl[...]
        groups = []
        for gi in range(n_groups):
            hg = a[gi] * carry + b[gi]
            groups.append(hg)
            carry = hg[SCAN_GROUP - 1:SCAN_GROUP, :]
        h = jnp.concatenate(groups, axis=0)
        hl[...] = carry
        y_ref[0, :, GMLP_DIM:D_MODEL] = (h * gate).astype(BF16)
        yield

    _run_interleaved(mixers(), project_next())

    @pl.when(i == nt - 1)
    def _fin():
        oconv_ref[0] = hconv[5:8, :]
        olru_ref[0] = hl[...]


def _odd_prompt(x, g_in, w_in, st_conv, st_lru, prm):
    B, T, _ = x.shape
    tt = 256
    nt = T // tt
    bt = lambda b, i: (b, i, 0)
    bs3 = lambda b, i: (b, 0, 0)
    c2 = lambda b, i: (0, 0)
    c3 = lambda b, i: (0, 0, 0)
    vec = lambda n: pl.BlockSpec((1, n), c2)
    return pl.pallas_call(
        functools.partial(_odd_prompt_kernel, tt=tt),
        grid=(B, nt),
        in_specs=[pl.BlockSpec((1, tt, D_MODEL), lambda b, i: (0, 0, 0)),
                  pl.BlockSpec((1, tt, D_MODEL), functools.partial(_next_tile, nt=nt, n_tiles=B * nt)),
                  vec(D_MODEL),
                  pl.BlockSpec((D_MODEL, ODD_PROJ), c2, pipeline_mode=pl.Buffered(1)),
                  pl.BlockSpec((1, CONV_WIDTH - 1, LRU_DIM), bs3),
                  pl.BlockSpec((1, 1, LRU_DIM), bs3),
                  vec(GMLP_DIM), vec(GMLP_DIM),
                  pl.BlockSpec((GMLP_HEADS, CHUNK, CHUNK), c3),
                  pl.BlockSpec((CHUNK, GMLP_DIM), c2),
                  pl.BlockSpec((CONV_WIDTH, LRU_DIM), c2), vec(LRU_DIM),
                  pl.BlockSpec((LRU_DIM, LRU_DIM), c2), vec(LRU_DIM),
                  pl.BlockSpec((LRU_DIM, LRU_DIM), c2), vec(LRU_DIM), vec(LRU_DIM)],
        out_specs=[pl.BlockSpec((1, tt, D_MODEL), bt),
                   pl.BlockSpec((1, CONV_WIDTH - 1, LRU_DIM), bs3),
                   pl.BlockSpec((1, 1, LRU_DIM), bs3)],
        out_shape=[jax.ShapeDtypeStruct((B, T, D_MODEL), BF16),
                   jax.ShapeDtypeStruct((B, CONV_WIDTH - 1, LRU_DIM), F32),
                   jax.ShapeDtypeStruct((B, 1, LRU_DIM), F32)],
        scratch_shapes=[pltpu.VMEM((tt, ODD_PROJ), F32),
                        pltpu.VMEM((8 + tt, LRU_DIM), F32), pltpu.VMEM((1, LRU_DIM), F32),
                        pltpu.VMEM((tt, GMLP_DIM), F32)],
        compiler_params=pltpu.CompilerParams(dimension_semantics=("arbitrary", "arbitrary"),
                                             vmem_limit_bytes=VMEM_LIMIT),
        name="odd_prompt",
    )(x, x, g_in, w_in, st_conv, st_lru, *prm)


def _even_sample_pre_kernel(p_ref, stp_ref, sts_ref,
                            mu_ref, w0_ref, wdec_ref, a0_ref, wa_ref, gw2_ref, kk_ref, ka_ref, rk_ref,
                            eseg_ref, poolw_ref, pools_ref,
                            r_ref, w_ref, kkn_ref, kka_ref, kp_ref, v_ref, g_ref, bonus_ref, ya_ref,
                            opool_ref, oshift_ref, *, T, start):
    prm = (mu_ref[...], w0_ref[...], wdec_ref[...], a0_ref[...], wa_ref[...], gw2_ref[...],
           kk_ref[...], ka_ref[...], rk_ref[...], eseg_ref[...])
    full = [stp_ref[s] for s in range(POOL_BUF)] + [p_ref[t][:, 0:POOL_DIM] for t in range(T)]
    wl = _pool_window_lanes()
    for t in range(T):
        P = p_ref[t][:, POOL_DIM:EVEN_PROJ]
        Pprev = sts_ref[...] if t == 0 else p_ref[t - 1][:, POOL_DIM:EVEN_PROJ]
        r, kp, v, ld, kk, a, g, bonus = _rwkv_pointwise(P, Pprev, prm)
        r_ref[t] = jnp.transpose(r)
        w_ref[t] = jnp.transpose(jnp.exp(ld))
        kkn_ref[t] = jnp.transpose(kk)
        kka_ref[t] = jnp.transpose(kk * a)
        kp_ref[t] = jnp.transpose(kp)
        v_ref[t] = jnp.transpose(v)
        g_ref[t] = g
        bonus_ref[t] = bonus
        e = POOL_BUF + t
        s2 = full[e] + full[e - 1]
        s4 = s2 + full[e - 2] + full[e - 3]
        s8 = s4 + full[e - 4] + full[e - 5] + full[e - 6] + full[e - 7]
        s16 = s8
        for s in range(8, 16):
            s16 = s16 + full[e - s]
        sel = _pool_lane_select(s2, s4, s8, s16)
        cnt = jnp.minimum(wl, start + t + 1).astype(F32)
        d = sel / cnt - full[e]
        ya_ref[t] = _bdot(d, poolw_ref[...]) * pools_ref[...]
    for s in range(POOL_BUF):
        opool_ref[s] = full[T + s]
    oshift_ref[...] = p_ref[T - 1][:, POOL_DIM:EVEN_PROJ]


def _even_sample_pre(p, st_pool, st_shift, prm, start):
    T, B, _ = p.shape
    cm = jax.ShapeDtypeStruct((T, RWKV_DIM, B), F32)
    bm = jax.ShapeDtypeStruct((T, B, RWKV_DIM), F32)
    return pl.pallas_call(
        functools.partial(_even_sample_pre_kernel, T=T, start=start),
        out_shape=[cm] * 6 + [bm] * 2 + [jax.ShapeDtypeStruct((T, B, POOL_DIM), F32),
                                   jax.ShapeDtypeStruct((POOL_BUF, B, POOL_DIM), F32),
                                   jax.ShapeDtypeStruct((B, RWKV_PROJ), F32)],
        compiler_params=pltpu.CompilerParams(vmem_limit_bytes=VMEM_LIMIT),
        name="even_sample_pre",
    )(p, st_pool, st_shift, *prm)


def _wkv_sample_kernel(r_ref, w_ref, kk_ref, kka_ref, kp_ref, v_ref, s_ref, o_ref, so_ref, *, T):
    group = range(WKV_SAMPLE_GROUP)

    def body(ib, carry):
        v0 = pl.multiple_of(ib * WKV_SAMPLE_GROUP, WKV_SAMPLE_GROUP)
        blk = pl.ds(v0, WKV_SAMPLE_GROUP)
        S = [s_ref[0, v0 + u] for u in group]
        for t in range(T):
            kk, w, kka, kp, r = kk_ref[t], w_ref[t], kka_ref[t], kp_ref[t], r_ref[t]
            vv = v_ref[t, blk, :]
            sk = [jnp.sum(S[u] * kk, axis=0, keepdims=True) for u in group]
            S = [S[u] * w - sk[u] * kka + vv[u:u + 1, :] * kp for u in group]
            o_ref[t, blk, :] = jnp.concatenate(
                [jnp.sum(S[u] * r, axis=0, keepdims=True) for u in group], axis=0)
        for u in group:
            so_ref[0, v0 + u] = S[u]
        return carry

    lax.fori_loop(0, HEAD_DIM // WKV_SAMPLE_GROUP, body, 0)


def _wkv_sample(r, w, kk, kka, kp, v, s):
    T, _, B = r.shape
    row_spec = pl.BlockSpec((T, HEAD_DIM, B), lambda h: (0, h, 0))
    st_spec = pl.BlockSpec((1, HEAD_DIM, HEAD_DIM, B), lambda h: (h, 0, 0, 0))
    return pl.pallas_call(
        functools.partial(_wkv_sample_kernel, T=T),
        grid=(RWKV_HEADS,),
        in_specs=[row_spec] * 6 + [st_spec],
        out_specs=[row_spec, st_spec],
        out_shape=[jax.ShapeDtypeStruct((T, RWKV_DIM, B), F32),
                   jax.ShapeDtypeStruct((RWKV_HEADS, HEAD_DIM, HEAD_DIM, B), F32)],
        compiler_params=pltpu.CompilerParams(dimension_semantics=("arbitrary",),
                                             vmem_limit_bytes=VMEM_LIMIT),
        name="wkv_sample",
    )(r, w, kk, kka, kp, v, s)


def _even_sample_post_kernel(o_ref, bonus_ref, g_ref, ya_ref, gng_ref, gnb_ref, eseg_ref, y_ref, *, T):
    for t in range(T):
        o = jnp.transpose(o_ref[t])
        yb = _rwkv_post(o, bonus_ref[t], g_ref[t], gng_ref[...], gnb_ref[...], eseg_ref[...])
        y_ref[t, :, 0:POOL_DIM] = ya_ref[t].astype(BF16)
        y_ref[t, :, POOL_DIM:D_MODEL] = yb.astype(BF16)


def _even_sample_post(o, bonus, g, ya, gn_g, gn_b, e_seg):
    T, _, B = o.shape
    return pl.pallas_call(
        functools.partial(_even_sample_post_kernel, T=T),
        out_shape=jax.ShapeDtypeStruct((T, B, D_MODEL), BF16),
        compiler_params=pltpu.CompilerParams(vmem_limit_bytes=VMEM_LIMIT),
        name="even_sample_post",
    )(o, bonus, g, ya, gn_g, gn_b, e_seg)


def _odd_sample_kernel(q_ref, stc_ref, stl_ref,
                       lng_ref, lnb_ref, wsm_ref, bsm_ref, cw_ref, cb_ref, wx_ref, bx_ref, wa_ref,
                       ba_ref, lam_ref, y_ref, v_ref, oconv_ref, olru_ref, *, T):
    vns = []
    us = []
    for t in range(T):
        u, vn = _gmlp_pre(q_ref[t][:, 0:2 * GMLP_DIM], lng_ref[...], lnb_ref[...])
        us.append(u)
        vns.append(vn)
        v_ref[t] = vn
    full = [stc_ref[s] for s in range(CONV_WIDTH - 1)] + \
           [q_ref[t][:, 2 * GMLP_DIM + LRU_DIM:ODD_PROJ] for t in range(T)]
    h = stl_ref[...]
    for t in range(T):
        mix = bsm_ref[t:t + 1, :]
        for j in range(t + 1):
            mix = mix + wsm_ref[t * T + j:t * T + j + 1, :] * vns[j]
        y_ref[t, :, 0:GMLP_DIM] = (us[t] * mix).astype(BF16)
        xc = full[t + CONV_WIDTH - 1] * cw_ref[CONV_WIDTH - 1:CONV_WIDTH, :] + cb_ref[...]
        for j in range(CONV_WIDTH - 1):
            xc = xc + full[t + j] * cw_ref[j:j + 1, :]
        a, b = _lru_gates(xc, wx_ref[...], bx_ref[...], wa_ref[...], ba_ref[...], lam_ref[...])
        h = a * h + b
        gate_in = q_ref[t][:, 2 * GMLP_DIM:2 * GMLP_DIM + LRU_DIM]
        y_ref[t, :, GMLP_DIM:D_MODEL] = (h * _gelu(gate_in)).astype(BF16)
    for s in range(CONV_WIDTH - 1):
        oconv_ref[s] = full[T + s]
    olru_ref[...] = h


def _odd_sample(q, st_conv, st_lru, prm):
    T, B, _ = q.shape
    return pl.pallas_call(
        functools.partial(_odd_sample_kernel, T=T),
        out_shape=[jax.ShapeDtypeStruct((T, B, D_MODEL), BF16),
                   jax.ShapeDtypeStruct((T, B, GMLP_DIM), F32),
                   jax.ShapeDtypeStruct((CONV_WIDTH - 1, B, LRU_DIM), F32),
                   jax.ShapeDtypeStruct((B, LRU_DIM), F32)],
        compiler_params=pltpu.CompilerParams(vmem_limit_bytes=VMEM_LIMIT),
        name="odd_sample",
    )(q, st_conv, st_lru, *prm)


def _block_diag(w):
    n, c, d = w.shape
    eye = jnp.eye(n, dtype=w.dtype)
    return (eye[:, None, :, None] * w[:, :, None, :]).reshape(n * c, n * d)


def _row(x):
    return x.reshape(1, -1)


def kernel(x_prompt, x_sample, state_pool, state_shift, state_wkv, state_conv, state_lru, ev_norm_g, ev_w_in, pool_w, pool_scale, rwkv_mu, rwkv_w0, rwkv_w_w2, rwkv_a0, rwkv_a_w2, rwkv_g_w2, rwkv_k_k, rwkv_k_a, rwkv_r_k, rwkv_gn_g, rwkv_gn_b, ev_w_out, od_norm_g, od_w_in, gmlp_ln_g, gmlp_ln_b, gmlp_ws, gmlp_bs, lru_conv_w, lru_conv_b, lru_wx, lru_bx, lru_wa, lru_ba, lru_lam, od_w_out, ff_norm_g, ff_w1, ff_w2, final_norm_g):
    B, T, _ = x_prompt.shape
    DB, DT, _ = x_sample.shape
    past_len = 16384

    seg_ids = jnp.arange(SEG_TILE) // HEAD_DIM
    e_seg = (seg_ids[:, None] == seg_ids[None, :]).astype(BF16)
    tri = (jnp.arange(WKV_CHUNK)[None, :] <= jnp.arange(WKV_CHUNK)[:, None]).astype(BF16)
    zlora = jnp.zeros((64, RWKV_DIM), F32)

    ev_common = (_row(rwkv_mu[0]), _row(rwkv_w0[0]),
                 jnp.concatenate([rwkv_w_w2[0], zlora], 0).astype(BF16), _row(rwkv_a0[0]),
                 jnp.concatenate([zlora, rwkv_a_w2[0]], 0).astype(BF16), rwkv_g_w2[0].astype(BF16),
                 _row(rwkv_k_k[0]), _row(rwkv_k_a[0]), _row(rwkv_r_k[0]))
    gn_g, gn_b = _row(rwkv_gn_g[0]), _row(rwkv_gn_b[0])
    pool_bd = _block_diag(pool_w[0]).astype(BF16)
    pool_sc = _row(pool_scale[0])
    w_in0 = ev_w_in[0].astype(BF16)
    g_in0 = _row(ev_norm_g[0])

    xp = x_prompt.reshape(B * T, D_MODEL)
    xs = jnp.transpose(x_sample, (1, 0, 2)).reshape(DT * DB, D_MODEL)

    ps = _inproj(xs, g_in0, w_in0).reshape(DT, DB, EVEN_PROJ)

    yp, p_pool, p_shift, p_wkv = _even_prompt(
        x_prompt, g_in0, w_in0,
        jnp.zeros((B, POOL_BUF, POOL_DIM), F32), jnp.zeros((B, 1, RWKV_PROJ), F32),
        jnp.zeros((B, RWKV_HEADS, HEAD_DIM, HEAD_DIM), F32),
        ev_common + (gn_g, gn_b, e_seg, pool_bd, pool_sc, tri), 0)

    pre = _even_sample_pre(ps, jnp.transpose(state_pool[0], (1, 0, 2)), state_shift[0],
                           ev_common + (e_seg, pool_bd, pool_sc), past_len)
    r_s, w_s, kk_s, kka_s, kp_s, v_s, g_s, bonus_s, ya_s, s_pool_tm, s_shift = pre
    o_s, s_wkv_bl = _wkv_sample(r_s, w_s, kk_s, kka_s, kp_s, v_s,
                                jnp.transpose(state_wkv[0], (1, 2, 3, 0)))
    ys = _even_sample_post(o_s, bonus_s, g_s, ya_s, gn_g, gn_b, e_seg)

    w_out0 = ev_w_out[0].astype(BF16)
    ffg = lambda l: _row(ff_norm_g[l])
    gfin = _row(final_norm_g)
    ff_w1_b, ff_w2_b = ff_w1.astype(BF16), ff_w2.astype(BF16)
    xp = _ffn(xp, yp.reshape(B * T, D_MODEL), w_out0, ffg(0), ff_w1_b, ff_w2_b, gfin, 0, False)
    xs = _ffn(xs, ys.reshape(DT * DB, D_MODEL), w_out0, ffg(0), ff_w1_b, ff_w2_b, gfin, 0, False)

    w_in1 = od_w_in[0].astype(BF16)
    g_in1 = _row(od_norm_g[0])
    qs = _inproj(xs, g_in1, w_in1).reshape(DT, DB, ODD_PROJ)

    lru_common = (lru_conv_w[0], _row(lru_conv_b[0]), _block_diag(lru_wx[0]).astype(BF16), _row(lru_bx[0]),
                  _block_diag(lru_wa[0]).astype(BF16), _row(lru_ba[0]), _row(lru_lam[0]))
    ln = (_row(gmlp_ln_g[0]), _row(gmlp_ln_b[0]))
    bias_full = jnp.repeat(jnp.transpose(gmlp_bs[0]), CHUNK, axis=1)
    yp, p_conv, p_lru = _odd_prompt(
        xp.reshape(B, T, D_MODEL), g_in1, w_in1,
        jnp.zeros((B, CONV_WIDTH - 1, LRU_DIM), F32), jnp.zeros((B, 1, LRU_DIM), F32),
        ln + (gmlp_ws[0], bias_full) + lru_common)

    ws_small = jnp.repeat(jnp.transpose(gmlp_ws[0][:, :DT, :DT], (1, 2, 0)).reshape(DT * DT, GMLP_HEADS),
                          CHUNK, axis=1)
    ys, s_v, s_conv_tm, s_lru = _odd_sample(
        qs, jnp.transpose(state_conv[0], (1, 0, 2)), state_lru[0],
        ln + (ws_small, bias_full[:DT]) + lru_common)

    w_out1 = od_w_out[0].astype(BF16)
    xp = _ffn(xp, yp.reshape(B * T, D_MODEL), w_out1, ffg(1), ff_w1_b, ff_w2_b, gfin, 1, True)
    xs = _ffn(xs, ys.reshape(DT * DB, D_MODEL), w_out1, ffg(1), ff_w1_b, ff_w2_b, gfin, 1, True)

    tm2bm = lambda t: jnp.transpose(t, (1, 0, 2))
    y_prompt = xp.reshape(B, T, D_MODEL)
    y_sample = tm2bm(xs.reshape(DT, DB, D_MODEL))
    return (y_prompt, y_sample,
            p_pool[None], p_shift.reshape(1, B, RWKV_PROJ), p_wkv[None],
            p_conv[None], p_lru.reshape(1, B, LRU_DIM),
            tm2bm(s_pool_tm)[None], s_shift[None],
            jnp.transpose(s_wkv_bl, (3, 0, 1, 2))[None],
            tm2bm(s_conv_tm)[None], s_lru[None], tm2bm(s_v)[None])
```

```python
import functools

import jax
import jax.numpy as jnp
from jax import lax
from jax.experimental import pallas as pl
from jax.experimental.pallas import tpu as pltpu

F32 = jnp.float32
BF16 = jnp.bfloat16

D_MODEL = 1024
NORM_EPS = 1e-6
D_FF = 4 * D_MODEL

POOL_WINDOWS = (2, 4, 8, 16)
POOL_GROUP_DIM = 64
POOL_DIM = 256
POOL_BUF = 15

HEAD_DIM = 64
RWKV_DIM = 768
RWKV_HEADS = 12
HEAD_PAIRS = RWKV_HEADS // 2
PAIR_DIM = 2 * HEAD_DIM
RWKV_PROJ = 2560
RWKV_GN_EPS = 64e-5
EXP_NEG_HALF = 0.6065306597126334
EVEN_PROJ = POOL_DIM + RWKV_PROJ
LORA_OFF = 3 * RWKV_DIM
LORA_PAD = 128
GATE_OFF = LORA_OFF + LORA_PAD

CHUNK = 128
GMLP_DIM = 512
GMLP_HEADS = 4
LN_EPS = 1e-5
GELU_C = 0.7978845608028654
LRU_DIM = 512
CONV_WIDTH = 4
LRU_C = 8.0
ODD_PROJ = 2048

WKV_CHUNK = 64
SEG_TILE = 256
LANES = 128
SCAN_GROUP = 8
PROJ_PIECE = 256
WKV_PREP_CHUNKS = 2
WKV_SAMPLE_GROUP = 8

PAST_LEN = 16384
MIXER_TILE = 256
DENSE_TILE = 1024

VMEM_LIMIT = 48 * 1024 * 1024
VMEM_LIMIT_FFN = 56 * 1024 * 1024


def _bdot(a, b):
    return jnp.dot(a.astype(BF16), b.astype(BF16), preferred_element_type=F32)


def _bdot_nt(a, b):
    return lax.dot_general(a.astype(BF16), b.astype(BF16), (((1,), (1,)), ((), ())),
                           preferred_element_type=F32)


def _split3(x):
    hi = x.astype(BF16)
    r1 = x - hi.astype(F32)
    mid = r1.astype(BF16)
    lo = (r1 - mid.astype(F32)).astype(BF16)
    return hi, mid, lo


def _exact_dot_rhs01(x, e):
    hi = x.astype(BF16)
    lo = (x - hi.astype(F32)).astype(BF16)
    d = lambda t: jnp.dot(t, e, preferred_element_type=F32)
    return d(hi) + d(lo)


def _exact_dot_lhs01(e, x):
    hi, mid, lo = _split3(x)
    d = lambda t: jnp.dot(e, t, preferred_element_type=F32)
    return d(hi) + d(mid) + d(lo)


def _segsum(x, e_seg):
    parts = [_exact_dot_rhs01(x[:, g * SEG_TILE:(g + 1) * SEG_TILE], e_seg)
             for g in range(RWKV_DIM // SEG_TILE)]
    return jnp.concatenate(parts, axis=1)


def _softplus(z):
    return jnp.maximum(z, 0.0) + jnp.log(1.0 + jnp.exp(-jnp.abs(z)))


def _sigmoid(z):
    return 0.5 * jnp.tanh(0.5 * z) + 0.5


def _gelu(z):
    hz = 0.5 * z
    return hz + hz * jnp.tanh(z * (GELU_C + (GELU_C * 0.044715) * (z * z)))


def _rmsnorm(x, g):
    ms = jnp.mean(x * x, axis=-1, keepdims=True)
    return x * lax.rsqrt(ms + NORM_EPS) * g


def _next_tile(b, i, *, nt, n_tiles):
    n = jnp.minimum(b * nt + i + 1, n_tiles - 1)
    return (n // nt, n % nt, 0)


def _run_interleaved(*stages):
    live = list(stages)
    while live:
        for gen in list(live):
            if next(gen, StopIteration) is StopIteration:
                live.remove(gen)


_POINTWISE_KEYS = ("r", "kp", "v", "ld", "kk", "a", "g", "bonus")


def _rwkv_pointwise_stages(P, Pprev, prm, out):
    (mu, w0, wdec, a0, wa, gw2, k_k, k_a, r_k, e_seg) = prm
    xs = P + (Pprev - P) * mu
    r = xs[:, 0:RWKV_DIM]
    k = xs[:, RWKV_DIM:2 * RWKV_DIM]
    v = xs[:, 2 * RWKV_DIM:3 * RWKV_DIM]
    c_wa = xs[:, LORA_OFF:GATE_OFF]
    cg = xs[:, GATE_OFF:RWKV_PROJ]
    yield
    ld = -EXP_NEG_HALF * _sigmoid(w0 + _bdot(jnp.tanh(c_wa), wdec))
    yield
    a = _sigmoid(a0 + _bdot(c_wa, wa))
    yield
    g = _bdot(_sigmoid(cg), gw2)
    yield
    kk = k * k_k
    kk = kk * lax.rsqrt(jnp.maximum(_segsum(kk * kk, e_seg), 1e-24))
    yield
    kp = k * (1.0 + (a - 1.0) * k_a)
    yield
    bonus = _segsum(r * kp * r_k, e_seg) * v
    out.update(r=r, kp=kp, v=v, ld=ld, kk=kk, a=a, g=g, bonus=bonus)
    yield


def _rwkv_pointwise(P, Pprev, prm):
    out = {}
    for _ in _rwkv_pointwise_stages(P, Pprev, prm, out):
        pass
    return tuple(out[key] for key in _POINTWISE_KEYS)


def _rwkv_post(o, bonus, g, gn_g, gn_b, e_seg):
    m = _segsum(o, e_seg) * (1.0 / HEAD_DIM)
    d = o - m
    var = _segsum(d * d, e_seg) * (1.0 / HEAD_DIM)
    on = d * lax.rsqrt(var + RWKV_GN_EPS) * gn_g + gn_b
    return (on + bonus) * g


def _pool_lane_select(s2, s4, s8, s16):
    lane = lax.broadcasted_iota(jnp.int32, (1, POOL_DIM), 1)
    return jnp.where(lane < 64, s2, jnp.where(lane < 128, s4, jnp.where(lane < 192, s8, s16)))


def _pool_window_lanes():
    lane = lax.broadcasted_iota(jnp.int32, (1, POOL_DIM), 1)
    return jnp.where(lane < 64, 2, jnp.where(lane < 128, 4, jnp.where(lane < 192, 8, 16)))


def _lru_gates(xc, wx, bx, wa, ba, lam):
    gx = _sigmoid(_bdot(xc, wx) + bx)
    ga = _sigmoid(_bdot(xc, wa) + ba)
    log_a = -LRU_C * ga * _softplus(-lam)
    a = jnp.exp(log_a)
    b = jnp.sqrt(-jnp.tanh(log_a) * (a * a + 1.0)) * gx * xc
    return a, b


def _gmlp_pre(zq, ln_g, ln_b):
    z = _gelu(zq)
    u = z[:, :GMLP_DIM]
    v = z[:, GMLP_DIM:]
    m = jnp.mean(v, axis=-1, keepdims=True)
    d = v - m
    var = jnp.mean(d * d, axis=-1, keepdims=True)
    return u, d * lax.rsqrt(var + LN_EPS) * ln_g + ln_b


def _inproj_kernel(x_ref, g_ref, w_ref, o_ref):
    h = _rmsnorm(x_ref[...], g_ref[...])
    o_ref[...] = jnp.dot(h.astype(BF16), w_ref[...], preferred_element_type=F32)


def _inproj(x, g, w):
    m, n = x.shape[0], w.shape[1]
    tm = min(DENSE_TILE, m)
    return pl.pallas_call(
        _inproj_kernel,
        grid=(m // tm,),
        in_specs=[pl.BlockSpec((tm, D_MODEL), lambda i: (i, 0)),
                  pl.BlockSpec((1, D_MODEL), lambda i: (0, 0)),
                  pl.BlockSpec((D_MODEL, n), lambda i: (0, 0), pipeline_mode=pl.Buffered(1))],
        out_specs=pl.BlockSpec((tm, n), lambda i: (i, 0)),
        out_shape=jax.ShapeDtypeStruct((m, n), F32),
        compiler_params=pltpu.CompilerParams(dimension_semantics=("arbitrary",),
                                             vmem_limit_bytes=VMEM_LIMIT),
        name="inproj",
    )(x, g, w)


def _ffn_kernel(x_ref, y_ref, wo_ref, g_ref, w1_ref, w2_ref, gf_ref, o_ref, *, final):
    x1 = x_ref[...] + jnp.dot(y_ref[...], wo_ref[...], preferred_element_type=F32)
    hf = _rmsnorm(x1, g_ref[...]).astype(BF16)
    acc = x1
    fc = 1024
    for c in range(D_FF // fc):
        h = jnp.dot(hf, w1_ref[:, c * fc:(c + 1) * fc], preferred_element_type=F32)
        h = jnp.square(jnp.maximum(h, 0.0)).astype(BF16)
        acc = acc + jnp.dot(h, w2_ref[c * fc:(c + 1) * fc, :], preferred_element_type=F32)
    if final:
        acc = _rmsnorm(acc, gf_ref[...])
    o_ref[...] = acc


def _ffn(x, y, wo, g, w1, w2, gf, layer, final):
    m = x.shape[0]
    tm = min(DENSE_TILE, m)
    const = lambda i: (0, 0)
    pick = lambda i: (layer, 0, 0)
    return pl.pallas_call(
        functools.partial(_ffn_kernel, final=final),
        grid=(m // tm,),
        in_specs=[pl.BlockSpec((tm, D_MODEL), lambda i: (i, 0)),
                  pl.BlockSpec((tm, D_MODEL), lambda i: (i, 0)),
                  pl.BlockSpec((D_MODEL, D_MODEL), const, pipeline_mode=pl.Buffered(1)),
                  pl.BlockSpec((1, D_MODEL), const),
                  pl.BlockSpec((None, D_MODEL, D_FF), pick, pipeline_mode=pl.Buffered(1)),
                  pl.BlockSpec((None, D_FF, D_MODEL), pick, pipeline_mode=pl.Buffered(1)),
                  pl.BlockSpec((1, D_MODEL), const)],
        out_specs=pl.BlockSpec((tm, D_MODEL), lambda i: (i, 0)),
        out_shape=jax.ShapeDtypeStruct((m, D_MODEL), F32),
        compiler_params=pltpu.CompilerParams(
            dimension_semantics=("arbitrary",),
            vmem_limit_bytes=VMEM_LIMIT_FFN if tm == DENSE_TILE else VMEM_LIMIT),
        name="ffn",
    )(x, y, wo, g, w1, w2, gf)


def _even_prompt_kernel(x0_ref, xn_ref, gin_ref, win_ref, stp_ref, sts_ref, stw_ref,
                        mu_ref, w0_ref, wdec_ref, a0_ref, wa_ref, gw2_ref, kk_ref, ka_ref, rk_ref,
                        gng_ref, gnb_ref, eseg_ref, poolw_ref, pools_ref, tri_ref,
                        y_ref, opool_ref, oshift_ref, owkv_ref,
                        p_s, hpool, hshift, S, r_s, kp_s, v_s, ld_s, kk_s, a_s, o_s,
                        lhs_b, add_b, vk_b, bend_b, pend_b, *, tt, start):
    i = pl.program_id(1)
    nt = pl.num_programs(1)
    C = WKV_CHUNK

    @pl.when(jnp.logical_and(i == 0, pl.program_id(0) == 0))
    def _first_projection():
        h0 = _rmsnorm(x0_ref[0], gin_ref[...]).astype(BF16)
        p_s[...] = jnp.dot(h0, win_ref[...], preferred_element_type=F32)

    @pl.when(i == 0)
    def _init():
        hpool[0:1, :] = jnp.zeros((1, POOL_DIM), F32)
        hpool[1:16, :] = stp_ref[0]
        hshift[...] = sts_ref[0]
        S[...] = jnp.zeros(S.shape, F32)
        for j in range(HEAD_PAIRS):
            S[j, 0:HEAD_DIM, 0:HEAD_DIM] = stw_ref[0, 2 * j]
            S[j, HEAD_DIM:PAIR_DIM, HEAD_DIM:PAIR_DIM] = stw_ref[0, 2 * j + 1]
        for j in range(HEAD_PAIRS):
            S[j] = jnp.transpose(S[j])

    p = p_s[...]
    rows = lax.broadcasted_iota(jnp.int32, (tt, 1), 0)
    e_seg = eseg_ref[...]
    prm = (mu_ref[...], w0_ref[...], wdec_ref[...], a0_ref[...], wa_ref[...], gw2_ref[...],
           kk_ref[...], ka_ref[...], rk_ref[...], e_seg)
    pw = {}

    def project_next():
        hb = _rmsnorm(xn_ref[0], gin_ref[...]).astype(BF16)
        yield
        for c0 in range(0, EVEN_PROJ, PROJ_PIECE):
            p_s[:, c0:c0 + PROJ_PIECE] = jnp.dot(hb, win_ref[:, c0:c0 + PROJ_PIECE],
                                                 preferred_element_type=F32)
            yield

    def pointwise():
        u = p[:, 0:POOL_DIM]
        ext = jnp.concatenate([hpool[...], u], axis=0)
        s2 = ext + pltpu.roll(ext, 1, 0)
        s4 = s2 + pltpu.roll(s2, 2, 0)
        s8 = s4 + pltpu.roll(s4, 4, 0)
        s16 = s8 + pltpu.roll(s8, 8, 0)
        sel = _pool_lane_select(s2, s4, s8, s16)[16:, :]
        pos = start + i * tt + rows
        cnt = jnp.minimum(_pool_window_lanes(), pos + 1).astype(F32)
        d = sel / cnt - u
        y_ref[0, :, 0:POOL_DIM] = (_bdot(d, poolw_ref[...]) * pools_ref[...]).astype(BF16)
        hpool[...] = ext[tt:tt + 16, :]
        yield
        P = p[:, POOL_DIM:EVEN_PROJ]
        Pprev = jnp.where(rows == 0, hshift[...], pltpu.roll(P, 1, 0))
        hshift[...] = P[tt - 1:tt, :]
        yield
        yield from _rwkv_pointwise_stages(P, Pprev, prm, pw)
        r_s[...] = pw["r"]
        kp_s[...] = pw["kp"]
        v_s[...] = pw["v"]
        ld_s[...] = pw["ld"]
        kk_s[...] = pw["kk"]
        a_s[...] = pw["a"]
        yield

    _run_interleaved(pointwise(), project_next())
    g, bonus = pw["g"], pw["bonus"]

    lane_c = lax.broadcasted_iota(jnp.int32, (C, PAIR_DIM), 1)
    row_c = lax.broadcasted_iota(jnp.int32, (C, PAIR_DIM), 0)
    head0 = lane_c < HEAD_DIM
    left = lane_c < C
    lo_strict = left & (lane_c < row_c)
    lo_incl = left & (lane_c <= row_c)
    hi_strict = jnp.logical_not(left) & (lane_c - C < row_c)
    hi_incl = jnp.logical_not(left) & (lane_c - C <= row_c)
    eye_r = (lane_c - C == row_c).astype(F32)
    zb = jnp.zeros((C, PAIR_DIM), BF16)
    zbw = jnp.zeros((C, 2 * PAIR_DIM), BF16)
    tri = tri_ref[...]
    pairs = range(HEAD_PAIRS)

    def stack_heads(x):
        z = jnp.zeros_like(x)
        return jnp.concatenate([jnp.where(head0, x, z), jnp.where(head0, z, x)], axis=0)

    def prepare(cg):
        qa_sm, qr_sm, v_sm, rhs_g, kb_src, slot = [], [], [], [], [], []
        for ci in range(WKV_PREP_CHUNKS):
            c = cg * WKV_PREP_CHUNKS + ci
            sl = slice(c * C, (c + 1) * C)
            R = r_s[sl, :]
            K = kp_s[sl, :]
            V = v_s[sl, :]
            LD = ld_s[sl, :]
            KK = kk_s[sl, :]
            KA = KK * a_s[sl, :]
            L = _exact_dot_lhs01(tri, LD)
            Lend = L[C - 1:C, :]
            enL = jnp.exp(-L)
            eE = jnp.exp(Lend - L)
            Qr = R * jnp.exp(L)
            Qa = KK * jnp.exp(L - LD)
            Kt = K * enL
            Bt = KA * enL
            Kend = K * eE
            Bend = KA * eE
            Pend = jnp.exp(Lend)
            for j in pairs:
                ls = slice(j * PAIR_DIM, (j + 1) * PAIR_DIM)
                qa_sm.append(stack_heads(Qa[:, ls]))
                qr_sm.append(stack_heads(Qr[:, ls]))
                v_sm.append(stack_heads(V[:, ls]).astype(BF16))
                bt = Bt[:, ls].astype(BF16)
                kt = Kt[:, ls].astype(BF16)
                rhs_g.append(jnp.concatenate([bt, kt], axis=0))
                kb_src.append((Kend[:, ls], Bend[:, ls], Pend[:, ls]))
                slot.append(c * HEAD_PAIRS + j)
        units = range(len(slot))
        heads = [(u, h) for u in units for h in range(2)]
        hrows = lambda x, h: x[h * C:(h + 1) * C]
        G = [_bdot_nt(jnp.concatenate([qa_sm[u], qr_sm[u]], axis=0), rhs_g[u]) for u in units]
        yield
        GA = [hrows(G[u], h) for u, h in heads]
        GR = [hrows(G[u], 2 + h) for u, h in heads]
        R = [jnp.where(lo_strict, -GA[k], eye_r) for k in range(len(heads))]
        level = 1
        while level < C:
            for k in range(len(heads)):
                rb = R[k].astype(BF16)
                P2 = jnp.dot(rb, jnp.concatenate([rb, zb], axis=0),
                             preferred_element_type=F32)
                R[k] = P2 + jnp.where(left, 0.0, R[k])
            level *= 2
            yield
        AV = []
        for k, (u, h) in enumerate(heads):
            lhs = jnp.concatenate([jnp.where(hi_strict, GA[k], 0.0), jnp.where(hi_incl, GR[k], 0.0)], axis=0)
            AV.append(jnp.dot(lhs.astype(BF16), jnp.concatenate([zb, hrows(v_sm[u], h)], axis=0),
                              preferred_element_type=F32))
        yield
        TQ = []
        for k, (u, h) in enumerate(heads):
            rhs = jnp.concatenate([hrows(qa_sm[u], h), AV[k][0:C]], axis=1).astype(BF16)
            TQ.append(jnp.dot(R[k].astype(BF16), jnp.concatenate([zbw, rhs], axis=0),
                              preferred_element_type=F32))
        yield
        AT = []
        for k, (u, h) in enumerate(heads):
            AT.append(jnp.dot(jnp.where(lo_incl, GR[k], 0.0).astype(BF16),
                              jnp.concatenate([TQ[k].astype(BF16), zbw], axis=0), preferred_element_type=F32))
        yield
        for u in units:
            kend, bend, pend = kb_src[u]
            kb_t = jnp.transpose(jnp.concatenate(
                [stack_heads(kend), stack_heads(bend), jnp.broadcast_to(pend, (PAIR_DIM, PAIR_DIM))],
                axis=1))
            for h in range(2):
                k = 2 * u + h
                lhs_b[slot[u], h * C:(h + 1) * C, :] = TQ[k][:, 0:PAIR_DIM].astype(BF16)
                lhs_b[slot[u], (2 + h) * C:(3 + h) * C, :] = (
                    hrows(qr_sm[u], h) - AT[k][:, 0:PAIR_DIM]).astype(BF16)
                add_b[slot[u], h * C:(h + 1) * C, :] = TQ[k][:, PAIR_DIM:2 * PAIR_DIM]
                add_b[slot[u], (2 + h) * C:(3 + h) * C, :] = AV[k][C:2 * C] - AT[k][:, PAIR_DIM:2 * PAIR_DIM]
            vk_b[slot[u]] = _bdot(kb_t[0:PAIR_DIM], v_sm[u])
            bend_b[slot[u]] = kb_t[PAIR_DIM:2 * PAIR_DIM].astype(BF16)
            pend_b[slot[u]] = kb_t[2 * PAIR_DIM:3 * PAIR_DIM]

    def advance(cg):
        for ci in range(WKV_PREP_CHUNKS):
            c = cg * WKV_PREP_CHUNKS + ci
            UO = [jnp.dot(lhs_b[c * HEAD_PAIRS + j], S[j].astype(BF16), preferred_element_type=F32)
                  + add_b[c * HEAD_PAIRS + j] for j in pairs]
            yield
            for j in pairs:
                u = c * HEAD_PAIRS + j
                S[j] = pend_b[u] * S[j] + vk_b[u] - jnp.dot(bend_b[u], UO[j][0:2 * C].astype(BF16),
                                                            preferred_element_type=F32)
                o_s[c * C:(c + 1) * C, j * PAIR_DIM:(j + 1) * PAIR_DIM] = UO[j][2 * C:3 * C] + UO[j][3 * C:4 * C]
            yield

    n_groups = tt // (C * WKV_PREP_CHUNKS)
    _run_interleaved(prepare(0))
    for cg in range(1, n_groups):
        _run_interleaved(prepare(cg), advance(cg - 1))
    _run_interleaved(advance(n_groups - 1))

    yb = _rwkv_post(o_s[...], bonus, g, gng_ref[...], gnb_ref[...], e_seg)
    y_ref[0, :, POOL_DIM:D_MODEL] = yb.astype(BF16)

    @pl.when(i == nt - 1)
    def _fin():
        opool_ref[0] = hpool[1:16, :]
        oshift_ref[0] = hshift[...]
        for j in range(HEAD_PAIRS):
            S[j] = jnp.transpose(S[j])
        for j in range(HEAD_PAIRS):
            owkv_ref[0, 2 * j] = S[j, 0:HEAD_DIM, 0:HEAD_DIM]
            owkv_ref[0, 2 * j + 1] = S[j, HEAD_DIM:PAIR_DIM, HEAD_DIM:PAIR_DIM]


def _even_prompt(x, g_in, w_in, st_pool, st_shift, st_wkv, prm, start):
    B, T, _ = x.shape
    tt = MIXER_TILE
    nt = T // tt
    bt = lambda b, i: (b, i, 0)
    bs3 = lambda b, i: (b, 0, 0)
    bs4 = lambda b, i: (b, 0, 0, 0)
    c2 = lambda b, i: (0, 0)
    vec = lambda n: pl.BlockSpec((1, n), c2)
    scr = lambda: pltpu.VMEM((tt, RWKV_DIM), F32)
    n_units = (tt // WKV_CHUNK) * HEAD_PAIRS
    return pl.pallas_call(
        functools.partial(_even_prompt_kernel, tt=tt, start=start),
        grid=(B, nt),
        in_specs=[pl.BlockSpec((1, tt, D_MODEL), lambda b, i: (0, 0, 0)),
                  pl.BlockSpec((1, tt, D_MODEL), functools.partial(_next_tile, nt=nt, n_tiles=B * nt)),
                  vec(D_MODEL),
                  pl.BlockSpec((D_MODEL, EVEN_PROJ), c2, pipeline_mode=pl.Buffered(1)),
                  pl.BlockSpec((1, POOL_BUF, POOL_DIM), bs3),
                  pl.BlockSpec((1, 1, RWKV_PROJ), bs3),
                  pl.BlockSpec((1, RWKV_HEADS, HEAD_DIM, HEAD_DIM), bs4),
                  vec(RWKV_PROJ), vec(RWKV_DIM), pl.BlockSpec((LORA_PAD, RWKV_DIM), c2), vec(RWKV_DIM),
                  pl.BlockSpec((LORA_PAD, RWKV_DIM), c2), pl.BlockSpec((LORA_PAD, RWKV_DIM), c2),
                  vec(RWKV_DIM), vec(RWKV_DIM), vec(RWKV_DIM), vec(RWKV_DIM), vec(RWKV_DIM),
                  pl.BlockSpec((SEG_TILE, SEG_TILE), c2), pl.BlockSpec((POOL_DIM, POOL_DIM), c2),
                  vec(POOL_DIM), pl.BlockSpec((WKV_CHUNK, WKV_CHUNK), c2)],
        out_specs=[pl.BlockSpec((1, tt, D_MODEL), bt),
                   pl.BlockSpec((1, POOL_BUF, POOL_DIM), bs3),
                   pl.BlockSpec((1, 1, RWKV_PROJ), bs3),
                   pl.BlockSpec((1, RWKV_HEADS, HEAD_DIM, HEAD_DIM), bs4)],
        out_shape=[jax.ShapeDtypeStruct((B, T, D_MODEL), BF16),
                   jax.ShapeDtypeStruct((B, POOL_BUF, POOL_DIM), F32),
                   jax.ShapeDtypeStruct((B, 1, RWKV_PROJ), F32),
                   jax.ShapeDtypeStruct((B, RWKV_HEADS, HEAD_DIM, HEAD_DIM), F32)],
        scratch_shapes=[pltpu.VMEM((tt, EVEN_PROJ), F32),
                        pltpu.VMEM((16, POOL_DIM), F32), pltpu.VMEM((1, RWKV_PROJ), F32),
                        pltpu.VMEM((HEAD_PAIRS, PAIR_DIM, PAIR_DIM), F32),
                        scr(), scr(), scr(), scr(), scr(), scr(), scr(),
                        pltpu.VMEM((n_units, 2 * PAIR_DIM, PAIR_DIM), BF16),
                        pltpu.VMEM((n_units, 2 * PAIR_DIM, PAIR_DIM), F32),
                        pltpu.VMEM((n_units, PAIR_DIM, PAIR_DIM), F32),
                        pltpu.VMEM((n_units, PAIR_DIM, PAIR_DIM), BF16),
                        pltpu.VMEM((n_units, PAIR_DIM, PAIR_DIM), F32)],
        compiler_params=pltpu.CompilerParams(dimension_semantics=("arbitrary", "arbitrary"),
                                             vmem_limit_bytes=VMEM_LIMIT),
        name="even_prompt",
    )(x, x, g_in, w_in, st_pool, st_shift, st_wkv, *prm)


def _odd_prompt_kernel(x0_ref, xn_ref, gin_ref, win_ref, stc_ref, stl_ref,
                       lng_ref, lnb_ref, ws_ref, bias_ref, cw_ref, cb_ref, wx_ref, bx_ref, wa_ref,
                       ba_ref, lam_ref, y_ref, oconv_ref, olru_ref, q_s, hconv, hl, mix_s, *, tt):
    i = pl.program_id(1)
    nt = pl.num_programs(1)

    @pl.when(jnp.logical_and(i == 0, pl.program_id(0) == 0))
    def _first_projection():
        h0 = _rmsnorm(x0_ref[0], gin_ref[...]).astype(BF16)
        q_s[...] = jnp.dot(h0, win_ref[...], preferred_element_type=F32)

    @pl.when(i == 0)
    def _init():
        hconv[0:5, :] = jnp.zeros((5, LRU_DIM), F32)
        hconv[5:8, :] = stc_ref[0]
        hl[...] = stl_ref[0]

    q = q_s[...]

    def project_next():
        hb = _rmsnorm(xn_ref[0], gin_ref[...]).astype(BF16)
        yield
        for c0 in range(0, ODD_PROJ, PROJ_PIECE):
            q_s[:, c0:c0 + PROJ_PIECE] = jnp.dot(hb, win_ref[:, c0:c0 + PROJ_PIECE],
                                                 preferred_element_type=F32)
            yield

    def mixers():
        u, vn = _gmlp_pre(q[:, 0:2 * GMLP_DIM], lng_ref[...], lnb_ref[...])
        yield
        rr = lax.broadcasted_iota(jnp.int32, (CHUNK, CHUNK), 0)
        cc = lax.broadcasted_iota(jnp.int32, (CHUNK, CHUNK), 1)
        causal = cc <= rr
        for h in range(GMLP_HEADS):
            wm = jnp.where(causal, ws_ref[h], 0.0).astype(BF16)
            ls = slice(h * CHUNK, (h + 1) * CHUNK)
            for c in range(tt // CHUNK):
                rs = slice(c * CHUNK, (c + 1) * CHUNK)
                mix_s[rs, ls] = (jnp.dot(wm, vn[rs, ls].astype(BF16), preferred_element_type=F32)
                                 + bias_ref[:, ls])
        y_ref[0, :, 0:GMLP_DIM] = (u * mix_s[...]).astype(BF16)
        yield

        gate = _gelu(q[:, 2 * GMLP_DIM:2 * GMLP_DIM + LRU_DIM])
        yield
        xr = q[:, 2 * GMLP_DIM + LRU_DIM:ODD_PROJ]
        hconv[8:8 + tt, :] = xr
        xc = xr * cw_ref[3:4, :] + cb_ref[...]
        for j in range(1, CONV_WIDTH):
            xc = xc + hconv[8 - j:8 - j + tt, :] * cw_ref[3 - j:4 - j, :]
        hconv[0:8, :] = hconv[tt:tt + 8, :]
        yield
        a, b = _lru_gates(xc, wx_ref[...], bx_ref[...], wa_ref[...], ba_ref[...], lam_ref[...])
        yield
        n_groups = tt // SCAN_GROUP
        a = a.reshape(n_groups, SCAN_GROUP, LRU_DIM)
        b = b.reshape(n_groups, SCAN_GROUP, LRU_DIM)
        in_group = lax.broadcasted_iota(jnp.int32, (1, SCAN_GROUP, 1), 1)
        dist = 1
        while dist < SCAN_GROUP:
            keep = in_group >= dist
            a_sh = jnp.where(keep, pltpu.roll(a, dist, 1), 1.0)
            b_sh = jnp.where(keep, pltpu.roll(b, dist, 1), 0.0)
            b = a * b_sh + b
            a = a * a_sh
            dist *= 2
            yield
        carry = hl[...]
        groups = []
        for gi in range(n_groups):
            hg = a[gi] * carry + b[gi]
            groups.append(hg)
            carry = hg[SCAN_GROUP - 1:SCAN_GROUP, :]
        h = jnp.concatenate(groups, axis=0)
        hl[...] = carry
        y_ref[0, :, GMLP_DIM:D_MODEL] = (h * gate).astype(BF16)
        yield

    _run_interleaved(mixers(), project_next())

    @pl.when(i == nt - 1)
    def _fin():
        oconv_ref[0] = hconv[5:8, :]
        olru_ref[0] = hl[...]


def _odd_prompt(x, g_in, w_in, st_conv, st_lru, prm):
    B, T, _ = x.shape
    tt = MIXER_TILE
    nt = T // tt
    bt = lambda b, i: (b, i, 0)
    bs3 = lambda b, i: (b, 0, 0)
    c2 = lambda b, i: (0, 0)
    c3 = lambda b, i: (0, 0, 0)
    vec = lambda n: pl.BlockSpec((1, n), c2)
    return pl.pallas_call(
        functools.partial(_odd_prompt_kernel, tt=tt),
        grid=(B, nt),
        in_specs=[pl.BlockSpec((1, tt, D_MODEL), lambda b, i: (0, 0, 0)),
                  pl.BlockSpec((1, tt, D_MODEL), functools.partial(_next_tile, nt=nt, n_tiles=B * nt)),
                  vec(D_MODEL),
                  pl.BlockSpec((D_MODEL, ODD_PROJ), c2, pipeline_mode=pl.Buffered(1)),
                  pl.BlockSpec((1, CONV_WIDTH - 1, LRU_DIM), bs3),
                  pl.BlockSpec((1, 1, LRU_DIM), bs3),
                  vec(GMLP_DIM), vec(GMLP_DIM),
                  pl.BlockSpec((GMLP_HEADS, CHUNK, CHUNK), c3),
                  pl.BlockSpec((CHUNK, GMLP_DIM), c2),
                  pl.BlockSpec((CONV_WIDTH, LRU_DIM), c2), vec(LRU_DIM),
                  pl.BlockSpec((LRU_DIM, LRU_DIM), c2), vec(LRU_DIM),
                  pl.BlockSpec((LRU_DIM, LRU_DIM), c2), vec(LRU_DIM), vec(LRU_DIM)],
        out_specs=[pl.BlockSpec((1, tt, D_MODEL), bt),
                   pl.BlockSpec((1, CONV_WIDTH - 1, LRU_DIM), bs3),
                   pl.BlockSpec((1, 1, LRU_DIM), bs3)],
        out_shape=[jax.ShapeDtypeStruct((B, T, D_MODEL), BF16),
                   jax.ShapeDtypeStruct((B, CONV_WIDTH - 1, LRU_DIM), F32),
                   jax.ShapeDtypeStruct((B, 1, LRU_DIM), F32)],
        scratch_shapes=[pltpu.VMEM((tt, ODD_PROJ), F32),
                        pltpu.VMEM((8 + tt, LRU_DIM), F32), pltpu.VMEM((1, LRU_DIM), F32),
                        pltpu.VMEM((tt, GMLP_DIM), F32)],
        compiler_params=pltpu.CompilerParams(dimension_semantics=("arbitrary", "arbitrary"),
                                             vmem_limit_bytes=VMEM_LIMIT),
        name="odd_prompt",
    )(x, x, g_in, w_in, st_conv, st_lru, *prm)


def _even_sample_pre_kernel(p_ref, stp_ref, sts_ref,
                            mu_ref, w0_ref, wdec_ref, a0_ref, wa_ref, gw2_ref, kk_ref, ka_ref, rk_ref,
                            eseg_ref, poolw_ref, pools_ref,
                            r_ref, w_ref, kkn_ref, kka_ref, kp_ref, v_ref, g_ref, bonus_ref, ya_ref,
                            opool_ref, oshift_ref, *, T, start):
    prm = (mu_ref[...], w0_ref[...], wdec_ref[...], a0_ref[...], wa_ref[...], gw2_ref[...],
           kk_ref[...], ka_ref[...], rk_ref[...], eseg_ref[...])
    full = [stp_ref[s] for s in range(POOL_BUF)] + [p_ref[t][:, 0:POOL_DIM] for t in range(T)]
    wl = _pool_window_lanes()
    for t in range(T):
        P = p_ref[t][:, POOL_DIM:EVEN_PROJ]
        Pprev = sts_ref[...] if t == 0 else p_ref[t - 1][:, POOL_DIM:EVEN_PROJ]
        r, kp, v, ld, kk, a, g, bonus = _rwkv_pointwise(P, Pprev, prm)
        r_ref[t] = jnp.transpose(r)
        w_ref[t] = jnp.transpose(jnp.exp(ld))
        kkn_ref[t] = jnp.transpose(kk)
        kka_ref[t] = jnp.transpose(kk * a)
        kp_ref[t] = jnp.transpose(kp)
        v_ref[t] = jnp.transpose(v)
        g_ref[t] = g
        bonus_ref[t] = bonus
        e = POOL_BUF + t
        s2 = full[e] + full[e - 1]
        s4 = s2 + full[e - 2] + full[e - 3]
        s8 = s4 + full[e - 4] + full[e - 5] + full[e - 6] + full[e - 7]
        s16 = s8
        for s in range(8, 16):
            s16 = s16 + full[e - s]
        sel = _pool_lane_select(s2, s4, s8, s16)
        cnt = jnp.minimum(wl, start + t + 1).astype(F32)
        d = sel / cnt - full[e]
        ya_ref[t] = _bdot(d, poolw_ref[...]) * pools_ref[...]
    for s in range(POOL_BUF):
        opool_ref[s] = full[T + s]
    oshift_ref[...] = p_ref[T - 1][:, POOL_DIM:EVEN_PROJ]


def _even_sample_pre(p, st_pool, st_shift, prm, start):
    T, B, _ = p.shape
    cm = jax.ShapeDtypeStruct((T, RWKV_DIM, B), F32)
    bm = jax.ShapeDtypeStruct((T, B, RWKV_DIM), F32)
    return pl.pallas_call(
        functools.partial(_even_sample_pre_kernel, T=T, start=start),
        out_shape=[cm] * 6 + [bm] * 2 + [jax.ShapeDtypeStruct((T, B, POOL_DIM), F32),
                                   jax.ShapeDtypeStruct((POOL_BUF, B, POOL_DIM), F32),
                                   jax.ShapeDtypeStruct((B, RWKV_PROJ), F32)],
        compiler_params=pltpu.CompilerParams(vmem_limit_bytes=VMEM_LIMIT),
        name="even_sample_pre",
    )(p, st_pool, st_shift, *prm)


def _wkv_sample_kernel(r_ref, w_ref, kk_ref, kka_ref, kp_ref, v_ref, s_ref, o_ref, so_ref, *, T):
    group = range(WKV_SAMPLE_GROUP)

    def body(ib, carry):
        v0 = pl.multiple_of(ib * WKV_SAMPLE_GROUP, WKV_SAMPLE_GROUP)
        blk = pl.ds(v0, WKV_SAMPLE_GROUP)
        S = [s_ref[0, v0 + u] for u in group]
        for t in range(T):
            kk, w, kka, kp, r = kk_ref[t], w_ref[t], kka_ref[t], kp_ref[t], r_ref[t]
            vv = v_ref[t, blk, :]
            sk = [jnp.sum(S[u] * kk, axis=0, keepdims=True) for u in group]
            S = [S[u] * w - sk[u] * kka + vv[u:u + 1, :] * kp for u in group]
            o_ref[t, blk, :] = jnp.concatenate(
                [jnp.sum(S[u] * r, axis=0, keepdims=True) for u in group], axis=0)
        for u in group:
            so_ref[0, v0 + u] = S[u]
        return carry

    lax.fori_loop(0, HEAD_DIM // WKV_SAMPLE_GROUP, body, 0)


def _wkv_sample(r, w, kk, kka, kp, v, s):
    T, _, B = r.shape
    row_spec = pl.BlockSpec((T, HEAD_DIM, B), lambda h: (0, h, 0))
    st_spec = pl.BlockSpec((1, HEAD_DIM, HEAD_DIM, B), lambda h: (h, 0, 0, 0))
    return pl.pallas_call(
        functools.partial(_wkv_sample_kernel, T=T),
        grid=(RWKV_HEADS,),
        in_specs=[row_spec] * 6 + [st_spec],
        out_specs=[row_spec, st_spec],
        out_shape=[jax.ShapeDtypeStruct((T, RWKV_DIM, B), F32),
                   jax.ShapeDtypeStruct((RWKV_HEADS, HEAD_DIM, HEAD_DIM, B), F32)],
        compiler_params=pltpu.CompilerParams(dimension_semantics=("arbitrary",),
                                             vmem_limit_bytes=VMEM_LIMIT),
        name="wkv_sample",
    )(r, w, kk, kka, kp, v, s)


def _even_sample_post_kernel(o_ref, bonus_ref, g_ref, ya_ref, gng_ref, gnb_ref, eseg_ref, y_ref, *, T):
    for t in range(T):
        o = jnp.transpose(o_ref[t])
        yb = _rwkv_post(o, bonus_ref[t], g_ref[t], gng_ref[...], gnb_ref[...], eseg_ref[...])
        y_ref[t, :, 0:POOL_DIM] = ya_ref[t].astype(BF16)
        y_ref[t, :, POOL_DIM:D_MODEL] = yb.astype(BF16)


def _even_sample_post(o, bonus, g, ya, gn_g, gn_b, e_seg):
    T, _, B = o.shape
    return pl.pallas_call(
        functools.partial(_even_sample_post_kernel, T=T),
        out_shape=jax.ShapeDtypeStruct((T, B, D_MODEL), BF16),
        compiler_params=pltpu.CompilerParams(vmem_limit_bytes=VMEM_LIMIT),
        name="even_sample_post",
    )(o, bonus, g, ya, gn_g, gn_b, e_seg)


def _odd_sample_kernel(q_ref, stc_ref, stl_ref,
                       lng_ref, lnb_ref, wsm_ref, bsm_ref, cw_ref, cb_ref, wx_ref, bx_ref, wa_ref,
                       ba_ref, lam_ref, y_ref, v_ref, oconv_ref, olru_ref, *, T):
    vns = []
    us = []
    for t in range(T):
        u, vn = _gmlp_pre(q_ref[t][:, 0:2 * GMLP_DIM], lng_ref[...], lnb_ref[...])
        us.append(u)
        vns.append(vn)
        v_ref[t] = vn
    full = [stc_ref[s] for s in range(CONV_WIDTH - 1)] + \
           [q_ref[t][:, 2 * GMLP_DIM + LRU_DIM:ODD_PROJ] for t in range(T)]
    h = stl_ref[...]
    for t in range(T):
        mix = bsm_ref[t:t + 1, :]
        for j in range(t + 1):
            mix = mix + wsm_ref[t * T + j:t * T + j + 1, :] * vns[j]
        y_ref[t, :, 0:GMLP_DIM] = (us[t] * mix).astype(BF16)
        xc = full[t + CONV_WIDTH - 1] * cw_ref[CONV_WIDTH - 1:CONV_WIDTH, :] + cb_ref[...]
        for j in range(CONV_WIDTH - 1):
            xc = xc + full[t + j] * cw_ref[j:j + 1, :]
        a, b = _lru_gates(xc, wx_ref[...], bx_ref[...], wa_ref[...], ba_ref[...], lam_ref[...])
        h = a * h + b
        gate_in = q_ref[t][:, 2 * GMLP_DIM:2 * GMLP_DIM + LRU_DIM]
        y_ref[t, :, GMLP_DIM:D_MODEL] = (h * _gelu(gate_in)).astype(BF16)
    for s in range(CONV_WIDTH - 1):
        oconv_ref[s] = full[T + s]
    olru_ref[...] = h


def _odd_sample(q, st_conv, st_lru, prm):
    T, B, _ = q.shape
    return pl.pallas_call(
        functools.partial(_odd_sample_kernel, T=T),
        out_shape=[jax.ShapeDtypeStruct((T, B, D_MODEL), BF16),
                   jax.ShapeDtypeStruct((T, B, GMLP_DIM), F32),
                   jax.ShapeDtypeStruct((CONV_WIDTH - 1, B, LRU_DIM), F32),
                   jax.ShapeDtypeStruct((B, LRU_DIM), F32)],
        compiler_params=pltpu.CompilerParams(vmem_limit_bytes=VMEM_LIMIT),
        name="odd_sample",
    )(q, st_conv, st_lru, *prm)


def _block_diag(w):
    n, c, d = w.shape
    eye = jnp.eye(n, dtype=w.dtype)
    return (eye[:, None, :, None] * w[:, :, None, :]).reshape(n * c, n * d)


def _row(x):
    return x.reshape(1, -1)


def kernel(x_prompt, x_sample, state_pool, state_shift, state_wkv, state_conv, state_lru, ev_norm_g, ev_w_in, pool_w, pool_scale, rwkv_mu, rwkv_w0, rwkv_w_w2, rwkv_a0, rwkv_a_w2, rwkv_g_w2, rwkv_k_k, rwkv_k_a, rwkv_r_k, rwkv_gn_g, rwkv_gn_b, ev_w_out, od_norm_g, od_w_in, gmlp_ln_g, gmlp_ln_b, gmlp_ws, gmlp_bs, lru_conv_w, lru_conv_b, lru_wx, lru_bx, lru_wa, lru_ba, lru_lam, od_w_out, ff_norm_g, ff_w1, ff_w2, final_norm_g):
    B, T, _ = x_prompt.shape
    DB, DT, _ = x_sample.shape

    seg_ids = jnp.arange(SEG_TILE) // HEAD_DIM
    e_seg = (seg_ids[:, None] == seg_ids[None, :]).astype(BF16)
    tri = (jnp.arange(WKV_CHUNK)[None, :] <= jnp.arange(WKV_CHUNK)[:, None]).astype(BF16)
    zlora = jnp.zeros((LORA_PAD // 2, RWKV_DIM), F32)

    ev_common = (_row(rwkv_mu[0]), _row(rwkv_w0[0]),
                 jnp.concatenate([rwkv_w_w2[0], zlora], 0).astype(BF16), _row(rwkv_a0[0]),
                 jnp.concatenate([zlora, rwkv_a_w2[0]], 0).astype(BF16), rwkv_g_w2[0].astype(BF16),
                 _row(rwkv_k_k[0]), _row(rwkv_k_a[0]), _row(rwkv_r_k[0]))
    gn_g, gn_b = _row(rwkv_gn_g[0]), _row(rwkv_gn_b[0])
    pool_bd = _block_diag(pool_w[0]).astype(BF16)
    pool_sc = _row(pool_scale[0])
    w_in0 = ev_w_in[0].astype(BF16)
    g_in0 = _row(ev_norm_g[0])

    xp = x_prompt.reshape(B * T, D_MODEL)
    xs = jnp.transpose(x_sample, (1, 0, 2)).reshape(DT * DB, D_MODEL)

    ps = _inproj(xs, g_in0, w_in0).reshape(DT, DB, EVEN_PROJ)

    yp, p_pool, p_shift, p_wkv = _even_prompt(
        x_prompt, g_in0, w_in0,
        jnp.zeros((B, POOL_BUF, POOL_DIM), F32), jnp.zeros((B, 1, RWKV_PROJ), F32),
        jnp.zeros((B, RWKV_HEADS, HEAD_DIM, HEAD_DIM), F32),
        ev_common + (gn_g, gn_b, e_seg, pool_bd, pool_sc, tri), 0)

    pre = _even_sample_pre(ps, jnp.transpose(state_pool[0], (1, 0, 2)), state_shift[0],
                           ev_common + (e_seg, pool_bd, pool_sc), PAST_LEN)
    r_s, w_s, kk_s, kka_s, kp_s, v_s, g_s, bonus_s, ya_s, s_pool_tm, s_shift = pre
    o_s, s_wkv_bl = _wkv_sample(r_s, w_s, kk_s, kka_s, kp_s, v_s,
                                jnp.transpose(state_wkv[0], (1, 2, 3, 0)))
    ys = _even_sample_post(o_s, bonus_s, g_s, ya_s, gn_g, gn_b, e_seg)

    w_out0 = ev_w_out[0].astype(BF16)
    ffg = lambda l: _row(ff_norm_g[l])
    gfin = _row(final_norm_g)
    ff_w1_b, ff_w2_b = ff_w1.astype(BF16), ff_w2.astype(BF16)
    xp = _ffn(xp, yp.reshape(B * T, D_MODEL), w_out0, ffg(0), ff_w1_b, ff_w2_b, gfin, 0, False)
    xs = _ffn(xs, ys.reshape(DT * DB, D_MODEL), w_out0, ffg(0), ff_w1_b, ff_w2_b, gfin, 0, False)

    w_in1 = od_w_in[0].astype(BF16)
    g_in1 = _row(od_norm_g[0])
    qs = _inproj(xs, g_in1, w_in1).reshape(DT, DB, ODD_PROJ)

    lru_common = (lru_conv_w[0], _row(lru_conv_b[0]), _block_diag(lru_wx[0]).astype(BF16), _row(lru_bx[0]),
                  _block_diag(lru_wa[0]).astype(BF16), _row(lru_ba[0]), _row(lru_lam[0]))
    ln = (_row(gmlp_ln_g[0]), _row(gmlp_ln_b[0]))
    bias_full = jnp.repeat(jnp.transpose(gmlp_bs[0]), CHUNK, axis=1)
    yp, p_conv, p_lru = _odd_prompt(
        xp.reshape(B, T, D_MODEL), g_in1, w_in1,
        jnp.zeros((B, CONV_WIDTH - 1, LRU_DIM), F32), jnp.zeros((B, 1, LRU_DIM), F32),
        ln + (gmlp_ws[0], bias_full) + lru_common)

    ws_small = jnp.repeat(jnp.transpose(gmlp_ws[0][:, :DT, :DT], (1, 2, 0)).reshape(DT * DT, GMLP_HEADS),
                          CHUNK, axis=1)
    ys, s_v, s_conv_tm, s_lru = _odd_sample(
        qs, jnp.transpose(state_conv[0], (1, 0, 2)), state_lru[0],
        ln + (ws_small, bias_full[:DT]) + lru_common)

    w_out1 = od_w_out[0].astype(BF16)
    xp = _ffn(xp, yp.reshape(B * T, D_MODEL), w_out1, ffg(1), ff_w1_b, ff_w2_b, gfin, 1, True)
    xs = _ffn(xs, ys.reshape(DT * DB, D_MODEL), w_out1, ffg(1), ff_w1_b, ff_w2_b, gfin, 1, True)

    tm2bm = lambda t: jnp.transpose(t, (1, 0, 2))
    y_prompt = xp.reshape(B, T, D_MODEL)
    y_sample = tm2bm(xs.reshape(DT, DB, D_MODEL))
    return (y_prompt, y_sample,
            p_pool[None], p_shift.reshape(1, B, RWKV_PROJ), p_wkv[None],
            p_conv[None], p_lru.reshape(1, B, LRU_DIM),
            tm2bm(s_pool_tm)[None], s_shift[None],
            jnp.transpose(s_wkv_bl, (3, 0, 1, 2))[None],
            tm2bm(s_conv_tm)[None], s_lru[None], tm2bm(s_v)[None])
```

```python
import functools

import jax
import jax.numpy as jnp
from jax import lax
from jax.experimental import pallas as pl
from jax.experimental.pallas import tpu as pltpu

F32 = jnp.float32
BF16 = jnp.bfloat16

D_MODEL = 1024
NORM_EPS = 1e-6
D_FF = 4 * D_MODEL

POOL_WINDOWS = (2, 4, 8, 16)
POOL_GROUP_DIM = 64
POOL_DIM = 256
POOL_BUF = 15

HEAD_DIM = 64
RWKV_DIM = 768
RWKV_HEADS = 12
HEAD_PAIRS = RWKV_HEADS // 2
PAIR_DIM = 2 * HEAD_DIM
RWKV_PROJ = 2560
RWKV_GN_EPS = 64e-5
EXP_NEG_HALF = 0.6065306597126334
EVEN_PROJ = POOL_DIM + RWKV_PROJ
LORA_OFF = 3 * RWKV_DIM
LORA_PAD = 128
GATE_OFF = LORA_OFF + LORA_PAD

CHUNK = 128
GMLP_DIM = 512
GMLP_HEADS = 4
LN_EPS = 1e-5
GELU_C = 0.7978845608028654
LRU_DIM = 512
CONV_WIDTH = 4
LRU_C = 8.0
ODD_PROJ = 2048

WKV_CHUNK = 64
SEG_TILE = 256
LANES = 128
SCAN_GROUP = 8
PROJ_PIECE = 256
WKV_PREP_CHUNKS = 2
WKV_SAMPLE_GROUP = 8

PAST_LEN = 16384
MIXER_TILE = 512
DENSE_TILE = 1024

VMEM_LIMIT = 48 * 1024 * 1024
VMEM_LIMIT_BIG = 56 * 1024 * 1024


def _bdot(a, b):
    return jnp.dot(a.astype(BF16), b.astype(BF16), preferred_element_type=F32)


def _bdot_nt(a, b):
    return lax.dot_general(a.astype(BF16), b.astype(BF16), (((1,), (1,)), ((), ())),
                           preferred_element_type=F32)


def _split3(x):
    hi = x.astype(BF16)
    r1 = x - hi.astype(F32)
    mid = r1.astype(BF16)
    lo = (r1 - mid.astype(F32)).astype(BF16)
    return hi, mid, lo


def _exact_dot_rhs01(x, e):
    hi = x.astype(BF16)
    lo = (x - hi.astype(F32)).astype(BF16)
    d = lambda t: jnp.dot(t, e, preferred_element_type=F32)
    return d(hi) + d(lo)


def _exact_dot_lhs01(e, x):
    hi, mid, lo = _split3(x)
    d = lambda t: jnp.dot(e, t, preferred_element_type=F32)
    return d(hi) + d(mid) + d(lo)


def _segsum(x, e_seg):
    parts = [_exact_dot_rhs01(x[:, g * SEG_TILE:(g + 1) * SEG_TILE], e_seg)
             for g in range(RWKV_DIM // SEG_TILE)]
    return jnp.concatenate(parts, axis=1)


def _softplus(z):
    return jnp.maximum(z, 0.0) + jnp.log(1.0 + jnp.exp(-jnp.abs(z)))


def _sigmoid(z):
    return 0.5 * jnp.tanh(0.5 * z) + 0.5


def _gelu(z):
    hz = 0.5 * z
    return hz + hz * jnp.tanh(z * (GELU_C + (GELU_C * 0.044715) * (z * z)))


def _rmsnorm(x, g):
    ms = jnp.mean(x * x, axis=-1, keepdims=True)
    return x * lax.rsqrt(ms + NORM_EPS) * g


def _next_tile(b, i, *, nt, n_tiles):
    n = jnp.minimum(b * nt + i + 1, n_tiles - 1)
    return (n // nt, n % nt, 0)


def _run_interleaved(*stages):
    live = list(stages)
    while live:
        for gen in list(live):
            if next(gen, StopIteration) is StopIteration:
                live.remove(gen)


_POINTWISE_KEYS = ("r", "kp", "v", "ld", "kk", "a", "g", "bonus")


def _rwkv_pointwise_stages(P, Pprev, prm, out):
    (mu, w0, wdec, a0, wa, gw2, k_k, k_a, r_k, e_seg) = prm
    xs = P + (Pprev - P) * mu
    r = xs[:, 0:RWKV_DIM]
    k = xs[:, RWKV_DIM:2 * RWKV_DIM]
    v = xs[:, 2 * RWKV_DIM:3 * RWKV_DIM]
    c_wa = xs[:, LORA_OFF:GATE_OFF]
    cg = xs[:, GATE_OFF:RWKV_PROJ]
    yield
    ld = -EXP_NEG_HALF * _sigmoid(w0 + _bdot(jnp.tanh(c_wa), wdec))
    yield
    a = _sigmoid(a0 + _bdot(c_wa, wa))
    yield
    g = _bdot(_sigmoid(cg), gw2)
    yield
    kk = k * k_k
    kk = kk * lax.rsqrt(jnp.maximum(_segsum(kk * kk, e_seg), 1e-24))
    yield
    kp = k * (1.0 + (a - 1.0) * k_a)
    yield
    bonus = _segsum(r * kp * r_k, e_seg) * v
    out.update(r=r, kp=kp, v=v, ld=ld, kk=kk, a=a, g=g, bonus=bonus)
    yield


def _rwkv_pointwise(P, Pprev, prm):
    out = {}
    for _ in _rwkv_pointwise_stages(P, Pprev, prm, out):
        pass
    return tuple(out[key] for key in _POINTWISE_KEYS)


def _rwkv_post(o, bonus, g, gn_g, gn_b, e_seg):
    m = _segsum(o, e_seg) * (1.0 / HEAD_DIM)
    d = o - m
    var = _segsum(d * d, e_seg) * (1.0 / HEAD_DIM)
    on = d * lax.rsqrt(var + RWKV_GN_EPS) * gn_g + gn_b
    return (on + bonus) * g


def _pool_lane_select(s2, s4, s8, s16):
    lane = lax.broadcasted_iota(jnp.int32, (1, POOL_DIM), 1)
    return jnp.where(lane < 64, s2, jnp.where(lane < 128, s4, jnp.where(lane < 192, s8, s16)))


def _pool_window_lanes():
    lane = lax.broadcasted_iota(jnp.int32, (1, POOL_DIM), 1)
    return jnp.where(lane < 64, 2, jnp.where(lane < 128, 4, jnp.where(lane < 192, 8, 16)))


def _lru_gates(xc, wx, bx, wa, ba, lam):
    gx = _sigmoid(_bdot(xc, wx) + bx)
    ga = _sigmoid(_bdot(xc, wa) + ba)
    log_a = -LRU_C * ga * _softplus(-lam)
    a = jnp.exp(log_a)
    b = jnp.sqrt(-jnp.tanh(log_a) * (a * a + 1.0)) * gx * xc
    return a, b


def _gmlp_pre(zq, ln_g, ln_b):
    z = _gelu(zq)
    u = z[:, :GMLP_DIM]
    v = z[:, GMLP_DIM:]
    m = jnp.mean(v, axis=-1, keepdims=True)
    d = v - m
    var = jnp.mean(d * d, axis=-1, keepdims=True)
    return u, d * lax.rsqrt(var + LN_EPS) * ln_g + ln_b


def _inproj_kernel(x_ref, g_ref, w_ref, o_ref):
    h = _rmsnorm(x_ref[...], g_ref[...])
    o_ref[...] = jnp.dot(h.astype(BF16), w_ref[...], preferred_element_type=F32)


def _inproj(x, g, w):
    m, n = x.shape[0], w.shape[1]
    tm = min(DENSE_TILE, m)
    return pl.pallas_call(
        _inproj_kernel,
        grid=(m // tm,),
        in_specs=[pl.BlockSpec((tm, D_MODEL), lambda i: (i, 0)),
                  pl.BlockSpec((1, D_MODEL), lambda i: (0, 0)),
                  pl.BlockSpec((D_MODEL, n), lambda i: (0, 0), pipeline_mode=pl.Buffered(1))],
        out_specs=pl.BlockSpec((tm, n), lambda i: (i, 0)),
        out_shape=jax.ShapeDtypeStruct((m, n), F32),
        compiler_params=pltpu.CompilerParams(dimension_semantics=("arbitrary",),
                                             vmem_limit_bytes=VMEM_LIMIT),
        name="inproj",
    )(x, g, w)


def _ffn_kernel(x_ref, y_ref, wo_ref, g_ref, w1_ref, w2_ref, gf_ref, o_ref, *, final):
    x1 = x_ref[...] + jnp.dot(y_ref[...], wo_ref[...], preferred_element_type=F32)
    hf = _rmsnorm(x1, g_ref[...]).astype(BF16)
    acc = x1
    fc = 1024
    for c in range(D_FF // fc):
        h = jnp.dot(hf, w1_ref[:, c * fc:(c + 1) * fc], preferred_element_type=F32)
        h = jnp.square(jnp.maximum(h, 0.0)).astype(BF16)
        acc = acc + jnp.dot(h, w2_ref[c * fc:(c + 1) * fc, :], preferred_element_type=F32)
    if final:
        acc = _rmsnorm(acc, gf_ref[...])
    o_ref[...] = acc


def _ffn(x, y, wo, g, w1, w2, gf, layer, final):
    m = x.shape[0]
    tm = min(DENSE_TILE, m)
    const = lambda i: (0, 0)
    pick = lambda i: (layer, 0, 0)
    return pl.pallas_call(
        functools.partial(_ffn_kernel, final=final),
        grid=(m // tm,),
        in_specs=[pl.BlockSpec((tm, D_MODEL), lambda i: (i, 0)),
                  pl.BlockSpec((tm, D_MODEL), lambda i: (i, 0)),
                  pl.BlockSpec((D_MODEL, D_MODEL), const, pipeline_mode=pl.Buffered(1)),
                  pl.BlockSpec((1, D_MODEL), const),
                  pl.BlockSpec((None, D_MODEL, D_FF), pick, pipeline_mode=pl.Buffered(1)),
                  pl.BlockSpec((None, D_FF, D_MODEL), pick, pipeline_mode=pl.Buffered(1)),
                  pl.BlockSpec((1, D_MODEL), const)],
        out_specs=pl.BlockSpec((tm, D_MODEL), lambda i: (i, 0)),
        out_shape=jax.ShapeDtypeStruct((m, D_MODEL), F32),
        compiler_params=pltpu.CompilerParams(
            dimension_semantics=("arbitrary",),
            vmem_limit_bytes=VMEM_LIMIT_BIG if tm == DENSE_TILE else VMEM_LIMIT),
        name="ffn",
    )(x, y, wo, g, w1, w2, gf)


def _even_prompt_kernel(x0_ref, xn_ref, gin_ref, win_ref, stp_ref, sts_ref, stw_ref,
                        mu_ref, w0_ref, wdec_ref, a0_ref, wa_ref, gw2_ref, kk_ref, ka_ref, rk_ref,
                        gng_ref, gnb_ref, eseg_ref, poolw_ref, pools_ref, tri_ref,
                        y_ref, opool_ref, oshift_ref, owkv_ref,
                        p_s, hpool, hshift, S, r_s, kp_s, v_s, ld_s, kk_s, a_s, o_s,
                        lhs_b, add_b, vk_b, bend_b, pend_b, *, tt, start):
    i = pl.program_id(1)
    nt = pl.num_programs(1)
    C = WKV_CHUNK

    @pl.when(jnp.logical_and(i == 0, pl.program_id(0) == 0))
    def _first_projection():
        h0 = _rmsnorm(x0_ref[0], gin_ref[...]).astype(BF16)
        p_s[...] = jnp.dot(h0, win_ref[...], preferred_element_type=F32)

    @pl.when(i == 0)
    def _init():
        hpool[0:1, :] = jnp.zeros((1, POOL_DIM), F32)
        hpool[1:16, :] = stp_ref[0]
        hshift[...] = sts_ref[0]
        S[...] = jnp.zeros(S.shape, F32)
        for j in range(HEAD_PAIRS):
            S[j, 0:HEAD_DIM, 0:HEAD_DIM] = stw_ref[0, 2 * j]
            S[j, HEAD_DIM:PAIR_DIM, HEAD_DIM:PAIR_DIM] = stw_ref[0, 2 * j + 1]
        for j in range(HEAD_PAIRS):
            S[j] = jnp.transpose(S[j])

    p = p_s[...]
    rows = lax.broadcasted_iota(jnp.int32, (tt, 1), 0)
    e_seg = eseg_ref[...]
    prm = (mu_ref[...], w0_ref[...], wdec_ref[...], a0_ref[...], wa_ref[...], gw2_ref[...],
           kk_ref[...], ka_ref[...], rk_ref[...], e_seg)
    pw = {}

    def project_next():
        hb = _rmsnorm(xn_ref[0], gin_ref[...]).astype(BF16)
        yield
        for c0 in range(0, EVEN_PROJ, PROJ_PIECE):
            p_s[:, c0:c0 + PROJ_PIECE] = jnp.dot(hb, win_ref[:, c0:c0 + PROJ_PIECE],
                                                 preferred_element_type=F32)
            yield

    def pointwise():
        u = p[:, 0:POOL_DIM]
        ext = jnp.concatenate([hpool[...], u], axis=0)
        s2 = ext + pltpu.roll(ext, 1, 0)
        s4 = s2 + pltpu.roll(s2, 2, 0)
        s8 = s4 + pltpu.roll(s4, 4, 0)
        s16 = s8 + pltpu.roll(s8, 8, 0)
        sel = _pool_lane_select(s2, s4, s8, s16)[16:, :]
        pos = start + i * tt + rows
        cnt = jnp.minimum(_pool_window_lanes(), pos + 1).astype(F32)
        d = sel / cnt - u
        y_ref[0, :, 0:POOL_DIM] = (_bdot(d, poolw_ref[...]) * pools_ref[...]).astype(BF16)
        hpool[...] = ext[tt:tt + 16, :]
        yield
        P = p[:, POOL_DIM:EVEN_PROJ]
        Pprev = jnp.where(rows == 0, hshift[...], pltpu.roll(P, 1, 0))
        hshift[...] = P[tt - 1:tt, :]
        yield
        yield from _rwkv_pointwise_stages(P, Pprev, prm, pw)
        r_s[...] = pw["r"]
        kp_s[...] = pw["kp"]
        v_s[...] = pw["v"]
        ld_s[...] = pw["ld"]
        kk_s[...] = pw["kk"]
        a_s[...] = pw["a"]
        yield

    _run_interleaved(pointwise(), project_next())
    g, bonus = pw["g"], pw["bonus"]

    lane_c = lax.broadcasted_iota(jnp.int32, (C, PAIR_DIM), 1)
    row_c = lax.broadcasted_iota(jnp.int32, (C, PAIR_DIM), 0)
    head0 = lane_c < HEAD_DIM
    left = lane_c < C
    lo_strict = left & (lane_c < row_c)
    lo_incl = left & (lane_c <= row_c)
    hi_strict = jnp.logical_not(left) & (lane_c - C < row_c)
    hi_incl = jnp.logical_not(left) & (lane_c - C <= row_c)
    eye_r = (lane_c - C == row_c).astype(F32)
    zb = jnp.zeros((C, PAIR_DIM), BF16)
    zbw = jnp.zeros((C, 2 * PAIR_DIM), BF16)
    tri = tri_ref[...]
    pairs = range(HEAD_PAIRS)

    def stack_heads(x):
        z = jnp.zeros_like(x)
        return jnp.concatenate([jnp.where(head0, x, z), jnp.where(head0, z, x)], axis=0)

    def prepare(cg):
        qa_sm, qr_sm, v_sm, rhs_g, kb_src, slot = [], [], [], [], [], []
        for ci in range(WKV_PREP_CHUNKS):
            c = cg * WKV_PREP_CHUNKS + ci
            sl = slice(c * C, (c + 1) * C)
            R = r_s[sl, :]
            K = kp_s[sl, :]
            V = v_s[sl, :]
            LD = ld_s[sl, :]
            KK = kk_s[sl, :]
            KA = KK * a_s[sl, :]
            L = _exact_dot_lhs01(tri, LD)
            Lend = L[C - 1:C, :]
            enL = jnp.exp(-L)
            eE = jnp.exp(Lend - L)
            Qr = R * jnp.exp(L)
            Qa = KK * jnp.exp(L - LD)
            Kt = K * enL
            Bt = KA * enL
            Kend = K * eE
            Bend = KA * eE
            Pend = jnp.exp(Lend)
            for j in pairs:
                ls = slice(j * PAIR_DIM, (j + 1) * PAIR_DIM)
                qa_sm.append(stack_heads(Qa[:, ls]))
                qr_sm.append(stack_heads(Qr[:, ls]))
                v_sm.append(stack_heads(V[:, ls]).astype(BF16))
                bt = Bt[:, ls].astype(BF16)
                kt = Kt[:, ls].astype(BF16)
                rhs_g.append(jnp.concatenate([bt, kt], axis=0))
                kb_src.append((Kend[:, ls], Bend[:, ls], Pend[:, ls]))
                slot.append(c * HEAD_PAIRS + j)
        units = range(len(slot))
        heads = [(u, h) for u in units for h in range(2)]
        hrows = lambda x, h: x[h * C:(h + 1) * C]
        G = [_bdot_nt(jnp.concatenate([qa_sm[u], qr_sm[u]], axis=0), rhs_g[u]) for u in units]
        yield
        GA = [hrows(G[u], h) for u, h in heads]
        GR = [hrows(G[u], 2 + h) for u, h in heads]
        R = [jnp.where(lo_strict, -GA[k], eye_r) for k in range(len(heads))]
        level = 1
        while level < C:
            for k in range(len(heads)):
                rb = R[k].astype(BF16)
                P2 = jnp.dot(rb, jnp.concatenate([rb, zb], axis=0),
                             preferred_element_type=F32)
                R[k] = P2 + jnp.where(left, 0.0, R[k])
            level *= 2
            yield
        AV = []
        for k, (u, h) in enumerate(heads):
            lhs = jnp.concatenate([jnp.where(hi_strict, GA[k], 0.0), jnp.where(hi_incl, GR[k], 0.0)], axis=0)
            AV.append(jnp.dot(lhs.astype(BF16), jnp.concatenate([zb, hrows(v_sm[u], h)], axis=0),
                              preferred_element_type=F32))
        yield
        TQ = []
        for k, (u, h) in enumerate(heads):
            rhs = jnp.concatenate([hrows(qa_sm[u], h), AV[k][0:C]], axis=1).astype(BF16)
            TQ.append(jnp.dot(R[k].astype(BF16), jnp.concatenate([zbw, rhs], axis=0),
                              preferred_element_type=F32))
        yield
        AT = []
        for k, (u, h) in enumerate(heads):
            AT.append(jnp.dot(jnp.where(lo_incl, GR[k], 0.0).astype(BF16),
                              jnp.concatenate([TQ[k].astype(BF16), zbw], axis=0), preferred_element_type=F32))
        yield
        for u in units:
            kend, bend, pend = kb_src[u]
            kb_t = jnp.transpose(jnp.concatenate(
                [stack_heads(kend), stack_heads(bend), jnp.broadcast_to(pend, (PAIR_DIM, PAIR_DIM))],
                axis=1))
            for h in range(2):
                k = 2 * u + h
                lhs_b[slot[u], h * C:(h + 1) * C, :] = TQ[k][:, 0:PAIR_DIM].astype(BF16)
                lhs_b[slot[u], (2 + h) * C:(3 + h) * C, :] = (
                    hrows(qr_sm[u], h) - AT[k][:, 0:PAIR_DIM]).astype(BF16)
                add_b[slot[u], h * C:(h + 1) * C, :] = TQ[k][:, PAIR_DIM:2 * PAIR_DIM]
                add_b[slot[u], (2 + h) * C:(3 + h) * C, :] = AV[k][C:2 * C] - AT[k][:, PAIR_DIM:2 * PAIR_DIM]
            vk_b[slot[u]] = _bdot(kb_t[0:PAIR_DIM], v_sm[u])
            bend_b[slot[u]] = kb_t[PAIR_DIM:2 * PAIR_DIM].astype(BF16)
            pend_b[slot[u]] = kb_t[2 * PAIR_DIM:3 * PAIR_DIM]

    def advance(cg):
        for ci in range(WKV_PREP_CHUNKS):
            c = cg * WKV_PREP_CHUNKS + ci
            UO = [jnp.dot(lhs_b[c * HEAD_PAIRS + j], S[j].astype(BF16), preferred_element_type=F32)
                  + add_b[c * HEAD_PAIRS + j] for j in pairs]
            yield
            for j in pairs:
                u = c * HEAD_PAIRS + j
                S[j] = pend_b[u] * S[j] + vk_b[u] - jnp.dot(bend_b[u], UO[j][0:2 * C].astype(BF16),
                                                            preferred_element_type=F32)
                o_s[c * C:(c + 1) * C, j * PAIR_DIM:(j + 1) * PAIR_DIM] = UO[j][2 * C:3 * C] + UO[j][3 * C:4 * C]
            yield

    n_groups = tt // (C * WKV_PREP_CHUNKS)
    _run_interleaved(prepare(0))
    for cg in range(1, n_groups):
        _run_interleaved(prepare(cg), advance(cg - 1))
    _run_interleaved(advance(n_groups - 1))

    yb = _rwkv_post(o_s[...], bonus, g, gng_ref[...], gnb_ref[...], e_seg)
    y_ref[0, :, POOL_DIM:D_MODEL] = yb.astype(BF16)

    @pl.when(i == nt - 1)
    def _fin():
        opool_ref[0] = hpool[1:16, :]
        oshift_ref[0] = hshift[...]
        for j in range(HEAD_PAIRS):
            S[j] = jnp.transpose(S[j])
        for j in range(HEAD_PAIRS):
            owkv_ref[0, 2 * j] = S[j, 0:HEAD_DIM, 0:HEAD_DIM]
            owkv_ref[0, 2 * j + 1] = S[j, HEAD_DIM:PAIR_DIM, HEAD_DIM:PAIR_DIM]


def _even_prompt(x, g_in, w_in, st_pool, st_shift, st_wkv, prm, start):
    B, T, _ = x.shape
    tt = MIXER_TILE
    nt = T // tt
    bt = lambda b, i: (b, i, 0)
    bs3 = lambda b, i: (b, 0, 0)
    bs4 = lambda b, i: (b, 0, 0, 0)
    c2 = lambda b, i: (0, 0)
    vec = lambda n: pl.BlockSpec((1, n), c2)
    scr = lambda: pltpu.VMEM((tt, RWKV_DIM), F32)
    n_units = (tt // WKV_CHUNK) * HEAD_PAIRS
    return pl.pallas_call(
        functools.partial(_even_prompt_kernel, tt=tt, start=start),
        grid=(B, nt),
        in_specs=[pl.BlockSpec((1, tt, D_MODEL), lambda b, i: (0, 0, 0)),
                  pl.BlockSpec((1, tt, D_MODEL), functools.partial(_next_tile, nt=nt, n_tiles=B * nt)),
                  vec(D_MODEL),
                  pl.BlockSpec((D_MODEL, EVEN_PROJ), c2, pipeline_mode=pl.Buffered(1)),
                  pl.BlockSpec((1, POOL_BUF, POOL_DIM), bs3),
                  pl.BlockSpec((1, 1, RWKV_PROJ), bs3),
                  pl.BlockSpec((1, RWKV_HEADS, HEAD_DIM, HEAD_DIM), bs4),
                  vec(RWKV_PROJ), vec(RWKV_DIM), pl.BlockSpec((LORA_PAD, RWKV_DIM), c2), vec(RWKV_DIM),
                  pl.BlockSpec((LORA_PAD, RWKV_DIM), c2), pl.BlockSpec((LORA_PAD, RWKV_DIM), c2),
                  vec(RWKV_DIM), vec(RWKV_DIM), vec(RWKV_DIM), vec(RWKV_DIM), vec(RWKV_DIM),
                  pl.BlockSpec((SEG_TILE, SEG_TILE), c2), pl.BlockSpec((POOL_DIM, POOL_DIM), c2),
                  vec(POOL_DIM), pl.BlockSpec((WKV_CHUNK, WKV_CHUNK), c2)],
        out_specs=[pl.BlockSpec((1, tt, D_MODEL), bt),
                   pl.BlockSpec((1, POOL_BUF, POOL_DIM), bs3),
                   pl.BlockSpec((1, 1, RWKV_PROJ), bs3),
                   pl.BlockSpec((1, RWKV_HEADS, HEAD_DIM, HEAD_DIM), bs4)],
        out_shape=[jax.ShapeDtypeStruct((B, T, D_MODEL), BF16),
                   jax.ShapeDtypeStruct((B, POOL_BUF, POOL_DIM), F32),
                   jax.ShapeDtypeStruct((B, 1, RWKV_PROJ), F32),
                   jax.ShapeDtypeStruct((B, RWKV_HEADS, HEAD_DIM, HEAD_DIM), F32)],
        scratch_shapes=[pltpu.VMEM((tt, EVEN_PROJ), F32),
                        pltpu.VMEM((16, POOL_DIM), F32), pltpu.VMEM((1, RWKV_PROJ), F32),
                        pltpu.VMEM((HEAD_PAIRS, PAIR_DIM, PAIR_DIM), F32),
                        scr(), scr(), scr(), scr(), scr(), scr(), scr(),
                        pltpu.VMEM((n_units, 2 * PAIR_DIM, PAIR_DIM), BF16),
                        pltpu.VMEM((n_units, 2 * PAIR_DIM, PAIR_DIM), F32),
                        pltpu.VMEM((n_units, PAIR_DIM, PAIR_DIM), F32),
                        pltpu.VMEM((n_units, PAIR_DIM, PAIR_DIM), BF16),
                        pltpu.VMEM((n_units, PAIR_DIM, PAIR_DIM), F32)],
        compiler_params=pltpu.CompilerParams(dimension_semantics=("arbitrary", "arbitrary"),
                                             vmem_limit_bytes=VMEM_LIMIT_BIG),
        name="even_prompt",
    )(x, x, g_in, w_in, st_pool, st_shift, st_wkv, *prm)


def _odd_prompt_kernel(x0_ref, xn_ref, gin_ref, win_ref, stc_ref, stl_ref,
                       lng_ref, lnb_ref, ws_ref, bias_ref, cw_ref, cb_ref, wx_ref, bx_ref, wa_ref,
                       ba_ref, lam_ref, y_ref, oconv_ref, olru_ref, q_s, hconv, hl, mix_s, *, tt):
    i = pl.program_id(1)
    nt = pl.num_programs(1)

    @pl.when(jnp.logical_and(i == 0, pl.program_id(0) == 0))
    def _first_projection():
        h0 = _rmsnorm(x0_ref[0], gin_ref[...]).astype(BF16)
        q_s[...] = jnp.dot(h0, win_ref[...], preferred_element_type=F32)

    @pl.when(i == 0)
    def _init():
        hconv[0:5, :] = jnp.zeros((5, LRU_DIM), F32)
        hconv[5:8, :] = stc_ref[0]
        hl[...] = stl_ref[0]

    q = q_s[...]

    def project_next():
        hb = _rmsnorm(xn_ref[0], gin_ref[...]).astype(BF16)
        yield
        for c0 in range(0, ODD_PROJ, PROJ_PIECE):
            q_s[:, c0:c0 + PROJ_PIECE] = jnp.dot(hb, win_ref[:, c0:c0 + PROJ_PIECE],
                                                 preferred_element_type=F32)
            yield

    def mixers():
        u, vn = _gmlp_pre(q[:, 0:2 * GMLP_DIM], lng_ref[...], lnb_ref[...])
        yield
        rr = lax.broadcasted_iota(jnp.int32, (CHUNK, CHUNK), 0)
        cc = lax.broadcasted_iota(jnp.int32, (CHUNK, CHUNK), 1)
        causal = cc <= rr
        for h in range(GMLP_HEADS):
            wm = jnp.where(causal, ws_ref[h], 0.0).astype(BF16)
            ls = slice(h * CHUNK, (h + 1) * CHUNK)
            for c in range(tt // CHUNK):
                rs = slice(c * CHUNK, (c + 1) * CHUNK)
                mix_s[rs, ls] = (jnp.dot(wm, vn[rs, ls].astype(BF16), preferred_element_type=F32)
                                 + bias_ref[:, ls])
        y_ref[0, :, 0:GMLP_DIM] = (u * mix_s[...]).astype(BF16)
        yield

        gate = _gelu(q[:, 2 * GMLP_DIM:2 * GMLP_DIM + LRU_DIM])
        yield
        xr = q[:, 2 * GMLP_DIM + LRU_DIM:ODD_PROJ]
        hconv[8:8 + tt, :] = xr
        xc = xr * cw_ref[3:4, :] + cb_ref[...]
        for j in range(1, CONV_WIDTH):
            xc = xc + hconv[8 - j:8 - j + tt, :] * cw_ref[3 - j:4 - j, :]
        hconv[0:8, :] = hconv[tt:tt + 8, :]
        yield
        a, b = _lru_gates(xc, wx_ref[...], bx_ref[...], wa_ref[...], ba_ref[...], lam_ref[...])
        yield
        n_groups = tt // SCAN_GROUP
        a = a.reshape(n_groups, SCAN_GROUP, LRU_DIM)
        b = b.reshape(n_groups, SCAN_GROUP, LRU_DIM)
        in_group = lax.broadcasted_iota(jnp.int32, (1, SCAN_GROUP, 1), 1)
        dist = 1
        while dist < SCAN_GROUP:
            keep = in_group >= dist
            a_sh = jnp.where(keep, pltpu.roll(a, dist, 1), 1.0)
            b_sh = jnp.where(keep, pltpu.roll(b, dist, 1), 0.0)
            b = a * b_sh + b
            a = a * a_sh
            dist *= 2
            yield
        carry = hl[...]
        groups = []
        for gi in range(n_groups):
            hg = a[gi] * carry + b[gi]
            groups.append(hg)
            carry = hg[SCAN_GROUP - 1:SCAN_GROUP, :]
        h = jnp.concatenate(groups, axis=0)
        hl[...] = carry
        y_ref[0, :, GMLP_DIM:D_MODEL] = (h * gate).astype(BF16)
        yield

    _run_interleaved(mixers(), project_next())

    @pl.when(i == nt - 1)
    def _fin():
        oconv_ref[0] = hconv[5:8, :]
        olru_ref[0] = hl[...]


def _odd_prompt(x, g_in, w_in, st_conv, st_lru, prm):
    B, T, _ = x.shape
    tt = MIXER_TILE
    nt = T // tt
    bt = lambda b, i: (b, i, 0)
    bs3 = lambda b, i: (b, 0, 0)
    c2 = lambda b, i: (0, 0)
    c3 = lambda b, i: (0, 0, 0)
    vec = lambda n: pl.BlockSpec((1, n), c2)
    return pl.pallas_call(
        functools.partial(_odd_prompt_kernel, tt=tt),
        grid=(B, nt),
        in_specs=[pl.BlockSpec((1, tt, D_MODEL), lambda b, i: (0, 0, 0)),
                  pl.BlockSpec((1, tt, D_MODEL), functools.partial(_next_tile, nt=nt, n_tiles=B * nt)),
                  vec(D_MODEL),
                  pl.BlockSpec((D_MODEL, ODD_PROJ), c2, pipeline_mode=pl.Buffered(1)),
                  pl.BlockSpec((1, CONV_WIDTH - 1, LRU_DIM), bs3),
                  pl.BlockSpec((1, 1, LRU_DIM), bs3),
                  vec(GMLP_DIM), vec(GMLP_DIM),
                  pl.BlockSpec((GMLP_HEADS, CHUNK, CHUNK), c3),
                  pl.BlockSpec((CHUNK, GMLP_DIM), c2),
                  pl.BlockSpec((CONV_WIDTH, LRU_DIM), c2), vec(LRU_DIM),
                  pl.BlockSpec((LRU_DIM, LRU_DIM), c2), vec(LRU_DIM),
                  pl.BlockSpec((LRU_DIM, LRU_DIM), c2), vec(LRU_DIM), vec(LRU_DIM)],
        out_specs=[pl.BlockSpec((1, tt, D_MODEL), bt),
                   pl.BlockSpec((1, CONV_WIDTH - 1, LRU_DIM), bs3),
                   pl.BlockSpec((1, 1, LRU_DIM), bs3)],
        out_shape=[jax.ShapeDtypeStruct((B, T, D_MODEL), BF16),
                   jax.ShapeDtypeStruct((B, CONV_WIDTH - 1, LRU_DIM), F32),
                   jax.ShapeDtypeStruct((B, 1, LRU_DIM), F32)],
        scratch_shapes=[pltpu.VMEM((tt, ODD_PROJ), F32),
                        pltpu.VMEM((8 + tt, LRU_DIM), F32), pltpu.VMEM((1, LRU_DIM), F32),
                        pltpu.VMEM((tt, GMLP_DIM), F32)],
        compiler_params=pltpu.CompilerParams(dimension_semantics=("arbitrary", "arbitrary"),
                                             vmem_limit_bytes=VMEM_LIMIT),
        name="odd_prompt",
    )(x, x, g_in, w_in, st_conv, st_lru, *prm)


def _even_sample_pre_kernel(p_ref, stp_ref, sts_ref,
                            mu_ref, w0_ref, wdec_ref, a0_ref, wa_ref, gw2_ref, kk_ref, ka_ref, rk_ref,
                            eseg_ref, poolw_ref, pools_ref,
                            r_ref, w_ref, kkn_ref, kka_ref, kp_ref, v_ref, g_ref, bonus_ref, ya_ref,
                            opool_ref, oshift_ref, *, T, start):
    prm = (mu_ref[...], w0_ref[...], wdec_ref[...], a0_ref[...], wa_ref[...], gw2_ref[...],
           kk_ref[...], ka_ref[...], rk_ref[...], eseg_ref[...])
    full = [stp_ref[s] for s in range(POOL_BUF)] + [p_ref[t][:, 0:POOL_DIM] for t in range(T)]
    wl = _pool_window_lanes()
    for t in range(T):
        P = p_ref[t][:, POOL_DIM:EVEN_PROJ]
        Pprev = sts_ref[...] if t == 0 else p_ref[t - 1][:, POOL_DIM:EVEN_PROJ]
        r, kp, v, ld, kk, a, g, bonus = _rwkv_pointwise(P, Pprev, prm)
        r_ref[t] = jnp.transpose(r)
        w_ref[t] = jnp.transpose(jnp.exp(ld))
        kkn_ref[t] = jnp.transpose(kk)
        kka_ref[t] = jnp.transpose(kk * a)
        kp_ref[t] = jnp.transpose(kp)
        v_ref[t] = jnp.transpose(v)
        g_ref[t] = g
        bonus_ref[t] = bonus
        e = POOL_BUF + t
        s2 = full[e] + full[e - 1]
        s4 = s2 + full[e - 2] + full[e - 3]
        s8 = s4 + full[e - 4] + full[e - 5] + full[e - 6] + full[e - 7]
        s16 = s8
        for s in range(8, 16):
            s16 = s16 + full[e - s]
        sel = _pool_lane_select(s2, s4, s8, s16)
        cnt = jnp.minimum(wl, start + t + 1).astype(F32)
        d = sel / cnt - full[e]
        ya_ref[t] = _bdot(d, poolw_ref[...]) * pools_ref[...]
    for s in range(POOL_BUF):
        opool_ref[s] = full[T + s]
    oshift_ref[...] = p_ref[T - 1][:, POOL_DIM:EVEN_PROJ]


def _even_sample_pre(p, st_pool, st_shift, prm, start):
    T, B, _ = p.shape
    cm = jax.ShapeDtypeStruct((T, RWKV_DIM, B), F32)
    bm = jax.ShapeDtypeStruct((T, B, RWKV_DIM), F32)
    return pl.pallas_call(
        functools.partial(_even_sample_pre_kernel, T=T, start=start),
        out_shape=[cm] * 6 + [bm] * 2 + [jax.ShapeDtypeStruct((T, B, POOL_DIM), F32),
                                   jax.ShapeDtypeStruct((POOL_BUF, B, POOL_DIM), F32),
                                   jax.ShapeDtypeStruct((B, RWKV_PROJ), F32)],
        compiler_params=pltpu.CompilerParams(vmem_limit_bytes=VMEM_LIMIT),
        name="even_sample_pre",
    )(p, st_pool, st_shift, *prm)


def _wkv_sample_kernel(r_ref, w_ref, kk_ref, kka_ref, kp_ref, v_ref, s_ref, o_ref, so_ref, *, T):
    group = range(WKV_SAMPLE_GROUP)

    def body(ib, carry):
        v0 = pl.multiple_of(ib * WKV_SAMPLE_GROUP, WKV_SAMPLE_GROUP)
        blk = pl.ds(v0, WKV_SAMPLE_GROUP)
        S = [s_ref[0, v0 + u] for u in group]
        for t in range(T):
            kk, w, kka, kp, r = kk_ref[t], w_ref[t], kka_ref[t], kp_ref[t], r_ref[t]
            vv = v_ref[t, blk, :]
            sk = [jnp.sum(S[u] * kk, axis=0, keepdims=True) for u in group]
            S = [S[u] * w - sk[u] * kka + vv[u:u + 1, :] * kp for u in group]
            o_ref[t, blk, :] = jnp.concatenate(
                [jnp.sum(S[u] * r, axis=0, keepdims=True) for u in group], axis=0)
        for u in group:
            so_ref[0, v0 + u] = S[u]
        return carry

    lax.fori_loop(0, HEAD_DIM // WKV_SAMPLE_GROUP, body, 0)


def _wkv_sample(r, w, kk, kka, kp, v, s):
    T, _, B = r.shape
    row_spec = pl.BlockSpec((T, HEAD_DIM, B), lambda h: (0, h, 0))
    st_spec = pl.BlockSpec((1, HEAD_DIM, HEAD_DIM, B), lambda h: (h, 0, 0, 0))
    return pl.pallas_call(
        functools.partial(_wkv_sample_kernel, T=T),
        grid=(RWKV_HEADS,),
        in_specs=[row_spec] * 6 + [st_spec],
        out_specs=[row_spec, st_spec],
        out_shape=[jax.ShapeDtypeStruct((T, RWKV_DIM, B), F32),
                   jax.ShapeDtypeStruct((RWKV_HEADS, HEAD_DIM, HEAD_DIM, B), F32)],
        compiler_params=pltpu.CompilerParams(dimension_semantics=("arbitrary",),
                                             vmem_limit_bytes=VMEM_LIMIT),
        name="wkv_sample",
    )(r, w, kk, kka, kp, v, s)


def _even_sample_post_kernel(o_ref, bonus_ref, g_ref, ya_ref, gng_ref, gnb_ref, eseg_ref, y_ref, *, T):
    for t in range(T):
        o = jnp.transpose(o_ref[t])
        yb = _rwkv_post(o, bonus_ref[t], g_ref[t], gng_ref[...], gnb_ref[...], eseg_ref[...])
        y_ref[t, :, 0:POOL_DIM] = ya_ref[t].astype(BF16)
        y_ref[t, :, POOL_DIM:D_MODEL] = yb.astype(BF16)


def _even_sample_post(o, bonus, g, ya, gn_g, gn_b, e_seg):
    T, _, B = o.shape
    return pl.pallas_call(
        functools.partial(_even_sample_post_kernel, T=T),
        out_shape=jax.ShapeDtypeStruct((T, B, D_MODEL), BF16),
        compiler_params=pltpu.CompilerParams(vmem_limit_bytes=VMEM_LIMIT),
        name="even_sample_post",
    )(o, bonus, g, ya, gn_g, gn_b, e_seg)


def _odd_sample_kernel(q_ref, stc_ref, stl_ref,
                       lng_ref, lnb_ref, wsm_ref, bsm_ref, cw_ref, cb_ref, wx_ref, bx_ref, wa_ref,
                       ba_ref, lam_ref, y_ref, v_ref, oconv_ref, olru_ref, *, T):
    vns = []
    us = []
    for t in range(T):
        u, vn = _gmlp_pre(q_ref[t][:, 0:2 * GMLP_DIM], lng_ref[...], lnb_ref[...])
        us.append(u)
        vns.append(vn)
        v_ref[t] = vn
    full = [stc_ref[s] for s in range(CONV_WIDTH - 1)] + \
           [q_ref[t][:, 2 * GMLP_DIM + LRU_DIM:ODD_PROJ] for t in range(T)]
    h = stl_ref[...]
    for t in range(T):
        mix = bsm_ref[t:t + 1, :]
        for j in range(t + 1):
            mix = mix + wsm_ref[t * T + j:t * T + j + 1, :] * vns[j]
        y_ref[t, :, 0:GMLP_DIM] = (us[t] * mix).astype(BF16)
        xc = full[t + CONV_WIDTH - 1] * cw_ref[CONV_WIDTH - 1:CONV_WIDTH, :] + cb_ref[...]
        for j in range(CONV_WIDTH - 1):
            xc = xc + full[t + j] * cw_ref[j:j + 1, :]
        a, b = _lru_gates(xc, wx_ref[...], bx_ref[...], wa_ref[...], ba_ref[...], lam_ref[...])
        h = a * h + b
        gate_in = q_ref[t][:, 2 * GMLP_DIM:2 * GMLP_DIM + LRU_DIM]
        y_ref[t, :, GMLP_DIM:D_MODEL] = (h * _gelu(gate_in)).astype(BF16)
    for s in range(CONV_WIDTH - 1):
        oconv_ref[s] = full[T + s]
    olru_ref[...] = h


def _odd_sample(q, st_conv, st_lru, prm):
    T, B, _ = q.shape
    return pl.pallas_call(
        functools.partial(_odd_sample_kernel, T=T),
        out_shape=[jax.ShapeDtypeStruct((T, B, D_MODEL), BF16),
                   jax.ShapeDtypeStruct((T, B, GMLP_DIM), F32),
                   jax.ShapeDtypeStruct((CONV_WIDTH - 1, B, LRU_DIM), F32),
                   jax.ShapeDtypeStruct((B, LRU_DIM), F32)],
        compiler_params=pltpu.CompilerParams(vmem_limit_bytes=VMEM_LIMIT),
        name="odd_sample",
    )(q, st_conv, st_lru, *prm)


def _block_diag(w):
    n, c, d = w.shape
    eye = jnp.eye(n, dtype=w.dtype)
    return (eye[:, None, :, None] * w[:, :, None, :]).reshape(n * c, n * d)


def _row(x):
    return x.reshape(1, -1)


def kernel(x_prompt, x_sample, state_pool, state_shift, state_wkv, state_conv, state_lru, ev_norm_g, ev_w_in, pool_w, pool_scale, rwkv_mu, rwkv_w0, rwkv_w_w2, rwkv_a0, rwkv_a_w2, rwkv_g_w2, rwkv_k_k, rwkv_k_a, rwkv_r_k, rwkv_gn_g, rwkv_gn_b, ev_w_out, od_norm_g, od_w_in, gmlp_ln_g, gmlp_ln_b, gmlp_ws, gmlp_bs, lru_conv_w, lru_conv_b, lru_wx, lru_bx, lru_wa, lru_ba, lru_lam, od_w_out, ff_norm_g, ff_w1, ff_w2, final_norm_g):
    B, T, _ = x_prompt.shape
    DB, DT, _ = x_sample.shape

    seg_ids = jnp.arange(SEG_TILE) // HEAD_DIM
    e_seg = (seg_ids[:, None] == seg_ids[None, :]).astype(BF16)
    tri = (jnp.arange(WKV_CHUNK)[None, :] <= jnp.arange(WKV_CHUNK)[:, None]).astype(BF16)
    zlora = jnp.zeros((LORA_PAD // 2, RWKV_DIM), F32)

    ev_common = (_row(rwkv_mu[0]), _row(rwkv_w0[0]),
                 jnp.concatenate([rwkv_w_w2[0], zlora], 0).astype(BF16), _row(rwkv_a0[0]),
                 jnp.concatenate([zlora, rwkv_a_w2[0]], 0).astype(BF16), rwkv_g_w2[0].astype(BF16),
                 _row(rwkv_k_k[0]), _row(rwkv_k_a[0]), _row(rwkv_r_k[0]))
    gn_g, gn_b = _row(rwkv_gn_g[0]), _row(rwkv_gn_b[0])
    pool_bd = _block_diag(pool_w[0]).astype(BF16)
    pool_sc = _row(pool_scale[0])
    w_in0 = ev_w_in[0].astype(BF16)
    g_in0 = _row(ev_norm_g[0])

    xp = x_prompt.reshape(B * T, D_MODEL)
    xs = jnp.transpose(x_sample, (1, 0, 2)).reshape(DT * DB, D_MODEL)

    ps = _inproj(xs, g_in0, w_in0).reshape(DT, DB, EVEN_PROJ)

    yp, p_pool, p_shift, p_wkv = _even_prompt(
        x_prompt, g_in0, w_in0,
        jnp.zeros((B, POOL_BUF, POOL_DIM), F32), jnp.zeros((B, 1, RWKV_PROJ), F32),
        jnp.zeros((B, RWKV_HEADS, HEAD_DIM, HEAD_DIM), F32),
        ev_common + (gn_g, gn_b, e_seg, pool_bd, pool_sc, tri), 0)

    pre = _even_sample_pre(ps, jnp.transpose(state_pool[0], (1, 0, 2)), state_shift[0],
                           ev_common + (e_seg, pool_bd, pool_sc), PAST_LEN)
    r_s, w_s, kk_s, kka_s, kp_s, v_s, g_s, bonus_s, ya_s, s_pool_tm, s_shift = pre
    o_s, s_wkv_bl = _wkv_sample(r_s, w_s, kk_s, kka_s, kp_s, v_s,
                                jnp.transpose(state_wkv[0], (1, 2, 3, 0)))
    ys = _even_sample_post(o_s, bonus_s, g_s, ya_s, gn_g, gn_b, e_seg)

    w_out0 = ev_w_out[0].astype(BF16)
    ffg = lambda l: _row(ff_norm_g[l])
    gfin = _row(final_norm_g)
    ff_w1_b, ff_w2_b = ff_w1.astype(BF16), ff_w2.astype(BF16)
    xp = _ffn(xp, yp.reshape(B * T, D_MODEL), w_out0, ffg(0), ff_w1_b, ff_w2_b, gfin, 0, False)
    xs = _ffn(xs, ys.reshape(DT * DB, D_MODEL), w_out0, ffg(0), ff_w1_b, ff_w2_b, gfin, 0, False)

    w_in1 = od_w_in[0].astype(BF16)
    g_in1 = _row(od_norm_g[0])
    qs = _inproj(xs, g_in1, w_in1).reshape(DT, DB, ODD_PROJ)

    lru_common = (lru_conv_w[0], _row(lru_conv_b[0]), _block_diag(lru_wx[0]).astype(BF16), _row(lru_bx[0]),
                  _block_diag(lru_wa[0]).astype(BF16), _row(lru_ba[0]), _row(lru_lam[0]))
    ln = (_row(gmlp_ln_g[0]), _row(gmlp_ln_b[0]))
    bias_full = jnp.repeat(jnp.transpose(gmlp_bs[0]), CHUNK, axis=1)
    yp, p_conv, p_lru = _odd_prompt(
        xp.reshape(B, T, D_MODEL), g_in1, w_in1,
        jnp.zeros((B, CONV_WIDTH - 1, LRU_DIM), F32), jnp.zeros((B, 1, LRU_DIM), F32),
        ln + (gmlp_ws[0], bias_full) + lru_common)

    ws_small = jnp.repeat(jnp.transpose(gmlp_ws[0][:, :DT, :DT], (1, 2, 0)).reshape(DT * DT, GMLP_HEADS),
                          CHUNK, axis=1)
    ys, s_v, s_conv_tm, s_lru = _odd_sample(
        qs, jnp.transpose(state_conv[0], (1, 0, 2)), state_lru[0],
        ln + (ws_small, bias_full[:DT]) + lru_common)

    w_out1 = od_w_out[0].astype(BF16)
    xp = _ffn(xp, yp.reshape(B * T, D_MODEL), w_out1, ffg(1), ff_w1_b, ff_w2_b, gfin, 1, True)
    xs = _ffn(xs, ys.reshape(DT * DB, D_MODEL), w_out1, ffg(1), ff_w1_b, ff_w2_b, gfin, 1, True)

    tm2bm = lambda t: jnp.transpose(t, (1, 0, 2))
    y_prompt = xp.reshape(B, T, D_MODEL)
    y_sample = tm2bm(xs.reshape(DT, DB, D_MODEL))
    return (y_prompt, y_sample,
            p_pool[None], p_shift.reshape(1, B, RWKV_PROJ), p_wkv[None],
            p_conv[None], p_lru.reshape(1, B, LRU_DIM),
            tm2bm(s_pool_tm)[None], s_shift[None],
            jnp.transpose(s_wkv_bl, (3, 0, 1, 2))[None],
            tm2bm(s_conv_tm)[None], s_lru[None], tm2bm(s_v)[None])
```

```python
import functools

import jax
import jax.numpy as jnp
from jax import lax
from jax.experimental import pallas as pl
from jax.experimental.pallas import tpu as pltpu

F32 = jnp.float32
BF16 = jnp.bfloat16

D_MODEL = 1024
NORM_EPS = 1e-6
D_FF = 4 * D_MODEL

POOL_WINDOWS = (2, 4, 8, 16)
POOL_GROUP_DIM = 64
POOL_DIM = 256
POOL_BUF = 15

HEAD_DIM = 64
RWKV_DIM = 768
RWKV_HEADS = 12
HEAD_PAIRS = RWKV_HEADS // 2
PAIR_DIM = 2 * HEAD_DIM
RWKV_PROJ = 2560
RWKV_GN_EPS = 64e-5
EXP_NEG_HALF = 0.6065306597126334
EVEN_PROJ = POOL_DIM + RWKV_PROJ
LORA_OFF = 3 * RWKV_DIM
LORA_PAD = 128
GATE_OFF = LORA_OFF + LORA_PAD

CHUNK = 128
GMLP_DIM = 512
GMLP_HEADS = 4
LN_EPS = 1e-5
GELU_C = 0.7978845608028654
LRU_DIM = 512
CONV_WIDTH = 4
LRU_C = 8.0
ODD_PROJ = 2048

WKV_CHUNK = 64
SEG_TILE = 256
SCAN_GROUP = 8
PROJ_PIECE = 256
WKV_PREP_CHUNKS = 2
WKV_SAMPLE_GROUP = 8

PAST_LEN = 16384
MIXER_TILE = 512
DENSE_TILE = 1024

V7X_VMEM_BYTES = 64 * 1024 * 1024
VMEM_LIMIT = V7X_VMEM_BYTES * 3 // 4
VMEM_LIMIT_BIG = V7X_VMEM_BYTES * 7 // 8


def _bdot(a, b):
    return jnp.dot(a.astype(BF16), b.astype(BF16), preferred_element_type=F32)


def _bdot_nt(a, b):
    return lax.dot_general(a.astype(BF16), b.astype(BF16), (((1,), (1,)), ((), ())),
                           preferred_element_type=F32)


def _split3(x):
    hi = x.astype(BF16)
    r1 = x - hi.astype(F32)
    mid = r1.astype(BF16)
    lo = (r1 - mid.astype(F32)).astype(BF16)
    return hi, mid, lo


def _exact_dot_rhs01(x, e):
    hi = x.astype(BF16)
    lo = (x - hi.astype(F32)).astype(BF16)
    d = lambda t: jnp.dot(t, e, preferred_element_type=F32)
    return d(hi) + d(lo)


def _exact_dot_lhs01(e, x):
    hi, mid, lo = _split3(x)
    d = lambda t: jnp.dot(e, t, preferred_element_type=F32)
    return d(hi) + d(mid) + d(lo)


def _segsum(x, e_seg):
    parts = [_exact_dot_rhs01(x[:, g * SEG_TILE:(g + 1) * SEG_TILE], e_seg)
             for g in range(RWKV_DIM // SEG_TILE)]
    return jnp.concatenate(parts, axis=1)


def _softplus(z):
    return jnp.maximum(z, 0.0) + jnp.log(1.0 + jnp.exp(-jnp.abs(z)))


def _sigmoid(z):
    return 0.5 * jnp.tanh(0.5 * z) + 0.5


def _gelu(z):
    hz = 0.5 * z
    return hz + hz * jnp.tanh(z * (GELU_C + (GELU_C * 0.044715) * (z * z)))


def _rmsnorm(x, g):
    ms = jnp.mean(x * x, axis=-1, keepdims=True)
    return x * lax.rsqrt(ms + NORM_EPS) * g


def _next_tile(b, i, *, nt, n_tiles):
    n = jnp.minimum(b * nt + i + 1, n_tiles - 1)
    return (n // nt, n % nt, 0)


def _run_interleaved(*stages):
    live = list(stages)
    while live:
        for gen in list(live):
            if next(gen, StopIteration) is StopIteration:
                live.remove(gen)


_POINTWISE_KEYS = ("r", "kp", "v", "ld", "kk", "a", "g", "bonus")


def _rwkv_pointwise_stages(P, Pprev, prm, out):
    (mu, w0, wdec, a0, wa, gw2, k_k, k_a, r_k, e_seg) = prm
    xs = P + (Pprev - P) * mu
    r = xs[:, 0:RWKV_DIM]
    k = xs[:, RWKV_DIM:2 * RWKV_DIM]
    v = xs[:, 2 * RWKV_DIM:3 * RWKV_DIM]
    c_wa = xs[:, LORA_OFF:GATE_OFF]
    cg = xs[:, GATE_OFF:RWKV_PROJ]
    yield
    ld = -EXP_NEG_HALF * _sigmoid(w0 + _bdot(jnp.tanh(c_wa), wdec))
    yield
    a = _sigmoid(a0 + _bdot(c_wa, wa))
    yield
    g = _bdot(_sigmoid(cg), gw2)
    yield
    kk = k * k_k
    kk = kk * lax.rsqrt(jnp.maximum(_segsum(kk * kk, e_seg), 1e-24))
    yield
    kp = k * (1.0 + (a - 1.0) * k_a)
    yield
    bonus = _segsum(r * kp * r_k, e_seg) * v
    out.update(r=r, kp=kp, v=v, ld=ld, kk=kk, a=a, g=g, bonus=bonus)
    yield


def _rwkv_pointwise(P, Pprev, prm):
    out = {}
    for _ in _rwkv_pointwise_stages(P, Pprev, prm, out):
        pass
    return tuple(out[key] for key in _POINTWISE_KEYS)


def _rwkv_post(o, bonus, g, gn_g, gn_b, e_seg):
    m = _segsum(o, e_seg) * (1.0 / HEAD_DIM)
    d = o - m
    var = _segsum(d * d, e_seg) * (1.0 / HEAD_DIM)
    on = d * lax.rsqrt(var + RWKV_GN_EPS) * gn_g + gn_b
    return (on + bonus) * g


def _pool_by_group(per_group):
    lane = lax.broadcasted_iota(jnp.int32, (1, POOL_DIM), 1)
    out = per_group[-1]
    for i in range(len(per_group) - 2, -1, -1):
        out = jnp.where(lane < (i + 1) * POOL_GROUP_DIM, per_group[i], out)
    return out


def _pool_lane_select(s2, s4, s8, s16):
    return _pool_by_group((s2, s4, s8, s16))


def _pool_window_lanes():
    return _pool_by_group(POOL_WINDOWS)


def _lru_gates(xc, wx, bx, wa, ba, lam):
    gx = _sigmoid(_bdot(xc, wx) + bx)
    ga = _sigmoid(_bdot(xc, wa) + ba)
    log_a = -LRU_C * ga * _softplus(-lam)
    a = jnp.exp(log_a)
    b = jnp.sqrt(-jnp.tanh(log_a) * (a * a + 1.0)) * gx * xc
    return a, b


def _gmlp_pre(zq, ln_g, ln_b):
    z = _gelu(zq)
    u = z[:, :GMLP_DIM]
    v = z[:, GMLP_DIM:]
    m = jnp.mean(v, axis=-1, keepdims=True)
    d = v - m
    var = jnp.mean(d * d, axis=-1, keepdims=True)
    return u, d * lax.rsqrt(var + LN_EPS) * ln_g + ln_b


def _inproj_kernel(x_ref, g_ref, w_ref, o_ref):
    h = _rmsnorm(x_ref[...], g_ref[...])
    o_ref[...] = jnp.dot(h.astype(BF16), w_ref[...], preferred_element_type=F32)


def _inproj(x, g, w):
    m, n = x.shape[0], w.shape[1]
    tm = min(DENSE_TILE, m)
    return pl.pallas_call(
        _inproj_kernel,
        grid=(m // tm,),
        in_specs=[pl.BlockSpec((tm, D_MODEL), lambda i: (i, 0)),
                  pl.BlockSpec((1, D_MODEL), lambda i: (0, 0)),
                  pl.BlockSpec((D_MODEL, n), lambda i: (0, 0), pipeline_mode=pl.Buffered(1))],
        out_specs=pl.BlockSpec((tm, n), lambda i: (i, 0)),
        out_shape=jax.ShapeDtypeStruct((m, n), F32),
        compiler_params=pltpu.CompilerParams(dimension_semantics=("arbitrary",),
                                             vmem_limit_bytes=VMEM_LIMIT),
        name="inproj",
    )(x, g, w)


def _ffn_kernel(x_ref, y_ref, wo_ref, g_ref, w1_ref, w2_ref, gf_ref, o_ref, *, final):
    x1 = x_ref[...] + jnp.dot(y_ref[...], wo_ref[...], preferred_element_type=F32)
    hf = _rmsnorm(x1, g_ref[...]).astype(BF16)
    acc = x1
    fc = 1024
    for c in range(D_FF // fc):
        h = jnp.dot(hf, w1_ref[:, c * fc:(c + 1) * fc], preferred_element_type=F32)
        h = jnp.square(jnp.maximum(h, 0.0)).astype(BF16)
        acc = acc + jnp.dot(h, w2_ref[c * fc:(c + 1) * fc, :], preferred_element_type=F32)
    if final:
        acc = _rmsnorm(acc, gf_ref[...])
    o_ref[...] = acc


def _ffn(x, y, wo, g, w1, w2, gf, layer, final):
    m = x.shape[0]
    tm = min(DENSE_TILE, m)
    const = lambda i: (0, 0)
    pick = lambda i: (layer, 0, 0)
    return pl.pallas_call(
        functools.partial(_ffn_kernel, final=final),
        grid=(m // tm,),
        in_specs=[pl.BlockSpec((tm, D_MODEL), lambda i: (i, 0)),
                  pl.BlockSpec((tm, D_MODEL), lambda i: (i, 0)),
                  pl.BlockSpec((D_MODEL, D_MODEL), const, pipeline_mode=pl.Buffered(1)),
                  pl.BlockSpec((1, D_MODEL), const),
                  pl.BlockSpec((None, D_MODEL, D_FF), pick, pipeline_mode=pl.Buffered(1)),
                  pl.BlockSpec((None, D_FF, D_MODEL), pick, pipeline_mode=pl.Buffered(1)),
                  pl.BlockSpec((1, D_MODEL), const)],
        out_specs=pl.BlockSpec((tm, D_MODEL), lambda i: (i, 0)),
        out_shape=jax.ShapeDtypeStruct((m, D_MODEL), F32),
        compiler_params=pltpu.CompilerParams(
            dimension_semantics=("arbitrary",),
            vmem_limit_bytes=VMEM_LIMIT_BIG if tm == DENSE_TILE else VMEM_LIMIT),
        name="ffn",
    )(x, y, wo, g, w1, w2, gf)


def _even_prompt_kernel(x0_ref, xn_ref, gin_ref, win_ref, stp_ref, sts_ref, stw_ref,
                        mu_ref, w0_ref, wdec_ref, a0_ref, wa_ref, gw2_ref, kk_ref, ka_ref, rk_ref,
                        gng_ref, gnb_ref, eseg_ref, poolw_ref, pools_ref, tri_ref,
                        y_ref, opool_ref, oshift_ref, owkv_ref,
                        p_s, hpool, hshift, S, r_s, kp_s, v_s, ld_s, kk_s, a_s, o_s,
                        lhs_b, add_b, vk_b, bend_b, pend_b, *, tt, start):
    i = pl.program_id(1)
    nt = pl.num_programs(1)
    C = WKV_CHUNK

    @pl.when(jnp.logical_and(i == 0, pl.program_id(0) == 0))
    def _first_projection():
        h0 = _rmsnorm(x0_ref[0], gin_ref[...]).astype(BF16)
        p_s[...] = jnp.dot(h0, win_ref[...], preferred_element_type=F32)

    @pl.when(i == 0)
    def _init():
        hpool[0:1, :] = jnp.zeros((1, POOL_DIM), F32)
        hpool[1:16, :] = stp_ref[0]
        hshift[...] = sts_ref[0]
        S[...] = jnp.zeros(S.shape, F32)
        for j in range(HEAD_PAIRS):
            S[j, 0:HEAD_DIM, 0:HEAD_DIM] = stw_ref[0, 2 * j]
            S[j, HEAD_DIM:PAIR_DIM, HEAD_DIM:PAIR_DIM] = stw_ref[0, 2 * j + 1]
        for j in range(HEAD_PAIRS):
            S[j] = jnp.transpose(S[j])

    p = p_s[...]
    rows = lax.broadcasted_iota(jnp.int32, (tt, 1), 0)
    e_seg = eseg_ref[...]
    prm = (mu_ref[...], w0_ref[...], wdec_ref[...], a0_ref[...], wa_ref[...], gw2_ref[...],
           kk_ref[...], ka_ref[...], rk_ref[...], e_seg)
    pw = {}

    def project_next():
        hb = _rmsnorm(xn_ref[0], gin_ref[...]).astype(BF16)
        yield
        for c0 in range(0, EVEN_PROJ, PROJ_PIECE):
            p_s[:, c0:c0 + PROJ_PIECE] = jnp.dot(hb, win_ref[:, c0:c0 + PROJ_PIECE],
                                                 preferred_element_type=F32)
            yield

    def pointwise():
        u = p[:, 0:POOL_DIM]
        ext = jnp.concatenate([hpool[...], u], axis=0)
        s2 = ext + pltpu.roll(ext, 1, 0)
        s4 = s2 + pltpu.roll(s2, 2, 0)
        s8 = s4 + pltpu.roll(s4, 4, 0)
        s16 = s8 + pltpu.roll(s8, 8, 0)
        sel = _pool_lane_select(s2, s4, s8, s16)[16:, :]
        pos = start + i * tt + rows
        cnt = jnp.minimum(_pool_window_lanes(), pos + 1).astype(F32)
        d = sel / cnt - u
        y_ref[0, :, 0:POOL_DIM] = (_bdot(d, poolw_ref[...]) * pools_ref[...]).astype(BF16)
        hpool[...] = ext[tt:tt + 16, :]
        yield
        P = p[:, POOL_DIM:EVEN_PROJ]
        Pprev = jnp.where(rows == 0, hshift[...], pltpu.roll(P, 1, 0))
        hshift[...] = P[tt - 1:tt, :]
        yield
        yield from _rwkv_pointwise_stages(P, Pprev, prm, pw)
        r_s[...] = pw["r"]
        kp_s[...] = pw["kp"]
        v_s[...] = pw["v"]
        ld_s[...] = pw["ld"]
        kk_s[...] = pw["kk"]
        a_s[...] = pw["a"]
        yield

    _run_interleaved(pointwise(), project_next())
    g, bonus = pw["g"], pw["bonus"]

    lane_c = lax.broadcasted_iota(jnp.int32, (C, PAIR_DIM), 1)
    row_c = lax.broadcasted_iota(jnp.int32, (C, PAIR_DIM), 0)
    head0 = lane_c < HEAD_DIM
    left = lane_c < C
    lo_strict = left & (lane_c < row_c)
    lo_incl = left & (lane_c <= row_c)
    hi_strict = jnp.logical_not(left) & (lane_c - C < row_c)
    hi_incl = jnp.logical_not(left) & (lane_c - C <= row_c)
    eye_r = (lane_c - C == row_c).astype(F32)
    zb = jnp.zeros((C, PAIR_DIM), BF16)
    zbw = jnp.zeros((C, 2 * PAIR_DIM), BF16)
    tri = tri_ref[...]
    pairs = range(HEAD_PAIRS)

    def stack_heads(x):
        z = jnp.zeros_like(x)
        return jnp.concatenate([jnp.where(head0, x, z), jnp.where(head0, z, x)], axis=0)

    def prepare(cg):
        qa_sm, qr_sm, v_sm, rhs_g, kb_src, slot = [], [], [], [], [], []
        for ci in range(WKV_PREP_CHUNKS):
            c = cg * WKV_PREP_CHUNKS + ci
            sl = slice(c * C, (c + 1) * C)
            R = r_s[sl, :]
            K = kp_s[sl, :]
            V = v_s[sl, :]
            LD = ld_s[sl, :]
            KK = kk_s[sl, :]
            KA = KK * a_s[sl, :]
            L = _exact_dot_lhs01(tri, LD)
            Lend = L[C - 1:C, :]
            enL = jnp.exp(-L)
            eE = jnp.exp(Lend - L)
            Qr = R * jnp.exp(L)
            Qa = KK * jnp.exp(L - LD)
            Kt = K * enL
            Bt = KA * enL
            Kend = K * eE
            Bend = KA * eE
            Pend = jnp.exp(Lend)
            for j in pairs:
                ls = slice(j * PAIR_DIM, (j + 1) * PAIR_DIM)
                qa_sm.append(stack_heads(Qa[:, ls]))
                qr_sm.append(stack_heads(Qr[:, ls]))
                v_sm.append(stack_heads(V[:, ls]).astype(BF16))
                bt = Bt[:, ls].astype(BF16)
                kt = Kt[:, ls].astype(BF16)
                rhs_g.append(jnp.concatenate([bt, kt], axis=0))
                kb_src.append((Kend[:, ls], Bend[:, ls], Pend[:, ls]))
                slot.append(c * HEAD_PAIRS + j)
        units = range(len(slot))
        heads = [(u, h) for u in units for h in range(2)]
        hrows = lambda x, h: x[h * C:(h + 1) * C]
        G = [_bdot_nt(jnp.concatenate([qa_sm[u], qr_sm[u]], axis=0), rhs_g[u]) for u in units]
        yield
        GA = [hrows(G[u], h) for u, h in heads]
        GR = [hrows(G[u], 2 + h) for u, h in heads]
        R = [jnp.where(lo_strict, -GA[k], eye_r) for k in range(len(heads))]
        level = 1
        while level < C:
            for k in range(len(heads)):
                rb = R[k].astype(BF16)
                P2 = jnp.dot(rb, jnp.concatenate([rb, zb], axis=0),
                             preferred_element_type=F32)
                R[k] = P2 + jnp.where(left, 0.0, R[k])
            level *= 2
            yield
        AV = []
        for k, (u, h) in enumerate(heads):
            lhs = jnp.concatenate([jnp.where(hi_strict, GA[k], 0.0), jnp.where(hi_incl, GR[k], 0.0)], axis=0)
            AV.append(jnp.dot(lhs.astype(BF16), jnp.concatenate([zb, hrows(v_sm[u], h)], axis=0),
                              preferred_element_type=F32))
        yield
        TQ = []
        for k, (u, h) in enumerate(heads):
            rhs = jnp.concatenate([hrows(qa_sm[u], h), AV[k][0:C]], axis=1).astype(BF16)
            TQ.append(jnp.dot(R[k].astype(BF16), jnp.concatenate([zbw, rhs], axis=0),
                              preferred_element_type=F32))
        yield
        AT = []
        for k, (u, h) in enumerate(heads):
            AT.append(jnp.dot(jnp.where(lo_incl, GR[k], 0.0).astype(BF16),
                              jnp.concatenate([TQ[k].astype(BF16), zbw], axis=0), preferred_element_type=F32))
        yield
        for u in units:
            kend, bend, pend = kb_src[u]
            kb_t = jnp.transpose(jnp.concatenate(
                [stack_heads(kend), stack_heads(bend), jnp.broadcast_to(pend, (PAIR_DIM, PAIR_DIM))],
                axis=1))
            for h in range(2):
                k = 2 * u + h
                lhs_b[slot[u], h * C:(h + 1) * C, :] = TQ[k][:, 0:PAIR_DIM].astype(BF16)
                lhs_b[slot[u], (2 + h) * C:(3 + h) * C, :] = (
                    hrows(qr_sm[u], h) - AT[k][:, 0:PAIR_DIM]).astype(BF16)
                add_b[slot[u], h * C:(h + 1) * C, :] = TQ[k][:, PAIR_DIM:2 * PAIR_DIM]
                add_b[slot[u], (2 + h) * C:(3 + h) * C, :] = AV[k][C:2 * C] - AT[k][:, PAIR_DIM:2 * PAIR_DIM]
            vk_b[slot[u]] = _bdot(kb_t[0:PAIR_DIM], v_sm[u])
            bend_b[slot[u]] = kb_t[PAIR_DIM:2 * PAIR_DIM].astype(BF16)
            pend_b[slot[u]] = kb_t[2 * PAIR_DIM:3 * PAIR_DIM]

    def advance(cg):
        for ci in range(WKV_PREP_CHUNKS):
            c = cg * WKV_PREP_CHUNKS + ci
            UO = [jnp.dot(lhs_b[c * HEAD_PAIRS + j], S[j].astype(BF16), preferred_element_type=F32)
                  + add_b[c * HEAD_PAIRS + j] for j in pairs]
            yield
            for j in pairs:
                u = c * HEAD_PAIRS + j
                S[j] = pend_b[u] * S[j] + vk_b[u] - jnp.dot(bend_b[u], UO[j][0:2 * C].astype(BF16),
                                                            preferred_element_type=F32)
                o_s[c * C:(c + 1) * C, j * PAIR_DIM:(j + 1) * PAIR_DIM] = UO[j][2 * C:3 * C] + UO[j][3 * C:4 * C]
            yield

    n_groups = tt // (C * WKV_PREP_CHUNKS)
    _run_interleaved(prepare(0))
    for cg in range(1, n_groups):
        _run_interleaved(prepare(cg), advance(cg - 1))
    _run_interleaved(advance(n_groups - 1))

    yb = _rwkv_post(o_s[...], bonus, g, gng_ref[...], gnb_ref[...], e_seg)
    y_ref[0, :, POOL_DIM:D_MODEL] = yb.astype(BF16)

    @pl.when(i == nt - 1)
    def _fin():
        opool_ref[0] = hpool[1:16, :]
        oshift_ref[0] = hshift[...]
        for j in range(HEAD_PAIRS):
            S[j] = jnp.transpose(S[j])
        for j in range(HEAD_PAIRS):
            owkv_ref[0, 2 * j] = S[j, 0:HEAD_DIM, 0:HEAD_DIM]
            owkv_ref[0, 2 * j + 1] = S[j, HEAD_DIM:PAIR_DIM, HEAD_DIM:PAIR_DIM]


def _even_prompt(x, g_in, w_in, st_pool, st_shift, st_wkv, prm, start):
    B, T, _ = x.shape
    tt = MIXER_TILE
    nt = T // tt
    bt = lambda b, i: (b, i, 0)
    bs3 = lambda b, i: (b, 0, 0)
    bs4 = lambda b, i: (b, 0, 0, 0)
    c2 = lambda b, i: (0, 0)
    vec = lambda n: pl.BlockSpec((1, n), c2)
    scr = lambda: pltpu.VMEM((tt, RWKV_DIM), F32)
    n_units = (tt // WKV_CHUNK) * HEAD_PAIRS
    return pl.pallas_call(
        functools.partial(_even_prompt_kernel, tt=tt, start=start),
        grid=(B, nt),
        in_specs=[pl.BlockSpec((1, tt, D_MODEL), lambda b, i: (0, 0, 0)),
                  pl.BlockSpec((1, tt, D_MODEL), functools.partial(_next_tile, nt=nt, n_tiles=B * nt)),
                  vec(D_MODEL),
                  pl.BlockSpec((D_MODEL, EVEN_PROJ), c2, pipeline_mode=pl.Buffered(1)),
                  pl.BlockSpec((1, POOL_BUF, POOL_DIM), bs3),
                  pl.BlockSpec((1, 1, RWKV_PROJ), bs3),
                  pl.BlockSpec((1, RWKV_HEADS, HEAD_DIM, HEAD_DIM), bs4),
                  vec(RWKV_PROJ), vec(RWKV_DIM), pl.BlockSpec((LORA_PAD, RWKV_DIM), c2), vec(RWKV_DIM),
                  pl.BlockSpec((LORA_PAD, RWKV_DIM), c2), pl.BlockSpec((LORA_PAD, RWKV_DIM), c2),
                  vec(RWKV_DIM), vec(RWKV_DIM), vec(RWKV_DIM), vec(RWKV_DIM), vec(RWKV_DIM),
                  pl.BlockSpec((SEG_TILE, SEG_TILE), c2), pl.BlockSpec((POOL_DIM, POOL_DIM), c2),
                  vec(POOL_DIM), pl.BlockSpec((WKV_CHUNK, WKV_CHUNK), c2)],
        out_specs=[pl.BlockSpec((1, tt, D_MODEL), bt),
                   pl.BlockSpec((1, POOL_BUF, POOL_DIM), bs3),
                   pl.BlockSpec((1, 1, RWKV_PROJ), bs3),
                   pl.BlockSpec((1, RWKV_HEADS, HEAD_DIM, HEAD_DIM), bs4)],
        out_shape=[jax.ShapeDtypeStruct((B, T, D_MODEL), BF16),
                   jax.ShapeDtypeStruct((B, POOL_BUF, POOL_DIM), F32),
                   jax.ShapeDtypeStruct((B, 1, RWKV_PROJ), F32),
                   jax.ShapeDtypeStruct((B, RWKV_HEADS, HEAD_DIM, HEAD_DIM), F32)],
        scratch_shapes=[pltpu.VMEM((tt, EVEN_PROJ), F32),
                        pltpu.VMEM((16, POOL_DIM), F32), pltpu.VMEM((1, RWKV_PROJ), F32),
                        pltpu.VMEM((HEAD_PAIRS, PAIR_DIM, PAIR_DIM), F32),
                        scr(), scr(), scr(), scr(), scr(), scr(), scr(),
                        pltpu.VMEM((n_units, 2 * PAIR_DIM, PAIR_DIM), BF16),
                        pltpu.VMEM((n_units, 2 * PAIR_DIM, PAIR_DIM), F32),
                        pltpu.VMEM((n_units, PAIR_DIM, PAIR_DIM), F32),
                        pltpu.VMEM((n_units, PAIR_DIM, PAIR_DIM), BF16),
                        pltpu.VMEM((n_units, PAIR_DIM, PAIR_DIM), F32)],
        compiler_params=pltpu.CompilerParams(dimension_semantics=("arbitrary", "arbitrary"),
                                             vmem_limit_bytes=VMEM_LIMIT_BIG),
        name="even_prompt",
    )(x, x, g_in, w_in, st_pool, st_shift, st_wkv, *prm)


def _odd_prompt_kernel(x0_ref, xn_ref, gin_ref, win_ref, stc_ref, stl_ref,
                       lng_ref, lnb_ref, ws_ref, bias_ref, cw_ref, cb_ref, wx_ref, bx_ref, wa_ref,
                       ba_ref, lam_ref, y_ref, oconv_ref, olru_ref, q_s, hconv, hl, mix_s, *, tt):
    i = pl.program_id(1)
    nt = pl.num_programs(1)

    @pl.when(jnp.logical_and(i == 0, pl.program_id(0) == 0))
    def _first_projection():
        h0 = _rmsnorm(x0_ref[0], gin_ref[...]).astype(BF16)
        q_s[...] = jnp.dot(h0, win_ref[...], preferred_element_type=F32)

    @pl.when(i == 0)
    def _init():
        hconv[0:5, :] = jnp.zeros((5, LRU_DIM), F32)
        hconv[5:8, :] = stc_ref[0]
        hl[...] = stl_ref[0]

    q = q_s[...]

    def project_next():
        hb = _rmsnorm(xn_ref[0], gin_ref[...]).astype(BF16)
        yield
        for c0 in range(0, ODD_PROJ, PROJ_PIECE):
            q_s[:, c0:c0 + PROJ_PIECE] = jnp.dot(hb, win_ref[:, c0:c0 + PROJ_PIECE],
                                                 preferred_element_type=F32)
            yield

    def mixers():
        u, vn = _gmlp_pre(q[:, 0:2 * GMLP_DIM], lng_ref[...], lnb_ref[...])
        yield
        rr = lax.broadcasted_iota(jnp.int32, (CHUNK, CHUNK), 0)
        cc = lax.broadcasted_iota(jnp.int32, (CHUNK, CHUNK), 1)
        causal = cc <= rr
        for h in range(GMLP_HEADS):
            wm = jnp.where(causal, ws_ref[h], 0.0).astype(BF16)
            ls = slice(h * CHUNK, (h + 1) * CHUNK)
            for c in range(tt // CHUNK):
                rs = slice(c * CHUNK, (c + 1) * CHUNK)
                mix_s[rs, ls] = (jnp.dot(wm, vn[rs, ls].astype(BF16), preferred_element_type=F32)
                                 + bias_ref[:, ls])
        y_ref[0, :, 0:GMLP_DIM] = (u * mix_s[...]).astype(BF16)
        yield

        gate = _gelu(q[:, 2 * GMLP_DIM:2 * GMLP_DIM + LRU_DIM])
        yield
        xr = q[:, 2 * GMLP_DIM + LRU_DIM:ODD_PROJ]
        hconv[8:8 + tt, :] = xr
        xc = xr * cw_ref[3:4, :] + cb_ref[...]
        for j in range(1, CONV_WIDTH):
            xc = xc + hconv[8 - j:8 - j + tt, :] * cw_ref[3 - j:4 - j, :]
        hconv[0:8, :] = hconv[tt:tt + 8, :]
        yield
        a, b = _lru_gates(xc, wx_ref[...], bx_ref[...], wa_ref[...], ba_ref[...], lam_ref[...])
        yield
        n_groups = tt // SCAN_GROUP
        a = a.reshape(n_groups, SCAN_GROUP, LRU_DIM)
        b = b.reshape(n_groups, SCAN_GROUP, LRU_DIM)
        in_group = lax.broadcasted_iota(jnp.int32, (1, SCAN_GROUP, 1), 1)
        dist = 1
        while dist < SCAN_GROUP:
            keep = in_group >= dist
            a_sh = jnp.where(keep, pltpu.roll(a, dist, 1), 1.0)
            b_sh = jnp.where(keep, pltpu.roll(b, dist, 1), 0.0)
            b = a * b_sh + b
            a = a * a_sh
            dist *= 2
            yield
        carry = hl[...]
        groups = []
        for gi in range(n_groups):
            hg = a[gi] * carry + b[gi]
            groups.append(hg)
            carry = hg[SCAN_GROUP - 1:SCAN_GROUP, :]
        h = jnp.concatenate(groups, axis=0)
        hl[...] = carry
        y_ref[0, :, GMLP_DIM:D_MODEL] = (h * gate).astype(BF16)
        yield

    _run_interleaved(mixers(), project_next())

    @pl.when(i == nt - 1)
    def _fin():
        oconv_ref[0] = hconv[5:8, :]
        olru_ref[0] = hl[...]


def _odd_prompt(x, g_in, w_in, st_conv, st_lru, prm):
    B, T, _ = x.shape
    tt = MIXER_TILE
    nt = T // tt
    bt = lambda b, i: (b, i, 0)
    bs3 = lambda b, i: (b, 0, 0)
    c2 = lambda b, i: (0, 0)
    c3 = lambda b, i: (0, 0, 0)
    vec = lambda n: pl.BlockSpec((1, n), c2)
    return pl.pallas_call(
        functools.partial(_odd_prompt_kernel, tt=tt),
        grid=(B, nt),
        in_specs=[pl.BlockSpec((1, tt, D_MODEL), lambda b, i: (0, 0, 0)),
                  pl.BlockSpec((1, tt, D_MODEL), functools.partial(_next_tile, nt=nt, n_tiles=B * nt)),
                  vec(D_MODEL),
                  pl.BlockSpec((D_MODEL, ODD_PROJ), c2, pipeline_mode=pl.Buffered(1)),
                  pl.BlockSpec((1, CONV_WIDTH - 1, LRU_DIM), bs3),
                  pl.BlockSpec((1, 1, LRU_DIM), bs3),
                  vec(GMLP_DIM), vec(GMLP_DIM),
                  pl.BlockSpec((GMLP_HEADS, CHUNK, CHUNK), c3),
                  pl.BlockSpec((CHUNK, GMLP_DIM), c2),
                  pl.BlockSpec((CONV_WIDTH, LRU_DIM), c2), vec(LRU_DIM),
                  pl.BlockSpec((LRU_DIM, LRU_DIM), c2), vec(LRU_DIM),
                  pl.BlockSpec((LRU_DIM, LRU_DIM), c2), vec(LRU_DIM), vec(LRU_DIM)],
        out_specs=[pl.BlockSpec((1, tt, D_MODEL), bt),
                   pl.BlockSpec((1, CONV_WIDTH - 1, LRU_DIM), bs3),
                   pl.BlockSpec((1, 1, LRU_DIM), bs3)],
        out_shape=[jax.ShapeDtypeStruct((B, T, D_MODEL), BF16),
                   jax.ShapeDtypeStruct((B, CONV_WIDTH - 1, LRU_DIM), F32),
                   jax.ShapeDtypeStruct((B, 1, LRU_DIM), F32)],
        scratch_shapes=[pltpu.VMEM((tt, ODD_PROJ), F32),
                        pltpu.VMEM((8 + tt, LRU_DIM), F32), pltpu.VMEM((1, LRU_DIM), F32),
                        pltpu.VMEM((tt, GMLP_DIM), F32)],
        compiler_params=pltpu.CompilerParams(dimension_semantics=("arbitrary", "arbitrary"),
                                             vmem_limit_bytes=VMEM_LIMIT),
        name="odd_prompt",
    )(x, x, g_in, w_in, st_conv, st_lru, *prm)


def _even_sample_pre_kernel(p_ref, stp_ref, sts_ref,
                            mu_ref, w0_ref, wdec_ref, a0_ref, wa_ref, gw2_ref, kk_ref, ka_ref, rk_ref,
                            eseg_ref, poolw_ref, pools_ref,
                            r_ref, w_ref, kkn_ref, kka_ref, kp_ref, v_ref, g_ref, bonus_ref, ya_ref,
                            opool_ref, oshift_ref, *, T, start):
    prm = (mu_ref[...], w0_ref[...], wdec_ref[...], a0_ref[...], wa_ref[...], gw2_ref[...],
           kk_ref[...], ka_ref[...], rk_ref[...], eseg_ref[...])
    full = [stp_ref[s] for s in range(POOL_BUF)] + [p_ref[t][:, 0:POOL_DIM] for t in range(T)]
    wl = _pool_window_lanes()
    for t in range(T):
        P = p_ref[t][:, POOL_DIM:EVEN_PROJ]
        Pprev = sts_ref[...] if t == 0 else p_ref[t - 1][:, POOL_DIM:EVEN_PROJ]
        r, kp, v, ld, kk, a, g, bonus = _rwkv_pointwise(P, Pprev, prm)
        r_ref[t] = jnp.transpose(r)
        w_ref[t] = jnp.transpose(jnp.exp(ld))
        kkn_ref[t] = jnp.transpose(kk)
        kka_ref[t] = jnp.transpose(kk * a)
        kp_ref[t] = jnp.transpose(kp)
        v_ref[t] = jnp.transpose(v)
        g_ref[t] = g
        bonus_ref[t] = bonus
        e = POOL_BUF + t
        s2 = full[e] + full[e - 1]
        s4 = s2 + full[e - 2] + full[e - 3]
        s8 = s4 + full[e - 4] + full[e - 5] + full[e - 6] + full[e - 7]
        s16 = s8
        for s in range(8, 16):
            s16 = s16 + full[e - s]
        sel = _pool_lane_select(s2, s4, s8, s16)
        cnt = jnp.minimum(wl, start + t + 1).astype(F32)
        d = sel / cnt - full[e]
        ya_ref[t] = _bdot(d, poolw_ref[...]) * pools_ref[...]
    for s in range(POOL_BUF):
        opool_ref[s] = full[T + s]
    oshift_ref[...] = p_ref[T - 1][:, POOL_DIM:EVEN_PROJ]


def _even_sample_pre(p, st_pool, st_shift, prm, start):
    T, B, _ = p.shape
    cm = jax.ShapeDtypeStruct((T, RWKV_DIM, B), F32)
    bm = jax.ShapeDtypeStruct((T, B, RWKV_DIM), F32)
    return pl.pallas_call(
        functools.partial(_even_sample_pre_kernel, T=T, start=start),
        out_shape=[cm] * 6 + [bm] * 2 + [jax.ShapeDtypeStruct((T, B, POOL_DIM), F32),
                                   jax.ShapeDtypeStruct((POOL_BUF, B, POOL_DIM), F32),
                                   jax.ShapeDtypeStruct((B, RWKV_PROJ), F32)],
        compiler_params=pltpu.CompilerParams(vmem_limit_bytes=VMEM_LIMIT),
        name="even_sample_pre",
    )(p, st_pool, st_shift, *prm)


def _wkv_sample_kernel(r_ref, w_ref, kk_ref, kka_ref, kp_ref, v_ref, s_ref, o_ref, so_ref, *, T):
    group = range(WKV_SAMPLE_GROUP)

    def body(ib, carry):
        v0 = pl.multiple_of(ib * WKV_SAMPLE_GROUP, WKV_SAMPLE_GROUP)
        blk = pl.ds(v0, WKV_SAMPLE_GROUP)
        S = [s_ref[0, v0 + u] for u in group]
        for t in range(T):
            kk, w, kka, kp, r = kk_ref[t], w_ref[t], kka_ref[t], kp_ref[t], r_ref[t]
            vv = v_ref[t, blk, :]
            sk = [jnp.sum(S[u] * kk, axis=0, keepdims=True) for u in group]
            S = [S[u] * w - sk[u] * kka + vv[u:u + 1, :] * kp for u in group]
            o_ref[t, blk, :] = jnp.concatenate(
                [jnp.sum(S[u] * r, axis=0, keepdims=True) for u in group], axis=0)
        for u in group:
            so_ref[0, v0 + u] = S[u]
        return carry

    lax.fori_loop(0, HEAD_DIM // WKV_SAMPLE_GROUP, body, 0)


def _wkv_sample(r, w, kk, kka, kp, v, s):
    T, _, B = r.shape
    row_spec = pl.BlockSpec((T, HEAD_DIM, B), lambda h: (0, h, 0))
    st_spec = pl.BlockSpec((1, HEAD_DIM, HEAD_DIM, B), lambda h: (h, 0, 0, 0))
    return pl.pallas_call(
        functools.partial(_wkv_sample_kernel, T=T),
        grid=(RWKV_HEADS,),
        in_specs=[row_spec] * 6 + [st_spec],
        out_specs=[row_spec, st_spec],
        out_shape=[jax.ShapeDtypeStruct((T, RWKV_DIM, B), F32),
                   jax.ShapeDtypeStruct((RWKV_HEADS, HEAD_DIM, HEAD_DIM, B), F32)],
        compiler_params=pltpu.CompilerParams(dimension_semantics=("arbitrary",),
                                             vmem_limit_bytes=VMEM_LIMIT),
        name="wkv_sample",
    )(r, w, kk, kka, kp, v, s)


def _even_sample_post_kernel(o_ref, bonus_ref, g_ref, ya_ref, gng_ref, gnb_ref, eseg_ref, y_ref, *, T):
    for t in range(T):
        o = jnp.transpose(o_ref[t])
        yb = _rwkv_post(o, bonus_ref[t], g_ref[t], gng_ref[...], gnb_ref[...], eseg_ref[...])
        y_ref[t, :, 0:POOL_DIM] = ya_ref[t].astype(BF16)
        y_ref[t, :, POOL_DIM:D_MODEL] = yb.astype(BF16)


def _even_sample_post(o, bonus, g, ya, gn_g, gn_b, e_seg):
    T, _, B = o.shape
    return pl.pallas_call(
        functools.partial(_even_sample_post_kernel, T=T),
        out_shape=jax.ShapeDtypeStruct((T, B, D_MODEL), BF16),
        compiler_params=pltpu.CompilerParams(vmem_limit_bytes=VMEM_LIMIT),
        name="even_sample_post",
    )(o, bonus, g, ya, gn_g, gn_b, e_seg)


def _odd_sample_kernel(q_ref, stc_ref, stl_ref,
                       lng_ref, lnb_ref, wsm_ref, bsm_ref, cw_ref, cb_ref, wx_ref, bx_ref, wa_ref,
                       ba_ref, lam_ref, y_ref, v_ref, oconv_ref, olru_ref, *, T):
    vns = []
    us = []
    for t in range(T):
        u, vn = _gmlp_pre(q_ref[t][:, 0:2 * GMLP_DIM], lng_ref[...], lnb_ref[...])
        us.append(u)
        vns.append(vn)
        v_ref[t] = vn
    full = [stc_ref[s] for s in range(CONV_WIDTH - 1)] + \
           [q_ref[t][:, 2 * GMLP_DIM + LRU_DIM:ODD_PROJ] for t in range(T)]
    h = stl_ref[...]
    for t in range(T):
        mix = bsm_ref[t:t + 1, :]
        for j in range(t + 1):
            mix = mix + wsm_ref[t * T + j:t * T + j + 1, :] * vns[j]
        y_ref[t, :, 0:GMLP_DIM] = (us[t] * mix).astype(BF16)
        xc = full[t + CONV_WIDTH - 1] * cw_ref[CONV_WIDTH - 1:CONV_WIDTH, :] + cb_ref[...]
        for j in range(CONV_WIDTH - 1):
            xc = xc + full[t + j] * cw_ref[j:j + 1, :]
        a, b = _lru_gates(xc, wx_ref[...], bx_ref[...], wa_ref[...], ba_ref[...], lam_ref[...])
        h = a * h + b
        gate_in = q_ref[t][:, 2 * GMLP_DIM:2 * GMLP_DIM + LRU_DIM]
        y_ref[t, :, GMLP_DIM:D_MODEL] = (h * _gelu(gate_in)).astype(BF16)
    for s in range(CONV_WIDTH - 1):
        oconv_ref[s] = full[T + s]
    olru_ref[...] = h


def _odd_sample(q, st_conv, st_lru, prm):
    T, B, _ = q.shape
    return pl.pallas_call(
        functools.partial(_odd_sample_kernel, T=T),
        out_shape=[jax.ShapeDtypeStruct((T, B, D_MODEL), BF16),
                   jax.ShapeDtypeStruct((T, B, GMLP_DIM), F32),
                   jax.ShapeDtypeStruct((CONV_WIDTH - 1, B, LRU_DIM), F32),
                   jax.ShapeDtypeStruct((B, LRU_DIM), F32)],
        compiler_params=pltpu.CompilerParams(vmem_limit_bytes=VMEM_LIMIT),
        name="odd_sample",
    )(q, st_conv, st_lru, *prm)


def _block_diag(w):
    n, c, d = w.shape
    eye = jnp.eye(n, dtype=w.dtype)
    return (eye[:, None, :, None] * w[:, :, None, :]).reshape(n * c, n * d)


def _row(x):
    return x.reshape(1, -1)


def kernel(x_prompt, x_sample, state_pool, state_shift, state_wkv, state_conv, state_lru, ev_norm_g, ev_w_in, pool_w, pool_scale, rwkv_mu, rwkv_w0, rwkv_w_w2, rwkv_a0, rwkv_a_w2, rwkv_g_w2, rwkv_k_k, rwkv_k_a, rwkv_r_k, rwkv_gn_g, rwkv_gn_b, ev_w_out, od_norm_g, od_w_in, gmlp_ln_g, gmlp_ln_b, gmlp_ws, gmlp_bs, lru_conv_w, lru_conv_b, lru_wx, lru_bx, lru_wa, lru_ba, lru_lam, od_w_out, ff_norm_g, ff_w1, ff_w2, final_norm_g):
    B, T, _ = x_prompt.shape
    DB, DT, _ = x_sample.shape

    seg_ids = jnp.arange(SEG_TILE) // HEAD_DIM
    e_seg = (seg_ids[:, None] == seg_ids[None, :]).astype(BF16)
    tri = (jnp.arange(WKV_CHUNK)[None, :] <= jnp.arange(WKV_CHUNK)[:, None]).astype(BF16)
    zlora = jnp.zeros((LORA_PAD // 2, RWKV_DIM), F32)

    ev_common = (_row(rwkv_mu[0]), _row(rwkv_w0[0]),
                 jnp.concatenate([rwkv_w_w2[0], zlora], 0).astype(BF16), _row(rwkv_a0[0]),
                 jnp.concatenate([zlora, rwkv_a_w2[0]], 0).astype(BF16), rwkv_g_w2[0].astype(BF16),
                 _row(rwkv_k_k[0]), _row(rwkv_k_a[0]), _row(rwkv_r_k[0]))
    gn_g, gn_b = _row(rwkv_gn_g[0]), _row(rwkv_gn_b[0])
    pool_bd = _block_diag(pool_w[0]).astype(BF16)
    pool_sc = _row(pool_scale[0])
    w_in0 = ev_w_in[0].astype(BF16)
    g_in0 = _row(ev_norm_g[0])

    xp = x_prompt.reshape(B * T, D_MODEL)
    xs = jnp.transpose(x_sample, (1, 0, 2)).reshape(DT * DB, D_MODEL)

    ps = _inproj(xs, g_in0, w_in0).reshape(DT, DB, EVEN_PROJ)

    yp, p_pool, p_shift, p_wkv = _even_prompt(
        x_prompt, g_in0, w_in0,
        jnp.zeros((B, POOL_BUF, POOL_DIM), F32), jnp.zeros((B, 1, RWKV_PROJ), F32),
        jnp.zeros((B, RWKV_HEADS, HEAD_DIM, HEAD_DIM), F32),
        ev_common + (gn_g, gn_b, e_seg, pool_bd, pool_sc, tri), 0)

    pre = _even_sample_pre(ps, jnp.transpose(state_pool[0], (1, 0, 2)), state_shift[0],
                           ev_common + (e_seg, pool_bd, pool_sc), PAST_LEN)
    r_s, w_s, kk_s, kka_s, kp_s, v_s, g_s, bonus_s, ya_s, s_pool_tm, s_shift = pre
    o_s, s_wkv_bl = _wkv_sample(r_s, w_s, kk_s, kka_s, kp_s, v_s,
                                jnp.transpose(state_wkv[0], (1, 2, 3, 0)))
    ys = _even_sample_post(o_s, bonus_s, g_s, ya_s, gn_g, gn_b, e_seg)

    w_out0 = ev_w_out[0].astype(BF16)
    ffg = lambda l: _row(ff_norm_g[l])
    gfin = _row(final_norm_g)
    ff_w1_b, ff_w2_b = ff_w1.astype(BF16), ff_w2.astype(BF16)
    xp = _ffn(xp, yp.reshape(B * T, D_MODEL), w_out0, ffg(0), ff_w1_b, ff_w2_b, gfin, 0, False)
    xs = _ffn(xs, ys.reshape(DT * DB, D_MODEL), w_out0, ffg(0), ff_w1_b, ff_w2_b, gfin, 0, False)

    w_in1 = od_w_in[0].astype(BF16)
    g_in1 = _row(od_norm_g[0])
    qs = _inproj(xs, g_in1, w_in1).reshape(DT, DB, ODD_PROJ)

    lru_common = (lru_conv_w[0], _row(lru_conv_b[0]), _block_diag(lru_wx[0]).astype(BF16), _row(lru_bx[0]),
                  _block_diag(lru_wa[0]).astype(BF16), _row(lru_ba[0]), _row(lru_lam[0]))
    ln = (_row(gmlp_ln_g[0]), _row(gmlp_ln_b[0]))
    bias_full = jnp.repeat(jnp.transpose(gmlp_bs[0]), CHUNK, axis=1)
    yp, p_conv, p_lru = _odd_prompt(
        xp.reshape(B, T, D_MODEL), g_in1, w_in1,
        jnp.zeros((B, CONV_WIDTH - 1, LRU_DIM), F32), jnp.zeros((B, 1, LRU_DIM), F32),
        ln + (gmlp_ws[0], bias_full) + lru_common)

    ws_small = jnp.repeat(jnp.transpose(gmlp_ws[0][:, :DT, :DT], (1, 2, 0)).reshape(DT * DT, GMLP_HEADS),
                          CHUNK, axis=1)
    ys, s_v, s_conv_tm, s_lru = _odd_sample(
        qs, jnp.transpose(state_conv[0], (1, 0, 2)), state_lru[0],
        ln + (ws_small, bias_full[:DT]) + lru_common)

    w_out1 = od_w_out[0].astype(BF16)
    xp = _ffn(xp, yp.reshape(B * T, D_MODEL), w_out1, ffg(1), ff_w1_b, ff_w2_b, gfin, 1, True)
    xs = _ffn(xs, ys.reshape(DT * DB, D_MODEL), w_out1, ffg(1), ff_w1_b, ff_w2_b, gfin, 1, True)

    tm2bm = lambda t: jnp.transpose(t, (1, 0, 2))
    y_prompt = xp.reshape(B, T, D_MODEL)
    y_sample = tm2bm(xs.reshape(DT, DB, D_MODEL))
    return (y_prompt, y_sample,
            p_pool[None], p_shift.reshape(1, B, RWKV_PROJ), p_wkv[None],
            p_conv[None], p_lru.reshape(1, B, LRU_DIM),
            tm2bm(s_pool_tm)[None], s_shift[None],
            jnp.transpose(s_wkv_bl, (3, 0, 1, 2))[None],
            tm2bm(s_conv_tm)[None], s_lru[None], tm2bm(s_v)[None])
```

```python
import functools

import jax
import jax.numpy as jnp
from jax import lax
from jax.experimental import pallas as pl
from jax.experimental.pallas import tpu as pltpu

F32 = jnp.float32
BF16 = jnp.bfloat16

D_MODEL = 1024
NORM_EPS = 1e-6
D_FF = 4 * D_MODEL

POOL_WINDOWS = (2, 4, 8, 16)
POOL_GROUP_DIM = 64
POOL_DIM = 256
POOL_BUF = 15

HEAD_DIM = 64
RWKV_DIM = 768
RWKV_HEADS = 12
HEAD_PAIRS = RWKV_HEADS // 2
PAIR_DIM = 2 * HEAD_DIM
RWKV_PROJ = 2560
RWKV_GN_EPS = 64e-5
EXP_NEG_HALF = 0.6065306597126334
EVEN_PROJ = POOL_DIM + RWKV_PROJ
LORA_OFF = 3 * RWKV_DIM
LORA_PAD = 128
GATE_OFF = LORA_OFF + LORA_PAD

CHUNK = 128
GMLP_DIM = 512
GMLP_HEADS = 4
LN_EPS = 1e-5
GELU_C = 0.7978845608028654
LRU_DIM = 512
CONV_WIDTH = 4
LRU_C = 8.0
ODD_PROJ = 2048

WKV_CHUNK = 64
SEG_TILE = 256
SCAN_GROUP = 8
PROJ_PIECE = 256
WKV_PREP_CHUNKS = 2
WKV_SAMPLE_GROUP = 8

PAST_LEN = 16384
MIXER_TILE = 512
DENSE_TILE = 1024

V7X_VMEM_BYTES = 64 * 1024 * 1024
VMEM_LIMIT = V7X_VMEM_BYTES * 3 // 4
VMEM_LIMIT_BIG = V7X_VMEM_BYTES * 7 // 8
VMEM_LIMIT_FFN = V7X_VMEM_BYTES * 29 // 32


def _bdot(a, b):
    return jnp.dot(a.astype(BF16), b.astype(BF16), preferred_element_type=F32)


def _bdot_nt(a, b):
    return lax.dot_general(a.astype(BF16), b.astype(BF16), (((1,), (1,)), ((), ())),
                           preferred_element_type=F32)


def _split3(x):
    hi = x.astype(BF16)
    r1 = x - hi.astype(F32)
    mid = r1.astype(BF16)
    lo = (r1 - mid.astype(F32)).astype(BF16)
    return hi, mid, lo


def _exact_dot_rhs01(x, e):
    hi = x.astype(BF16)
    lo = (x - hi.astype(F32)).astype(BF16)
    d = lambda t: jnp.dot(t, e, preferred_element_type=F32)
    return d(hi) + d(lo)


def _exact_dot_lhs01(e, x):
    hi, mid, lo = _split3(x)
    d = lambda t: jnp.dot(e, t, preferred_element_type=F32)
    return d(hi) + d(mid) + d(lo)


def _segsum(x, e_seg):
    parts = [_exact_dot_rhs01(x[:, g * SEG_TILE:(g + 1) * SEG_TILE], e_seg)
             for g in range(RWKV_DIM // SEG_TILE)]
    return jnp.concatenate(parts, axis=1)


def _softplus(z):
    return jnp.maximum(z, 0.0) + jnp.log(1.0 + jnp.exp(-jnp.abs(z)))


def _sigmoid(z):
    return 0.5 * jnp.tanh(0.5 * z) + 0.5


def _gelu(z):
    hz = 0.5 * z
    return hz + hz * jnp.tanh(z * (GELU_C + (GELU_C * 0.044715) * (z * z)))


def _rmsnorm(x, g):
    ms = jnp.mean(x * x, axis=-1, keepdims=True)
    return x * lax.rsqrt(ms + NORM_EPS) * g


def _next_tile(b, i, *, nt, n_tiles):
    n = jnp.minimum(b * nt + i + 1, n_tiles - 1)
    return (n // nt, n % nt, 0)


def _run_interleaved(*stages):
    live = list(stages)
    while live:
        for gen in list(live):
            if next(gen, StopIteration) is StopIteration:
                live.remove(gen)


_POINTWISE_KEYS = ("r", "kp", "v", "ld", "kk", "a", "g", "bonus")


def _rwkv_pointwise_stages(P, Pprev, prm, out):
    (mu, w0, wdec, a0, wa, gw2, k_k, k_a, r_k, e_seg) = prm
    xs = P + (Pprev - P) * mu
    r = xs[:, 0:RWKV_DIM]
    k = xs[:, RWKV_DIM:2 * RWKV_DIM]
    v = xs[:, 2 * RWKV_DIM:3 * RWKV_DIM]
    c_wa = xs[:, LORA_OFF:GATE_OFF]
    cg = xs[:, GATE_OFF:RWKV_PROJ]
    yield
    ld = -EXP_NEG_HALF * _sigmoid(w0 + _bdot(jnp.tanh(c_wa), wdec))
    yield
    a = _sigmoid(a0 + _bdot(c_wa, wa))
    yield
    g = _bdot(_sigmoid(cg), gw2)
    yield
    kk = k * k_k
    kk = kk * lax.rsqrt(jnp.maximum(_segsum(kk * kk, e_seg), 1e-24))
    yield
    kp = k * (1.0 + (a - 1.0) * k_a)
    yield
    bonus = _segsum(r * kp * r_k, e_seg) * v
    out.update(r=r, kp=kp, v=v, ld=ld, kk=kk, a=a, g=g, bonus=bonus)
    yield


def _rwkv_pointwise(P, Pprev, prm):
    out = {}
    for _ in _rwkv_pointwise_stages(P, Pprev, prm, out):
        pass
    return tuple(out[key] for key in _POINTWISE_KEYS)


def _rwkv_post(o, bonus, g, gn_g, gn_b, e_seg):
    m = _segsum(o, e_seg) * (1.0 / HEAD_DIM)
    d = o - m
    var = _segsum(d * d, e_seg) * (1.0 / HEAD_DIM)
    on = d * lax.rsqrt(var + RWKV_GN_EPS) * gn_g + gn_b
    return (on + bonus) * g


def _pool_by_group(per_group):
    lane = lax.broadcasted_iota(jnp.int32, (1, POOL_DIM), 1)
    out = per_group[-1]
    for i in range(len(per_group) - 2, -1, -1):
        out = jnp.where(lane < (i + 1) * POOL_GROUP_DIM, per_group[i], out)
    return out


def _pool_lane_select(s2, s4, s8, s16):
    return _pool_by_group((s2, s4, s8, s16))


def _pool_window_lanes():
    return _pool_by_group(POOL_WINDOWS)


def _lru_gates(xc, wx, bx, wa, ba, lam):
    gx = _sigmoid(_bdot(xc, wx) + bx)
    ga = _sigmoid(_bdot(xc, wa) + ba)
    log_a = -LRU_C * ga * _softplus(-lam)
    a = jnp.exp(log_a)
    b = jnp.sqrt(-jnp.tanh(log_a) * (a * a + 1.0)) * gx * xc
    return a, b


def _gmlp_pre(zq, ln_g, ln_b):
    z = _gelu(zq)
    u = z[:, :GMLP_DIM]
    v = z[:, GMLP_DIM:]
    m = jnp.mean(v, axis=-1, keepdims=True)
    d = v - m
    var = jnp.mean(d * d, axis=-1, keepdims=True)
    return u, d * lax.rsqrt(var + LN_EPS) * ln_g + ln_b


def _inproj_kernel(x_ref, g_ref, w_ref, o_ref):
    h = _rmsnorm(x_ref[...], g_ref[...])
    o_ref[...] = jnp.dot(h.astype(BF16), w_ref[...], preferred_element_type=F32)


def _inproj(x, g, w):
    m, n = x.shape[0], w.shape[1]
    tm = min(DENSE_TILE, m)
    return pl.pallas_call(
        _inproj_kernel,
        grid=(m // tm,),
        in_specs=[pl.BlockSpec((tm, D_MODEL), lambda i: (i, 0)),
                  pl.BlockSpec((1, D_MODEL), lambda i: (0, 0)),
                  pl.BlockSpec((D_MODEL, n), lambda i: (0, 0), pipeline_mode=pl.Buffered(1))],
        out_specs=pl.BlockSpec((tm, n), lambda i: (i, 0)),
        out_shape=jax.ShapeDtypeStruct((m, n), F32),
        compiler_params=pltpu.CompilerParams(dimension_semantics=("arbitrary",),
                                             vmem_limit_bytes=VMEM_LIMIT),
        name="inproj",
    )(x, g, w)


def _ffn_kernel(xp_ref, yp_ref, xs_ref, ys_ref, wo_ref, g_ref, w1_ref, w2_ref, gf_ref, op_ref, os_ref, *, final):
    def mlp(x_ref, y_ref, o_ref):
        x1 = x_ref[...] + jnp.dot(y_ref[...], wo_ref[...], preferred_element_type=F32)
        hf = _rmsnorm(x1, g_ref[...]).astype(BF16)
        acc = x1
        fc = 1024
        for c in range(D_FF // fc):
            h = jnp.dot(hf, w1_ref[:, c * fc:(c + 1) * fc], preferred_element_type=F32)
            h = jnp.square(jnp.maximum(h, 0.0)).astype(BF16)
            acc = acc + jnp.dot(h, w2_ref[c * fc:(c + 1) * fc, :], preferred_element_type=F32)
        if final:
            acc = _rmsnorm(acc, gf_ref[...])
        o_ref[...] = acc

    step = pl.program_id(0)
    last = pl.num_programs(0) - 1
    pl.when(step < last)(functools.partial(mlp, xp_ref, yp_ref, op_ref))
    pl.when(step == last)(functools.partial(mlp, xs_ref, ys_ref, os_ref))


def _ffn(xp, yp, xs, ys, wo, g, w1, w2, gf, layer, final):
    mp, ms = xp.shape[0], xs.shape[0]
    tm = DENSE_TILE
    n_p = mp // tm
    const = lambda i: (0, 0)
    pick = lambda i: (layer, 0, 0)
    prompt_rows = lambda i: (jnp.minimum(i, n_p - 1), 0)
    once = dict(pipeline_mode=pl.Buffered(1))
    return pl.pallas_call(
        functools.partial(_ffn_kernel, final=final),
        grid=(n_p + 1,),
        in_specs=[pl.BlockSpec((tm, D_MODEL), prompt_rows),
                  pl.BlockSpec((tm, D_MODEL), prompt_rows),
                  pl.BlockSpec((ms, D_MODEL), const, **once),
                  pl.BlockSpec((ms, D_MODEL), const, **once),
                  pl.BlockSpec((D_MODEL, D_MODEL), const, **once),
                  pl.BlockSpec((1, D_MODEL), const),
                  pl.BlockSpec((None, D_MODEL, D_FF), pick, **once),
                  pl.BlockSpec((None, D_FF, D_MODEL), pick, **once),
                  pl.BlockSpec((1, D_MODEL), const)],
        out_specs=[pl.BlockSpec((tm, D_MODEL), prompt_rows),
                   pl.BlockSpec((ms, D_MODEL), const)],
        out_shape=[jax.ShapeDtypeStruct((mp, D_MODEL), F32),
                   jax.ShapeDtypeStruct((ms, D_MODEL), F32)],
        compiler_params=pltpu.CompilerParams(dimension_semantics=("arbitrary",),
                                             vmem_limit_bytes=VMEM_LIMIT_FFN),
        name="ffn",
    )(xp, yp, xs, ys, wo, g, w1, w2, gf)


def _even_prompt_kernel(x0_ref, xn_ref, gin_ref, win_ref, stp_ref, sts_ref, stw_ref,
                        mu_ref, w0_ref, wdec_ref, a0_ref, wa_ref, gw2_ref, kk_ref, ka_ref, rk_ref,
                        gng_ref, gnb_ref, eseg_ref, poolw_ref, pools_ref, tri_ref,
                        y_ref, opool_ref, oshift_ref, owkv_ref,
                        p_s, hpool, hshift, S, r_s, kp_s, v_s, ld_s, kk_s, a_s, o_s,
                        lhs_b, add_b, vk_b, bend_b, pend_b, *, tt, start):
    i = pl.program_id(1)
    nt = pl.num_programs(1)
    C = WKV_CHUNK

    @pl.when(jnp.logical_and(i == 0, pl.program_id(0) == 0))
    def _first_projection():
        h0 = _rmsnorm(x0_ref[0], gin_ref[...]).astype(BF16)
        p_s[...] = jnp.dot(h0, win_ref[...], preferred_element_type=F32)

    @pl.when(i == 0)
    def _init():
        hpool[0:1, :] = jnp.zeros((1, POOL_DIM), F32)
        hpool[1:16, :] = stp_ref[0]
        hshift[...] = sts_ref[0]
        S[...] = jnp.zeros(S.shape, F32)
        for j in range(HEAD_PAIRS):
            S[j, 0:HEAD_DIM, 0:HEAD_DIM] = stw_ref[0, 2 * j]
            S[j, HEAD_DIM:PAIR_DIM, HEAD_DIM:PAIR_DIM] = stw_ref[0, 2 * j + 1]
        for j in range(HEAD_PAIRS):
            S[j] = jnp.transpose(S[j])

    p = p_s[...]
    rows = lax.broadcasted_iota(jnp.int32, (tt, 1), 0)
    e_seg = eseg_ref[...]
    prm = (mu_ref[...], w0_ref[...], wdec_ref[...], a0_ref[...], wa_ref[...], gw2_ref[...],
           kk_ref[...], ka_ref[...], rk_ref[...], e_seg)
    pw = {}

    def project_next():
        hb = _rmsnorm(xn_ref[0], gin_ref[...]).astype(BF16)
        yield
        for c0 in range(0, EVEN_PROJ, PROJ_PIECE):
            p_s[:, c0:c0 + PROJ_PIECE] = jnp.dot(hb, win_ref[:, c0:c0 + PROJ_PIECE],
                                                 preferred_element_type=F32)
            yield

    def pointwise():
        u = p[:, 0:POOL_DIM]
        ext = jnp.concatenate([hpool[...], u], axis=0)
        s2 = ext + pltpu.roll(ext, 1, 0)
        s4 = s2 + pltpu.roll(s2, 2, 0)
        s8 = s4 + pltpu.roll(s4, 4, 0)
        s16 = s8 + pltpu.roll(s8, 8, 0)
        sel = _pool_lane_select(s2, s4, s8, s16)[16:, :]
        pos = start + i * tt + rows
        cnt = jnp.minimum(_pool_window_lanes(), pos + 1).astype(F32)
        d = sel / cnt - u
        y_ref[0, :, 0:POOL_DIM] = (_bdot(d, poolw_ref[...]) * pools_ref[...]).astype(BF16)
        hpool[...] = ext[tt:tt + 16, :]
        yield
        P = p[:, POOL_DIM:EVEN_PROJ]
        Pprev = jnp.where(rows == 0, hshift[...], pltpu.roll(P, 1, 0))
        hshift[...] = P[tt - 1:tt, :]
        yield
        yield from _rwkv_pointwise_stages(P, Pprev, prm, pw)
        r_s[...] = pw["r"]
        kp_s[...] = pw["kp"]
        v_s[...] = pw["v"]
        ld_s[...] = pw["ld"]
        kk_s[...] = pw["kk"]
        a_s[...] = pw["a"]
        yield

    _run_interleaved(pointwise(), project_next())
    g, bonus = pw["g"], pw["bonus"]

    lane_c = lax.broadcasted_iota(jnp.int32, (C, PAIR_DIM), 1)
    row_c = lax.broadcasted_iota(jnp.int32, (C, PAIR_DIM), 0)
    head0 = lane_c < HEAD_DIM
    left = lane_c < C
    lo_strict = left & (lane_c < row_c)
    lo_incl = left & (lane_c <= row_c)
    hi_strict = jnp.logical_not(left) & (lane_c - C < row_c)
    hi_incl = jnp.logical_not(left) & (lane_c - C <= row_c)
    eye_r = (lane_c - C == row_c).astype(F32)
    zb = jnp.zeros((C, PAIR_DIM), BF16)
    zbw = jnp.zeros((C, 2 * PAIR_DIM), BF16)
    tri = tri_ref[...]
    pairs = range(HEAD_PAIRS)

    def stack_heads(x):
        z = jnp.zeros_like(x)
        return jnp.concatenate([jnp.where(head0, x, z), jnp.where(head0, z, x)], axis=0)

    def prepare(cg):
        qa_sm, qr_sm, v_sm, rhs_g, kb_src, slot = [], [], [], [], [], []
        for ci in range(WKV_PREP_CHUNKS):
            c = cg * WKV_PREP_CHUNKS + ci
            sl = slice(c * C, (c + 1) * C)
            R = r_s[sl, :]
            K = kp_s[sl, :]
            V = v_s[sl, :]
            LD = ld_s[sl, :]
            KK = kk_s[sl, :]
            KA = KK * a_s[sl, :]
            L = _exact_dot_lhs01(tri, LD)
            Lend = L[C - 1:C, :]
            enL = jnp.exp(-L)
            eE = jnp.exp(Lend - L)
            Qr = R * jnp.exp(L)
            Qa = KK * jnp.exp(L - LD)
            Kt = K * enL
            Bt = KA * enL
            Kend = K * eE
            Bend = KA * eE
            Pend = jnp.exp(Lend)
            for j in pairs:
                ls = slice(j * PAIR_DIM, (j + 1) * PAIR_DIM)
                qa_sm.append(stack_heads(Qa[:, ls]))
                qr_sm.append(stack_heads(Qr[:, ls]))
                v_sm.append(stack_heads(V[:, ls]).astype(BF16))
                bt = Bt[:, ls].astype(BF16)
                kt = Kt[:, ls].astype(BF16)
                rhs_g.append(jnp.concatenate([bt, kt], axis=0))
                kb_src.append((Kend[:, ls], Bend[:, ls], Pend[:, ls]))
                slot.append(c * HEAD_PAIRS + j)
        units = range(len(slot))
        heads = [(u, h) for u in units for h in range(2)]
        hrows = lambda x, h: x[h * C:(h + 1) * C]
        G = [_bdot_nt(jnp.concatenate([qa_sm[u], qr_sm[u]], axis=0), rhs_g[u]) for u in units]
        yield
        GA = [hrows(G[u], h) for u, h in heads]
        GR = [hrows(G[u], 2 + h) for u, h in heads]
        R = [jnp.where(lo_strict, -GA[k], eye_r) for k in range(len(heads))]
        level = 1
        while level < C:
            for k in range(len(heads)):
                rb = R[k].astype(BF16)
                P2 = jnp.dot(rb, jnp.concatenate([rb, zb], axis=0),
                             preferred_element_type=F32)
                R[k] = P2 + jnp.where(left, 0.0, R[k])
            level *= 2
            yield
        AV = []
        for k, (u, h) in enumerate(heads):
            lhs = jnp.concatenate([jnp.where(hi_strict, GA[k], 0.0), jnp.where(hi_incl, GR[k], 0.0)], axis=0)
            AV.append(jnp.dot(lhs.astype(BF16), jnp.concatenate([zb, hrows(v_sm[u], h)], axis=0),
                              preferred_element_type=F32))
        yield
        TQ = []
        for k, (u, h) in enumerate(heads):
            rhs = jnp.concatenate([hrows(qa_sm[u], h), AV[k][0:C]], axis=1).astype(BF16)
            TQ.append(jnp.dot(R[k].astype(BF16), jnp.concatenate([zbw, rhs], axis=0),
                              preferred_element_type=F32))
        yield
        AT = []
        for k, (u, h) in enumerate(heads):
            AT.append(jnp.dot(jnp.where(lo_incl, GR[k], 0.0).astype(BF16),
                              jnp.concatenate([TQ[k].astype(BF16), zbw], axis=0), preferred_element_type=F32))
        yield
        for u in units:
            kend, bend, pend = kb_src[u]
            kb_t = jnp.transpose(jnp.concatenate(
                [stack_heads(kend), stack_heads(bend), jnp.broadcast_to(pend, (PAIR_DIM, PAIR_DIM))],
                axis=1))
            for h in range(2):
                k = 2 * u + h
                lhs_b[slot[u], h * C:(h + 1) * C, :] = TQ[k][:, 0:PAIR_DIM].astype(BF16)
                lhs_b[slot[u], (2 + h) * C:(3 + h) * C, :] = (
                    hrows(qr_sm[u], h) - AT[k][:, 0:PAIR_DIM]).astype(BF16)
                add_b[slot[u], h * C:(h + 1) * C, :] = TQ[k][:, PAIR_DIM:2 * PAIR_DIM]
                add_b[slot[u], (2 + h) * C:(3 + h) * C, :] = AV[k][C:2 * C] - AT[k][:, PAIR_DIM:2 * PAIR_DIM]
            vk_b[slot[u]] = _bdot(kb_t[0:PAIR_DIM], v_sm[u])
            bend_b[slot[u]] = kb_t[PAIR_DIM:2 * PAIR_DIM].astype(BF16)
            pend_b[slot[u]] = kb_t[2 * PAIR_DIM:3 * PAIR_DIM]

    def advance(cg):
        for ci in range(WKV_PREP_CHUNKS):
            c = cg * WKV_PREP_CHUNKS + ci
            UO = [jnp.dot(lhs_b[c * HEAD_PAIRS + j], S[j].astype(BF16), preferred_element_type=F32)
                  + add_b[c * HEAD_PAIRS + j] for j in pairs]
            yield
            for j in pairs:
                u = c * HEAD_PAIRS + j
                S[j] = pend_b[u] * S[j] + vk_b[u] - jnp.dot(bend_b[u], UO[j][0:2 * C].astype(BF16),
                                                            preferred_element_type=F32)
                o_s[c * C:(c + 1) * C, j * PAIR_DIM:(j + 1) * PAIR_DIM] = UO[j][2 * C:3 * C] + UO[j][3 * C:4 * C]
            yield

    n_groups = tt // (C * WKV_PREP_CHUNKS)
    _run_interleaved(prepare(0))
    for cg in range(1, n_groups):
        _run_interleaved(prepare(cg), advance(cg - 1))
    _run_interleaved(advance(n_groups - 1))

    yb = _rwkv_post(o_s[...], bonus, g, gng_ref[...], gnb_ref[...], e_seg)
    y_ref[0, :, POOL_DIM:D_MODEL] = yb.astype(BF16)

    @pl.when(i == nt - 1)
    def _fin():
        opool_ref[0] = hpool[1:16, :]
        oshift_ref[0] = hshift[...]
        for j in range(HEAD_PAIRS):
            S[j] = jnp.transpose(S[j])
        for j in range(HEAD_PAIRS):
            owkv_ref[0, 2 * j] = S[j, 0:HEAD_DIM, 0:HEAD_DIM]
            owkv_ref[0, 2 * j + 1] = S[j, HEAD_DIM:PAIR_DIM, HEAD_DIM:PAIR_DIM]


def _even_prompt(x, g_in, w_in, st_pool, st_shift, st_wkv, prm, start):
    B, T, _ = x.shape
    tt = MIXER_TILE
    nt = T // tt
    bt = lambda b, i: (b, i, 0)
    bs3 = lambda b, i: (b, 0, 0)
    bs4 = lambda b, i: (b, 0, 0, 0)
    c2 = lambda b, i: (0, 0)
    vec = lambda n: pl.BlockSpec((1, n), c2)
    scr = lambda: pltpu.VMEM((tt, RWKV_DIM), F32)
    n_units = (tt // WKV_CHUNK) * HEAD_PAIRS
    return pl.pallas_call(
        functools.partial(_even_prompt_kernel, tt=tt, start=start),
        grid=(B, nt),
        in_specs=[pl.BlockSpec((1, tt, D_MODEL), lambda b, i: (0, 0, 0)),
                  pl.BlockSpec((1, tt, D_MODEL), functools.partial(_next_tile, nt=nt, n_tiles=B * nt)),
                  vec(D_MODEL),
                  pl.BlockSpec((D_MODEL, EVEN_PROJ), c2, pipeline_mode=pl.Buffered(1)),
                  pl.BlockSpec((1, POOL_BUF, POOL_DIM), bs3),
                  pl.BlockSpec((1, 1, RWKV_PROJ), bs3),
                  pl.BlockSpec((1, RWKV_HEADS, HEAD_DIM, HEAD_DIM), bs4),
                  vec(RWKV_PROJ), vec(RWKV_DIM), pl.BlockSpec((LORA_PAD, RWKV_DIM), c2), vec(RWKV_DIM),
                  pl.BlockSpec((LORA_PAD, RWKV_DIM), c2), pl.BlockSpec((LORA_PAD, RWKV_DIM), c2),
                  vec(RWKV_DIM), vec(RWKV_DIM), vec(RWKV_DIM), vec(RWKV_DIM), vec(RWKV_DIM),
                  pl.BlockSpec((SEG_TILE, SEG_TILE), c2), pl.BlockSpec((POOL_DIM, POOL_DIM), c2),
                  vec(POOL_DIM), pl.BlockSpec((WKV_CHUNK, WKV_CHUNK), c2)],
        out_specs=[pl.BlockSpec((1, tt, D_MODEL), bt),
                   pl.BlockSpec((1, POOL_BUF, POOL_DIM), bs3),
                   pl.BlockSpec((1, 1, RWKV_PROJ), bs3),
                   pl.BlockSpec((1, RWKV_HEADS, HEAD_DIM, HEAD_DIM), bs4)],
        out_shape=[jax.ShapeDtypeStruct((B, T, D_MODEL), BF16),
                   jax.ShapeDtypeStruct((B, POOL_BUF, POOL_DIM), F32),
                   jax.ShapeDtypeStruct((B, 1, RWKV_PROJ), F32),
                   jax.ShapeDtypeStruct((B, RWKV_HEADS, HEAD_DIM, HEAD_DIM), F32)],
        scratch_shapes=[pltpu.VMEM((tt, EVEN_PROJ), F32),
                        pltpu.VMEM((16, POOL_DIM), F32), pltpu.VMEM((1, RWKV_PROJ), F32),
                        pltpu.VMEM((HEAD_PAIRS, PAIR_DIM, PAIR_DIM), F32),
                        scr(), scr(), scr(), scr(), scr(), scr(), scr(),
                        pltpu.VMEM((n_units, 2 * PAIR_DIM, PAIR_DIM), BF16),
                        pltpu.VMEM((n_units, 2 * PAIR_DIM, PAIR_DIM), F32),
                        pltpu.VMEM((n_units, PAIR_DIM, PAIR_DIM), F32),
                        pltpu.VMEM((n_units, PAIR_DIM, PAIR_DIM), BF16),
                        pltpu.VMEM((n_units, PAIR_DIM, PAIR_DIM), F32)],
        compiler_params=pltpu.CompilerParams(dimension_semantics=("arbitrary", "arbitrary"),
                                             vmem_limit_bytes=VMEM_LIMIT_BIG),
        name="even_prompt",
    )(x, x, g_in, w_in, st_pool, st_shift, st_wkv, *prm)


def _odd_prompt_kernel(x0_ref, xn_ref, gin_ref, win_ref, stc_ref, stl_ref,
                       lng_ref, lnb_ref, ws_ref, bias_ref, cw_ref, cb_ref, wx_ref, bx_ref, wa_ref,
                       ba_ref, lam_ref, y_ref, oconv_ref, olru_ref, q_s, hconv, hl, mix_s, *, tt):
    i = pl.program_id(1)
    nt = pl.num_programs(1)

    @pl.when(jnp.logical_and(i == 0, pl.program_id(0) == 0))
    def _first_projection():
        h0 = _rmsnorm(x0_ref[0], gin_ref[...]).astype(BF16)
        q_s[...] = jnp.dot(h0, win_ref[...], preferred_element_type=F32)

    @pl.when(i == 0)
    def _init():
        hconv[0:5, :] = jnp.zeros((5, LRU_DIM), F32)
        hconv[5:8, :] = stc_ref[0]
        hl[...] = stl_ref[0]

    q = q_s[...]

    def project_next():
        hb = _rmsnorm(xn_ref[0], gin_ref[...]).astype(BF16)
        yield
        for c0 in range(0, ODD_PROJ, PROJ_PIECE):
            q_s[:, c0:c0 + PROJ_PIECE] = jnp.dot(hb, win_ref[:, c0:c0 + PROJ_PIECE],
                                                 preferred_element_type=F32)
            yield

    def mixers():
        u, vn = _gmlp_pre(q[:, 0:2 * GMLP_DIM], lng_ref[...], lnb_ref[...])
        yield
        rr = lax.broadcasted_iota(jnp.int32, (CHUNK, CHUNK), 0)
        cc = lax.broadcasted_iota(jnp.int32, (CHUNK, CHUNK), 1)
        causal = cc <= rr
        for h in range(GMLP_HEADS):
            wm = jnp.where(causal, ws_ref[h], 0.0).astype(BF16)
            ls = slice(h * CHUNK, (h + 1) * CHUNK)
            for c in range(tt // CHUNK):
                rs = slice(c * CHUNK, (c + 1) * CHUNK)
                mix_s[rs, ls] = (jnp.dot(wm, vn[rs, ls].astype(BF16), preferred_element_type=F32)
                                 + bias_ref[:, ls])
        y_ref[0, :, 0:GMLP_DIM] = (u * mix_s[...]).astype(BF16)
        yield

        gate = _gelu(q[:, 2 * GMLP_DIM:2 * GMLP_DIM + LRU_DIM])
        yield
        xr = q[:, 2 * GMLP_DIM + LRU_DIM:ODD_PROJ]
        hconv[8:8 + tt, :] = xr
        xc = xr * cw_ref[3:4, :] + cb_ref[...]
        for j in range(1, CONV_WIDTH):
            xc = xc + hconv[8 - j:8 - j + tt, :] * cw_ref[3 - j:4 - j, :]
        hconv[0:8, :] = hconv[tt:tt + 8, :]
        yield
        a, b = _lru_gates(xc, wx_ref[...], bx_ref[...], wa_ref[...], ba_ref[...], lam_ref[...])
        yield
        n_groups = tt // SCAN_GROUP
        a = a.reshape(n_groups, SCAN_GROUP, LRU_DIM)
        b = b.reshape(n_groups, SCAN_GROUP, LRU_DIM)
        in_group = lax.broadcasted_iota(jnp.int32, (1, SCAN_GROUP, 1), 1)
        dist = 1
        while dist < SCAN_GROUP:
            keep = in_group >= dist
            a_sh = jnp.where(keep, pltpu.roll(a, dist, 1), 1.0)
            b_sh = jnp.where(keep, pltpu.roll(b, dist, 1), 0.0)
            b = a * b_sh + b
            a = a * a_sh
            dist *= 2
            yield
        carry = hl[...]
        groups = []
        for gi in range(n_groups):
            hg = a[gi] * carry + b[gi]
            groups.append(hg)
            carry = hg[SCAN_GROUP - 1:SCAN_GROUP, :]
        h = jnp.concatenate(groups, axis=0)
        hl[...] = carry
        y_ref[0, :, GMLP_DIM:D_MODEL] = (h * gate).astype(BF16)
        yield

    _run_interleaved(mixers(), project_next())

    @pl.when(i == nt - 1)
    def _fin():
        oconv_ref[0] = hconv[5:8, :]
        olru_ref[0] = hl[...]


def _odd_prompt(x, g_in, w_in, st_conv, st_lru, prm):
    B, T, _ = x.shape
    tt = MIXER_TILE
    nt = T // tt
    bt = lambda b, i: (b, i, 0)
    bs3 = lambda b, i: (b, 0, 0)
    c2 = lambda b, i: (0, 0)
    c3 = lambda b, i: (0, 0, 0)
    vec = lambda n: pl.BlockSpec((1, n), c2)
    return pl.pallas_call(
        functools.partial(_odd_prompt_kernel, tt=tt),
        grid=(B, nt),
        in_specs=[pl.BlockSpec((1, tt, D_MODEL), lambda b, i: (0, 0, 0)),
                  pl.BlockSpec((1, tt, D_MODEL), functools.partial(_next_tile, nt=nt, n_tiles=B * nt)),
                  vec(D_MODEL),
                  pl.BlockSpec((D_MODEL, ODD_PROJ), c2, pipeline_mode=pl.Buffered(1)),
                  pl.BlockSpec((1, CONV_WIDTH - 1, LRU_DIM), bs3),
                  pl.BlockSpec((1, 1, LRU_DIM), bs3),
                  vec(GMLP_DIM), vec(GMLP_DIM),
                  pl.BlockSpec((GMLP_HEADS, CHUNK, CHUNK), c3),
                  pl.BlockSpec((CHUNK, GMLP_DIM), c2),
                  pl.BlockSpec((CONV_WIDTH, LRU_DIM), c2), vec(LRU_DIM),
                  pl.BlockSpec((LRU_DIM, LRU_DIM), c2), vec(LRU_DIM),
                  pl.BlockSpec((LRU_DIM, LRU_DIM), c2), vec(LRU_DIM), vec(LRU_DIM)],
        out_specs=[pl.BlockSpec((1, tt, D_MODEL), bt),
                   pl.BlockSpec((1, CONV_WIDTH - 1, LRU_DIM), bs3),
                   pl.BlockSpec((1, 1, LRU_DIM), bs3)],
        out_shape=[jax.ShapeDtypeStruct((B, T, D_MODEL), BF16),
                   jax.ShapeDtypeStruct((B, CONV_WIDTH - 1, LRU_DIM), F32),
                   jax.ShapeDtypeStruct((B, 1, LRU_DIM), F32)],
        scratch_shapes=[pltpu.VMEM((tt, ODD_PROJ), F32),
                        pltpu.VMEM((8 + tt, LRU_DIM), F32), pltpu.VMEM((1, LRU_DIM), F32),
                        pltpu.VMEM((tt, GMLP_DIM), F32)],
        compiler_params=pltpu.CompilerParams(dimension_semantics=("arbitrary", "arbitrary"),
                                             vmem_limit_bytes=VMEM_LIMIT),
        name="odd_prompt",
    )(x, x, g_in, w_in, st_conv, st_lru, *prm)


def _even_sample_pre_kernel(p_ref, stp_ref, sts_ref,
                            mu_ref, w0_ref, wdec_ref, a0_ref, wa_ref, gw2_ref, kk_ref, ka_ref, rk_ref,
                            eseg_ref, poolw_ref, pools_ref,
                            r_ref, w_ref, kkn_ref, kka_ref, kp_ref, v_ref, g_ref, bonus_ref, ya_ref,
                            opool_ref, oshift_ref, *, T, start):
    prm = (mu_ref[...], w0_ref[...], wdec_ref[...], a0_ref[...], wa_ref[...], gw2_ref[...],
           kk_ref[...], ka_ref[...], rk_ref[...], eseg_ref[...])
    full = [stp_ref[s] for s in range(POOL_BUF)] + [p_ref[t][:, 0:POOL_DIM] for t in range(T)]
    wl = _pool_window_lanes()
    for t in range(T):
        P = p_ref[t][:, POOL_DIM:EVEN_PROJ]
        Pprev = sts_ref[...] if t == 0 else p_ref[t - 1][:, POOL_DIM:EVEN_PROJ]
        r, kp, v, ld, kk, a, g, bonus = _rwkv_pointwise(P, Pprev, prm)
        r_ref[t] = jnp.transpose(r)
        w_ref[t] = jnp.transpose(jnp.exp(ld))
        kkn_ref[t] = jnp.transpose(kk)
        kka_ref[t] = jnp.transpose(kk * a)
        kp_ref[t] = jnp.transpose(kp)
        v_ref[t] = jnp.transpose(v)
        g_ref[t] = g
        bonus_ref[t] = bonus
        e = POOL_BUF + t
        s2 = full[e] + full[e - 1]
        s4 = s2 + full[e - 2] + full[e - 3]
        s8 = s4 + full[e - 4] + full[e - 5] + full[e - 6] + full[e - 7]
        s16 = s8
        for s in range(8, 16):
            s16 = s16 + full[e - s]
        sel = _pool_lane_select(s2, s4, s8, s16)
        cnt = jnp.minimum(wl, start + t + 1).astype(F32)
        d = sel / cnt - full[e]
        ya_ref[t] = _bdot(d, poolw_ref[...]) * pools_ref[...]
    for s in range(POOL_BUF):
        opool_ref[s] = full[T + s]
    oshift_ref[...] = p_ref[T - 1][:, POOL_DIM:EVEN_PROJ]


def _even_sample_pre(p, st_pool, st_shift, prm, start):
    T, B, _ = p.shape
    cm = jax.ShapeDtypeStruct((T, RWKV_DIM, B), F32)
    bm = jax.ShapeDtypeStruct((T, B, RWKV_DIM), F32)
    return pl.pallas_call(
        functools.partial(_even_sample_pre_kernel, T=T, start=start),
        out_shape=[cm] * 6 + [bm] * 2 + [jax.ShapeDtypeStruct((T, B, POOL_DIM), F32),
                                   jax.ShapeDtypeStruct((POOL_BUF, B, POOL_DIM), F32),
                                   jax.ShapeDtypeStruct((B, RWKV_PROJ), F32)],
        compiler_params=pltpu.CompilerParams(vmem_limit_bytes=VMEM_LIMIT),
        name="even_sample_pre",
    )(p, st_pool, st_shift, *prm)


def _wkv_sample_kernel(r_ref, w_ref, kk_ref, kka_ref, kp_ref, v_ref, s_ref, o_ref, so_ref, *, T):
    group = range(WKV_SAMPLE_GROUP)

    def body(ib, carry):
        v0 = pl.multiple_of(ib * WKV_SAMPLE_GROUP, WKV_SAMPLE_GROUP)
        blk = pl.ds(v0, WKV_SAMPLE_GROUP)
        S = [s_ref[0, v0 + u] for u in group]
        for t in range(T):
            kk, w, kka, kp, r = kk_ref[t], w_ref[t], kka_ref[t], kp_ref[t], r_ref[t]
            vv = v_ref[t, blk, :]
            sk = [jnp.sum(S[u] * kk, axis=0, keepdims=True) for u in group]
            S = [S[u] * w - sk[u] * kka + vv[u:u + 1, :] * kp for u in group]
            o_ref[t, blk, :] = jnp.concatenate(
                [jnp.sum(S[u] * r, axis=0, keepdims=True) for u in group], axis=0)
        for u in group:
            so_ref[0, v0 + u] = S[u]
        return carry

    lax.fori_loop(0, HEAD_DIM // WKV_SAMPLE_GROUP, body, 0)


def _wkv_sample(r, w, kk, kka, kp, v, s):
    T, _, B = r.shape
    row_spec = pl.BlockSpec((T, HEAD_DIM, B), lambda h: (0, h, 0))
    st_spec = pl.BlockSpec((1, HEAD_DIM, HEAD_DIM, B), lambda h: (h, 0, 0, 0))
    return pl.pallas_call(
        functools.partial(_wkv_sample_kernel, T=T),
        grid=(RWKV_HEADS,),
        in_specs=[row_spec] * 6 + [st_spec],
        out_specs=[row_spec, st_spec],
        out_shape=[jax.ShapeDtypeStruct((T, RWKV_DIM, B), F32),
                   jax.ShapeDtypeStruct((RWKV_HEADS, HEAD_DIM, HEAD_DIM, B), F32)],
        compiler_params=pltpu.CompilerParams(dimension_semantics=("arbitrary",),
                                             vmem_limit_bytes=VMEM_LIMIT),
        name="wkv_sample",
    )(r, w, kk, kka, kp, v, s)


def _even_sample_post_kernel(o_ref, bonus_ref, g_ref, ya_ref, gng_ref, gnb_ref, eseg_ref, y_ref, *, T):
    for t in range(T):
        o = jnp.transpose(o_ref[t])
        yb = _rwkv_post(o, bonus_ref[t], g_ref[t], gng_ref[...], gnb_ref[...], eseg_ref[...])
        y_ref[t, :, 0:POOL_DIM] = ya_ref[t].astype(BF16)
        y_ref[t, :, POOL_DIM:D_MODEL] = yb.astype(BF16)


def _even_sample_post(o, bonus, g, ya, gn_g, gn_b, e_seg):
    T, _, B = o.shape
    return pl.pallas_call(
        functools.partial(_even_sample_post_kernel, T=T),
        out_shape=jax.ShapeDtypeStruct((T, B, D_MODEL), BF16),
        compiler_params=pltpu.CompilerParams(vmem_limit_bytes=VMEM_LIMIT),
        name="even_sample_post",
    )(o, bonus, g, ya, gn_g, gn_b, e_seg)


def _odd_sample_kernel(q_ref, stc_ref, stl_ref,
                       lng_ref, lnb_ref, wsm_ref, bsm_ref, cw_ref, cb_ref, wx_ref, bx_ref, wa_ref,
                       ba_ref, lam_ref, y_ref, v_ref, oconv_ref, olru_ref, *, T):
    vns = []
    us = []
    for t in range(T):
        u, vn = _gmlp_pre(q_ref[t][:, 0:2 * GMLP_DIM], lng_ref[...], lnb_ref[...])
        us.append(u)
        vns.append(vn)
        v_ref[t] = vn
    full = [stc_ref[s] for s in range(CONV_WIDTH - 1)] + \
           [q_ref[t][:, 2 * GMLP_DIM + LRU_DIM:ODD_PROJ] for t in range(T)]
    h = stl_ref[...]
    for t in range(T):
        mix = bsm_ref[t:t + 1, :]
        for j in range(t + 1):
            mix = mix + wsm_ref[t * T + j:t * T + j + 1, :] * vns[j]
        y_ref[t, :, 0:GMLP_DIM] = (us[t] * mix).astype(BF16)
        xc = full[t + CONV_WIDTH - 1] * cw_ref[CONV_WIDTH - 1:CONV_WIDTH, :] + cb_ref[...]
        for j in range(CONV_WIDTH - 1):
            xc = xc + full[t + j] * cw_ref[j:j + 1, :]
        a, b = _lru_gates(xc, wx_ref[...], bx_ref[...], wa_ref[...], ba_ref[...], lam_ref[...])
        h = a * h + b
        gate_in = q_ref[t][:, 2 * GMLP_DIM:2 * GMLP_DIM + LRU_DIM]
        y_ref[t, :, GMLP_DIM:D_MODEL] = (h * _gelu(gate_in)).astype(BF16)
    for s in range(CONV_WIDTH - 1):
        oconv_ref[s] = full[T + s]
    olru_ref[...] = h


def _odd_sample(q, st_conv, st_lru, prm):
    T, B, _ = q.shape
    return pl.pallas_call(
        functools.partial(_odd_sample_kernel, T=T),
        out_shape=[jax.ShapeDtypeStruct((T, B, D_MODEL), BF16),
                   jax.ShapeDtypeStruct((T, B, GMLP_DIM), F32),
                   jax.ShapeDtypeStruct((CONV_WIDTH - 1, B, LRU_DIM), F32),
                   jax.ShapeDtypeStruct((B, LRU_DIM), F32)],
        compiler_params=pltpu.CompilerParams(vmem_limit_bytes=VMEM_LIMIT),
        name="odd_sample",
    )(q, st_conv, st_lru, *prm)


def _block_diag(w):
    n, c, d = w.shape
    eye = jnp.eye(n, dtype=w.dtype)
    return (eye[:, None, :, None] * w[:, :, None, :]).reshape(n * c, n * d)


def _row(x):
    return x.reshape(1, -1)


def kernel(x_prompt, x_sample, state_pool, state_shift, state_wkv, state_conv, state_lru, ev_norm_g, ev_w_in, pool_w, pool_scale, rwkv_mu, rwkv_w0, rwkv_w_w2, rwkv_a0, rwkv_a_w2, rwkv_g_w2, rwkv_k_k, rwkv_k_a, rwkv_r_k, rwkv_gn_g, rwkv_gn_b, ev_w_out, od_norm_g, od_w_in, gmlp_ln_g, gmlp_ln_b, gmlp_ws, gmlp_bs, lru_conv_w, lru_conv_b, lru_wx, lru_bx, lru_wa, lru_ba, lru_lam, od_w_out, ff_norm_g, ff_w1, ff_w2, final_norm_g):
    B, T, _ = x_prompt.shape
    DB, DT, _ = x_sample.shape

    seg_ids = jnp.arange(SEG_TILE) // HEAD_DIM
    e_seg = (seg_ids[:, None] == seg_ids[None, :]).astype(BF16)
    tri = (jnp.arange(WKV_CHUNK)[None, :] <= jnp.arange(WKV_CHUNK)[:, None]).astype(BF16)
    zlora = jnp.zeros((LORA_PAD // 2, RWKV_DIM), F32)

    ev_common = (_row(rwkv_mu[0]), _row(rwkv_w0[0]),
                 jnp.concatenate([rwkv_w_w2[0], zlora], 0).astype(BF16), _row(rwkv_a0[0]),
                 jnp.concatenate([zlora, rwkv_a_w2[0]], 0).astype(BF16), rwkv_g_w2[0].astype(BF16),
                 _row(rwkv_k_k[0]), _row(rwkv_k_a[0]), _row(rwkv_r_k[0]))
    gn_g, gn_b = _row(rwkv_gn_g[0]), _row(rwkv_gn_b[0])
    pool_bd = _block_diag(pool_w[0]).astype(BF16)
    pool_sc = _row(pool_scale[0])
    w_in0 = ev_w_in[0].astype(BF16)
    g_in0 = _row(ev_norm_g[0])

    xp = x_prompt.reshape(B * T, D_MODEL)
    xs = jnp.transpose(x_sample, (1, 0, 2)).reshape(DT * DB, D_MODEL)

    ps = _inproj(xs, g_in0, w_in0).reshape(DT, DB, EVEN_PROJ)

    yp, p_pool, p_shift, p_wkv = _even_prompt(
        x_prompt, g_in0, w_in0,
        jnp.zeros((B, POOL_BUF, POOL_DIM), F32), jnp.zeros((B, 1, RWKV_PROJ), F32),
        jnp.zeros((B, RWKV_HEADS, HEAD_DIM, HEAD_DIM), F32),
        ev_common + (gn_g, gn_b, e_seg, pool_bd, pool_sc, tri), 0)

    pre = _even_sample_pre(ps, jnp.transpose(state_pool[0], (1, 0, 2)), state_shift[0],
                           ev_common + (e_seg, pool_bd, pool_sc), PAST_LEN)
    r_s, w_s, kk_s, kka_s, kp_s, v_s, g_s, bonus_s, ya_s, s_pool_tm, s_shift = pre
    o_s, s_wkv_bl = _wkv_sample(r_s, w_s, kk_s, kka_s, kp_s, v_s,
                                jnp.transpose(state_wkv[0], (1, 2, 3, 0)))
    ys = _even_sample_post(o_s, bonus_s, g_s, ya_s, gn_g, gn_b, e_seg)

    w_out0 = ev_w_out[0].astype(BF16)
    ffg = lambda l: _row(ff_norm_g[l])
    gfin = _row(final_norm_g)
    ff_w1_b, ff_w2_b = ff_w1.astype(BF16), ff_w2.astype(BF16)
    xp, xs = _ffn(xp, yp.reshape(B * T, D_MODEL), xs, ys.reshape(DT * DB, D_MODEL),
                  w_out0, ffg(0), ff_w1_b, ff_w2_b, gfin, 0, False)

    w_in1 = od_w_in[0].astype(BF16)
    g_in1 = _row(od_norm_g[0])
    qs = _inproj(xs, g_in1, w_in1).reshape(DT, DB, ODD_PROJ)

    lru_common = (lru_conv_w[0], _row(lru_conv_b[0]), _block_diag(lru_wx[0]).astype(BF16), _row(lru_bx[0]),
                  _block_diag(lru_wa[0]).astype(BF16), _row(lru_ba[0]), _row(lru_lam[0]))
    ln = (_row(gmlp_ln_g[0]), _row(gmlp_ln_b[0]))
    bias_full = jnp.repeat(jnp.transpose(gmlp_bs[0]), CHUNK, axis=1)
    yp, p_conv, p_lru = _odd_prompt(
        xp.reshape(B, T, D_MODEL), g_in1, w_in1,
        jnp.zeros((B, CONV_WIDTH - 1, LRU_DIM), F32), jnp.zeros((B, 1, LRU_DIM), F32),
        ln + (gmlp_ws[0], bias_full) + lru_common)

    ws_small = jnp.repeat(jnp.transpose(gmlp_ws[0][:, :DT, :DT], (1, 2, 0)).reshape(DT * DT, GMLP_HEADS),
                          CHUNK, axis=1)
    ys, s_v, s_conv_tm, s_lru = _odd_sample(
        qs, jnp.transpose(state_conv[0], (1, 0, 2)), state_lru[0],
        ln + (ws_small, bias_full[:DT]) + lru_common)

    w_out1 = od_w_out[0].astype(BF16)
    xp, xs = _ffn(xp, yp.reshape(B * T, D_MODEL), xs, ys.reshape(DT * DB, D_MODEL),
                  w_out1, ffg(1), ff_w1_b, ff_w2_b, gfin, 1, True)

    tm2bm = lambda t: jnp.transpose(t, (1, 0, 2))
    y_prompt = xp.reshape(B, T, D_MODEL)
    y_sample = tm2bm(xs.reshape(DT, DB, D_MODEL))
    return (y_prompt, y_sample,
            p_pool[None], p_shift.reshape(1, B, RWKV_PROJ), p_wkv[None],
            p_conv[None], p_lru.reshape(1, B, LRU_DIM),
            tm2bm(s_pool_tm)[None], s_shift[None],
            jnp.transpose(s_wkv_bl, (3, 0, 1, 2))[None],
            tm2bm(s_conv_tm)[None], s_lru[None], tm2bm(s_v)[None])
```

```python
import functools

import jax
import jax.numpy as jnp
from jax import lax
from jax.experimental import pallas as pl
from jax.experimental.pallas import tpu as pltpu

F32 = jnp.float32
BF16 = jnp.bfloat16

D_MODEL = 1024
NORM_EPS = 1e-6
D_FF = 4 * D_MODEL

POOL_WINDOWS = (2, 4, 8, 16)
POOL_GROUP_DIM = 64
POOL_DIM = 256
POOL_BUF = 15

HEAD_DIM = 64
RWKV_DIM = 768
RWKV_HEADS = 12
HEAD_PAIRS = RWKV_HEADS // 2
PAIR_DIM = 2 * HEAD_DIM
RWKV_PROJ = 2560
RWKV_GN_EPS = 64e-5
EXP_NEG_HALF = 0.6065306597126334
EVEN_PROJ = POOL_DIM + RWKV_PROJ
LORA_OFF = 3 * RWKV_DIM
LORA_PAD = 128
GATE_OFF = LORA_OFF + LORA_PAD

CHUNK = 128
GMLP_DIM = 512
GMLP_HEADS = 4
LN_EPS = 1e-5
GELU_C = 0.7978845608028654
LRU_DIM = 512
CONV_WIDTH = 4
LRU_C = 8.0
ODD_PROJ = 2048

WKV_CHUNK = 64
SEG_TILE = 256
SCAN_GROUP = 8
PROJ_PIECE = 256
WKV_PREP_CHUNKS = 2
WKV_SAMPLE_GROUP = 8

PAST_LEN = 16384
MIXER_TILE = 512
DENSE_TILE = 1024

V7X_VMEM_BYTES = 64 * 1024 * 1024
VMEM_LIMIT = V7X_VMEM_BYTES * 3 // 4
VMEM_LIMIT_BIG = V7X_VMEM_BYTES * 7 // 8


def _bdot(a, b):
    return jnp.dot(a.astype(BF16), b.astype(BF16), preferred_element_type=F32)


def _bdot_nt(a, b):
    return lax.dot_general(a.astype(BF16), b.astype(BF16), (((1,), (1,)), ((), ())),
                           preferred_element_type=F32)


def _split3(x):
    hi = x.astype(BF16)
    r1 = x - hi.astype(F32)
    mid = r1.astype(BF16)
    lo = (r1 - mid.astype(F32)).astype(BF16)
    return hi, mid, lo


def _exact_dot_rhs01(x, e):
    hi = x.astype(BF16)
    lo = (x - hi.astype(F32)).astype(BF16)
    d = lambda t: jnp.dot(t, e, preferred_element_type=F32)
    return d(hi) + d(lo)


def _exact_dot_lhs01(e, x):
    hi, mid, lo = _split3(x)
    d = lambda t: jnp.dot(e, t, preferred_element_type=F32)
    return d(hi) + d(mid) + d(lo)


def _segsum(x, e_seg):
    parts = [_exact_dot_rhs01(x[:, g * SEG_TILE:(g + 1) * SEG_TILE], e_seg)
             for g in range(RWKV_DIM // SEG_TILE)]
    return jnp.concatenate(parts, axis=1)


def _softplus(z):
    return jnp.maximum(z, 0.0) + jnp.log(1.0 + jnp.exp(-jnp.abs(z)))


def _sigmoid(z):
    return 0.5 * jnp.tanh(0.5 * z) + 0.5


def _gelu(z):
    hz = 0.5 * z
    return hz + hz * jnp.tanh(z * (GELU_C + (GELU_C * 0.044715) * (z * z)))


def _rmsnorm(x, g):
    ms = jnp.mean(x * x, axis=-1, keepdims=True)
    return x * lax.rsqrt(ms + NORM_EPS) * g


def _next_tile(b, i, *, nt, n_tiles):
    n = jnp.minimum(b * nt + i + 1, n_tiles - 1)
    return (n // nt, n % nt, 0)


def _run_interleaved(*stages):
    live = list(stages)
    while live:
        for gen in list(live):
            if next(gen, StopIteration) is StopIteration:
                live.remove(gen)


_POINTWISE_KEYS = ("r", "kp", "v", "ld", "kk", "a", "g", "bonus")


def _rwkv_pointwise_stages(P, Pprev, prm, out):
    (mu, w0, wdec, a0, wa, gw2, k_k, k_a, r_k, e_seg) = prm
    xs = P + (Pprev - P) * mu
    r = xs[:, 0:RWKV_DIM]
    k = xs[:, RWKV_DIM:2 * RWKV_DIM]
    v = xs[:, 2 * RWKV_DIM:3 * RWKV_DIM]
    c_wa = xs[:, LORA_OFF:GATE_OFF]
    cg = xs[:, GATE_OFF:RWKV_PROJ]
    yield
    ld = -EXP_NEG_HALF * _sigmoid(w0 + _bdot(jnp.tanh(c_wa), wdec))
    yield
    a = _sigmoid(a0 + _bdot(c_wa, wa))
    yield
    g = _bdot(_sigmoid(cg), gw2)
    yield
    kk = k * k_k
    kk = kk * lax.rsqrt(jnp.maximum(_segsum(kk * kk, e_seg), 1e-24))
    yield
    kp = k * (1.0 + (a - 1.0) * k_a)
    yield
    bonus = _segsum(r * kp * r_k, e_seg) * v
    out.update(r=r, kp=kp, v=v, ld=ld, kk=kk, a=a, g=g, bonus=bonus)
    yield


def _rwkv_pointwise(P, Pprev, prm):
    out = {}
    for _ in _rwkv_pointwise_stages(P, Pprev, prm, out):
        pass
    return tuple(out[key] for key in _POINTWISE_KEYS)


def _rwkv_post(o, bonus, g, gn_g, gn_b, e_seg):
    m = _segsum(o, e_seg) * (1.0 / HEAD_DIM)
    d = o - m
    var = _segsum(d * d, e_seg) * (1.0 / HEAD_DIM)
    on = d * lax.rsqrt(var + RWKV_GN_EPS) * gn_g + gn_b
    return (on + bonus) * g


def _pool_by_group(per_group):
    lane = lax.broadcasted_iota(jnp.int32, (1, POOL_DIM), 1)
    out = per_group[-1]
    for i in range(len(per_group) - 2, -1, -1):
        out = jnp.where(lane < (i + 1) * POOL_GROUP_DIM, per_group[i], out)
    return out


def _pool_lane_select(s2, s4, s8, s16):
    return _pool_by_group((s2, s4, s8, s16))


def _pool_window_lanes():
    return _pool_by_group(POOL_WINDOWS)


def _lru_gates(xc, wx, bx, wa, ba, lam):
    gx = _sigmoid(_bdot(xc, wx) + bx)
    ga = _sigmoid(_bdot(xc, wa) + ba)
    log_a = -LRU_C * ga * _softplus(-lam)
    a = jnp.exp(log_a)
    b = jnp.sqrt(-jnp.tanh(log_a) * (a * a + 1.0)) * gx * xc
    return a, b


def _gmlp_pre(zq, ln_g, ln_b):
    z = _gelu(zq)
    u = z[:, :GMLP_DIM]
    v = z[:, GMLP_DIM:]
    m = jnp.mean(v, axis=-1, keepdims=True)
    d = v - m
    var = jnp.mean(d * d, axis=-1, keepdims=True)
    return u, d * lax.rsqrt(var + LN_EPS) * ln_g + ln_b


def _ffn_kernel(x_ref, y_ref, wo_ref, g_ref, w1_ref, w2_ref, gf_ref, o_ref, *, final):
    x1 = x_ref[...] + jnp.dot(y_ref[...], wo_ref[...], preferred_element_type=F32)
    hf = _rmsnorm(x1, g_ref[...]).astype(BF16)
    acc = x1
    fc = 1024
    for c in range(D_FF // fc):
        h = jnp.dot(hf, w1_ref[:, c * fc:(c + 1) * fc], preferred_element_type=F32)
        h = jnp.square(jnp.maximum(h, 0.0)).astype(BF16)
        acc = acc + jnp.dot(h, w2_ref[c * fc:(c + 1) * fc, :], preferred_element_type=F32)
    if final:
        acc = _rmsnorm(acc, gf_ref[...])
    o_ref[...] = acc


def _ffn(x, y, wo, g, w1, w2, gf, layer, final):
    m = x.shape[0]
    tm = min(DENSE_TILE, m)
    const = lambda i: (0, 0)
    pick = lambda i: (layer, 0, 0)
    return pl.pallas_call(
        functools.partial(_ffn_kernel, final=final),
        grid=(m // tm,),
        in_specs=[pl.BlockSpec((tm, D_MODEL), lambda i: (i, 0)),
                  pl.BlockSpec((tm, D_MODEL), lambda i: (i, 0)),
                  pl.BlockSpec((D_MODEL, D_MODEL), const, pipeline_mode=pl.Buffered(1)),
                  pl.BlockSpec((1, D_MODEL), const),
                  pl.BlockSpec((None, D_MODEL, D_FF), pick, pipeline_mode=pl.Buffered(1)),
                  pl.BlockSpec((None, D_FF, D_MODEL), pick, pipeline_mode=pl.Buffered(1)),
                  pl.BlockSpec((1, D_MODEL), const)],
        out_specs=pl.BlockSpec((tm, D_MODEL), lambda i: (i, 0)),
        out_shape=jax.ShapeDtypeStruct((m, D_MODEL), F32),
        compiler_params=pltpu.CompilerParams(
            dimension_semantics=("arbitrary",),
            vmem_limit_bytes=VMEM_LIMIT_BIG if tm == DENSE_TILE else VMEM_LIMIT),
        name="ffn",
    )(x, y, wo, g, w1, w2, gf)


def _even_prompt_kernel(x0_ref, xn_ref, gin_ref, win_ref, stp_ref, sts_ref, stw_ref,
                        mu_ref, w0_ref, wdec_ref, a0_ref, wa_ref, gw2_ref, kk_ref, ka_ref, rk_ref,
                        gng_ref, gnb_ref, eseg_ref, poolw_ref, pools_ref, tri_ref,
                        y_ref, opool_ref, oshift_ref, owkv_ref,
                        p_s, hpool, hshift, S, r_s, kp_s, v_s, ld_s, kk_s, a_s, o_s,
                        lhs_b, add_b, vk_b, bend_b, pend_b, *, tt, start):
    i = pl.program_id(1)
    nt = pl.num_programs(1)
    C = WKV_CHUNK

    @pl.when(jnp.logical_and(i == 0, pl.program_id(0) == 0))
    def _first_projection():
        h0 = _rmsnorm(x0_ref[0], gin_ref[...]).astype(BF16)
        p_s[...] = jnp.dot(h0, win_ref[...], preferred_element_type=F32)

    @pl.when(i == 0)
    def _init():
        hpool[0:1, :] = jnp.zeros((1, POOL_DIM), F32)
        hpool[1:16, :] = stp_ref[0]
        hshift[...] = sts_ref[0]
        S[...] = jnp.zeros(S.shape, F32)
        for j in range(HEAD_PAIRS):
            S[j, 0:HEAD_DIM, 0:HEAD_DIM] = stw_ref[0, 2 * j]
            S[j, HEAD_DIM:PAIR_DIM, HEAD_DIM:PAIR_DIM] = stw_ref[0, 2 * j + 1]
        for j in range(HEAD_PAIRS):
            S[j] = jnp.transpose(S[j])

    p = p_s[...]
    rows = lax.broadcasted_iota(jnp.int32, (tt, 1), 0)
    e_seg = eseg_ref[...]
    prm = (mu_ref[...], w0_ref[...], wdec_ref[...], a0_ref[...], wa_ref[...], gw2_ref[...],
           kk_ref[...], ka_ref[...], rk_ref[...], e_seg)
    pw = {}

    def project_next():
        hb = _rmsnorm(xn_ref[0], gin_ref[...]).astype(BF16)
        yield
        for c0 in range(0, EVEN_PROJ, PROJ_PIECE):
            p_s[:, c0:c0 + PROJ_PIECE] = jnp.dot(hb, win_ref[:, c0:c0 + PROJ_PIECE],
                                                 preferred_element_type=F32)
            yield

    def pointwise():
        u = p[:, 0:POOL_DIM]
        ext = jnp.concatenate([hpool[...], u], axis=0)
        s2 = ext + pltpu.roll(ext, 1, 0)
        s4 = s2 + pltpu.roll(s2, 2, 0)
        s8 = s4 + pltpu.roll(s4, 4, 0)
        s16 = s8 + pltpu.roll(s8, 8, 0)
        sel = _pool_lane_select(s2, s4, s8, s16)[16:, :]
        pos = start + i * tt + rows
        cnt = jnp.minimum(_pool_window_lanes(), pos + 1).astype(F32)
        d = sel / cnt - u
        y_ref[0, :, 0:POOL_DIM] = (_bdot(d, poolw_ref[...]) * pools_ref[...]).astype(BF16)
        hpool[...] = ext[tt:tt + 16, :]
        yield
        P = p[:, POOL_DIM:EVEN_PROJ]
        Pprev = jnp.where(rows == 0, hshift[...], pltpu.roll(P, 1, 0))
        hshift[...] = P[tt - 1:tt, :]
        yield
        yield from _rwkv_pointwise_stages(P, Pprev, prm, pw)
        r_s[...] = pw["r"]
        kp_s[...] = pw["kp"]
        v_s[...] = pw["v"]
        ld_s[...] = pw["ld"]
        kk_s[...] = pw["kk"]
        a_s[...] = pw["a"]
        yield

    _run_interleaved(pointwise(), project_next())
    g, bonus = pw["g"], pw["bonus"]

    lane_c = lax.broadcasted_iota(jnp.int32, (C, PAIR_DIM), 1)
    row_c = lax.broadcasted_iota(jnp.int32, (C, PAIR_DIM), 0)
    head0 = lane_c < HEAD_DIM
    left = lane_c < C
    lo_strict = left & (lane_c < row_c)
    lo_incl = left & (lane_c <= row_c)
    hi_strict = jnp.logical_not(left) & (lane_c - C < row_c)
    hi_incl = jnp.logical_not(left) & (lane_c - C <= row_c)
    eye_r = (lane_c - C == row_c).astype(F32)
    zb = jnp.zeros((C, PAIR_DIM), BF16)
    zbw = jnp.zeros((C, 2 * PAIR_DIM), BF16)
    tri = tri_ref[...]
    pairs = range(HEAD_PAIRS)

    def stack_heads(x):
        z = jnp.zeros_like(x)
        return jnp.concatenate([jnp.where(head0, x, z), jnp.where(head0, z, x)], axis=0)

    def prepare(cg):
        qa_sm, qr_sm, v_sm, rhs_g, kb_src, slot = [], [], [], [], [], []
        for ci in range(WKV_PREP_CHUNKS):
            c = cg * WKV_PREP_CHUNKS + ci
            sl = slice(c * C, (c + 1) * C)
            R = r_s[sl, :]
            K = kp_s[sl, :]
            V = v_s[sl, :]
            LD = ld_s[sl, :]
            KK = kk_s[sl, :]
            KA = KK * a_s[sl, :]
            L = _exact_dot_lhs01(tri, LD)
            Lend = L[C - 1:C, :]
            enL = jnp.exp(-L)
            eE = jnp.exp(Lend - L)
            Qr = R * jnp.exp(L)
            Qa = KK * jnp.exp(L - LD)
            Kt = K * enL
            Bt = KA * enL
            Kend = K * eE
            Bend = KA * eE
            Pend = jnp.exp(Lend)
            for j in pairs:
                ls = slice(j * PAIR_DIM, (j + 1) * PAIR_DIM)
                qa_sm.append(stack_heads(Qa[:, ls]))
                qr_sm.append(stack_heads(Qr[:, ls]))
                v_sm.append(stack_heads(V[:, ls]).astype(BF16))
                bt = Bt[:, ls].astype(BF16)
                kt = Kt[:, ls].astype(BF16)
                rhs_g.append(jnp.concatenate([bt, kt], axis=0))
                kb_src.append((Kend[:, ls], Bend[:, ls], Pend[:, ls]))
                slot.append(c * HEAD_PAIRS + j)
        units = range(len(slot))
        heads = [(u, h) for u in units for h in range(2)]
        hrows = lambda x, h: x[h * C:(h + 1) * C]
        G = [_bdot_nt(jnp.concatenate([qa_sm[u], qr_sm[u]], axis=0), rhs_g[u]) for u in units]
        yield
        GA = [hrows(G[u], h) for u, h in heads]
        GR = [hrows(G[u], 2 + h) for u, h in heads]
        R = [jnp.where(lo_strict, -GA[k], eye_r) for k in range(len(heads))]
        level = 1
        while level < C:
            for k in range(len(heads)):
                rb = R[k].astype(BF16)
                P2 = jnp.dot(rb, jnp.concatenate([rb, zb], axis=0),
                             preferred_element_type=F32)
                R[k] = P2 + jnp.where(left, 0.0, R[k])
            level *= 2
            yield
        AV = []
        for k, (u, h) in enumerate(heads):
            lhs = jnp.concatenate([jnp.where(hi_strict, GA[k], 0.0), jnp.where(hi_incl, GR[k], 0.0)], axis=0)
            AV.append(jnp.dot(lhs.astype(BF16), jnp.concatenate([zb, hrows(v_sm[u], h)], axis=0),
                              preferred_element_type=F32))
        yield
        TQ = []
        for k, (u, h) in enumerate(heads):
            rhs = jnp.concatenate([hrows(qa_sm[u], h), AV[k][0:C]], axis=1).astype(BF16)
            TQ.append(jnp.dot(R[k].astype(BF16), jnp.concatenate([zbw, rhs], axis=0),
                              preferred_element_type=F32))
        yield
        AT = []
        for k, (u, h) in enumerate(heads):
            AT.append(jnp.dot(jnp.where(lo_incl, GR[k], 0.0).astype(BF16),
                              jnp.concatenate([TQ[k].astype(BF16), zbw], axis=0), preferred_element_type=F32))
        yield
        for u in units:
            kend, bend, pend = kb_src[u]
            kb_t = jnp.transpose(jnp.concatenate(
                [stack_heads(kend), stack_heads(bend), jnp.broadcast_to(pend, (PAIR_DIM, PAIR_DIM))],
                axis=1))
            for h in range(2):
                k = 2 * u + h
                lhs_b[slot[u], h * C:(h + 1) * C, :] = TQ[k][:, 0:PAIR_DIM].astype(BF16)
                lhs_b[slot[u], (2 + h) * C:(3 + h) * C, :] = (
                    hrows(qr_sm[u], h) - AT[k][:, 0:PAIR_DIM]).astype(BF16)
                add_b[slot[u], h * C:(h + 1) * C, :] = TQ[k][:, PAIR_DIM:2 * PAIR_DIM]
                add_b[slot[u], (2 + h) * C:(3 + h) * C, :] = AV[k][C:2 * C] - AT[k][:, PAIR_DIM:2 * PAIR_DIM]
            vk_b[slot[u]] = _bdot(kb_t[0:PAIR_DIM], v_sm[u])
            bend_b[slot[u]] = kb_t[PAIR_DIM:2 * PAIR_DIM].astype(BF16)
            pend_b[slot[u]] = kb_t[2 * PAIR_DIM:3 * PAIR_DIM]

    def advance(cg):
        for ci in range(WKV_PREP_CHUNKS):
            c = cg * WKV_PREP_CHUNKS + ci
            UO = [jnp.dot(lhs_b[c * HEAD_PAIRS + j], S[j].astype(BF16), preferred_element_type=F32)
                  + add_b[c * HEAD_PAIRS + j] for j in pairs]
            yield
            for j in pairs:
                u = c * HEAD_PAIRS + j
                S[j] = pend_b[u] * S[j] + vk_b[u] - jnp.dot(bend_b[u], UO[j][0:2 * C].astype(BF16),
                                                            preferred_element_type=F32)
                o_s[c * C:(c + 1) * C, j * PAIR_DIM:(j + 1) * PAIR_DIM] = UO[j][2 * C:3 * C] + UO[j][3 * C:4 * C]
            yield

    n_groups = tt // (C * WKV_PREP_CHUNKS)
    _run_interleaved(prepare(0))
    for cg in range(1, n_groups):
        _run_interleaved(prepare(cg), advance(cg - 1))
    _run_interleaved(advance(n_groups - 1))

    yb = _rwkv_post(o_s[...], bonus, g, gng_ref[...], gnb_ref[...], e_seg)
    y_ref[0, :, POOL_DIM:D_MODEL] = yb.astype(BF16)

    @pl.when(i == nt - 1)
    def _fin():
        opool_ref[0] = hpool[1:16, :]
        oshift_ref[0] = hshift[...]
        for j in range(HEAD_PAIRS):
            S[j] = jnp.transpose(S[j])
        for j in range(HEAD_PAIRS):
            owkv_ref[0, 2 * j] = S[j, 0:HEAD_DIM, 0:HEAD_DIM]
            owkv_ref[0, 2 * j + 1] = S[j, HEAD_DIM:PAIR_DIM, HEAD_DIM:PAIR_DIM]


def _even_prompt(x, g_in, w_in, st_pool, st_shift, st_wkv, prm, start):
    B, T, _ = x.shape
    tt = MIXER_TILE
    nt = T // tt
    bt = lambda b, i: (b, i, 0)
    bs3 = lambda b, i: (b, 0, 0)
    bs4 = lambda b, i: (b, 0, 0, 0)
    c2 = lambda b, i: (0, 0)
    vec = lambda n: pl.BlockSpec((1, n), c2)
    scr = lambda: pltpu.VMEM((tt, RWKV_DIM), F32)
    n_units = (tt // WKV_CHUNK) * HEAD_PAIRS
    return pl.pallas_call(
        functools.partial(_even_prompt_kernel, tt=tt, start=start),
        grid=(B, nt),
        in_specs=[pl.BlockSpec((1, tt, D_MODEL), lambda b, i: (0, 0, 0)),
                  pl.BlockSpec((1, tt, D_MODEL), functools.partial(_next_tile, nt=nt, n_tiles=B * nt)),
                  vec(D_MODEL),
                  pl.BlockSpec((D_MODEL, EVEN_PROJ), c2, pipeline_mode=pl.Buffered(1)),
                  pl.BlockSpec((1, POOL_BUF, POOL_DIM), bs3),
                  pl.BlockSpec((1, 1, RWKV_PROJ), bs3),
                  pl.BlockSpec((1, RWKV_HEADS, HEAD_DIM, HEAD_DIM), bs4),
                  vec(RWKV_PROJ), vec(RWKV_DIM), pl.BlockSpec((LORA_PAD, RWKV_DIM), c2), vec(RWKV_DIM),
                  pl.BlockSpec((LORA_PAD, RWKV_DIM), c2), pl.BlockSpec((LORA_PAD, RWKV_DIM), c2),
                  vec(RWKV_DIM), vec(RWKV_DIM), vec(RWKV_DIM), vec(RWKV_DIM), vec(RWKV_DIM),
                  pl.BlockSpec((SEG_TILE, SEG_TILE), c2), pl.BlockSpec((POOL_DIM, POOL_DIM), c2),
                  vec(POOL_DIM), pl.BlockSpec((WKV_CHUNK, WKV_CHUNK), c2)],
        out_specs=[pl.BlockSpec((1, tt, D_MODEL), bt),
                   pl.BlockSpec((1, POOL_BUF, POOL_DIM), bs3),
                   pl.BlockSpec((1, 1, RWKV_PROJ), bs3),
                   pl.BlockSpec((1, RWKV_HEADS, HEAD_DIM, HEAD_DIM), bs4)],
        out_shape=[jax.ShapeDtypeStruct((B, T, D_MODEL), BF16),
                   jax.ShapeDtypeStruct((B, POOL_BUF, POOL_DIM), F32),
                   jax.ShapeDtypeStruct((B, 1, RWKV_PROJ), F32),
                   jax.ShapeDtypeStruct((B, RWKV_HEADS, HEAD_DIM, HEAD_DIM), F32)],
        scratch_shapes=[pltpu.VMEM((tt, EVEN_PROJ), F32),
                        pltpu.VMEM((16, POOL_DIM), F32), pltpu.VMEM((1, RWKV_PROJ), F32),
                        pltpu.VMEM((HEAD_PAIRS, PAIR_DIM, PAIR_DIM), F32),
                        scr(), scr(), scr(), scr(), scr(), scr(), scr(),
                        pltpu.VMEM((n_units, 2 * PAIR_DIM, PAIR_DIM), BF16),
                        pltpu.VMEM((n_units, 2 * PAIR_DIM, PAIR_DIM), F32),
                        pltpu.VMEM((n_units, PAIR_DIM, PAIR_DIM), F32),
                        pltpu.VMEM((n_units, PAIR_DIM, PAIR_DIM), BF16),
                        pltpu.VMEM((n_units, PAIR_DIM, PAIR_DIM), F32)],
        compiler_params=pltpu.CompilerParams(dimension_semantics=("arbitrary", "arbitrary"),
                                             vmem_limit_bytes=VMEM_LIMIT_BIG),
        name="even_prompt",
    )(x, x, g_in, w_in, st_pool, st_shift, st_wkv, *prm)


def _odd_prompt_kernel(x0_ref, xn_ref, gin_ref, win_ref, stc_ref, stl_ref,
                       lng_ref, lnb_ref, ws_ref, bias_ref, cw_ref, cb_ref, wx_ref, bx_ref, wa_ref,
                       ba_ref, lam_ref, y_ref, oconv_ref, olru_ref, q_s, hconv, hl, mix_s, *, tt):
    i = pl.program_id(1)
    nt = pl.num_programs(1)

    @pl.when(jnp.logical_and(i == 0, pl.program_id(0) == 0))
    def _first_projection():
        h0 = _rmsnorm(x0_ref[0], gin_ref[...]).astype(BF16)
        q_s[...] = jnp.dot(h0, win_ref[...], preferred_element_type=F32)

    @pl.when(i == 0)
    def _init():
        hconv[0:5, :] = jnp.zeros((5, LRU_DIM), F32)
        hconv[5:8, :] = stc_ref[0]
        hl[...] = stl_ref[0]

    q = q_s[...]

    def project_next():
        hb = _rmsnorm(xn_ref[0], gin_ref[...]).astype(BF16)
        yield
        for c0 in range(0, ODD_PROJ, PROJ_PIECE):
            q_s[:, c0:c0 + PROJ_PIECE] = jnp.dot(hb, win_ref[:, c0:c0 + PROJ_PIECE],
                                                 preferred_element_type=F32)
            yield

    def mixers():
        u, vn = _gmlp_pre(q[:, 0:2 * GMLP_DIM], lng_ref[...], lnb_ref[...])
        yield
        rr = lax.broadcasted_iota(jnp.int32, (CHUNK, CHUNK), 0)
        cc = lax.broadcasted_iota(jnp.int32, (CHUNK, CHUNK), 1)
        causal = cc <= rr
        for h in range(GMLP_HEADS):
            wm = jnp.where(causal, ws_ref[h], 0.0).astype(BF16)
            ls = slice(h * CHUNK, (h + 1) * CHUNK)
            for c in range(tt // CHUNK):
                rs = slice(c * CHUNK, (c + 1) * CHUNK)
                mix_s[rs, ls] = (jnp.dot(wm, vn[rs, ls].astype(BF16), preferred_element_type=F32)
                                 + bias_ref[:, ls])
        y_ref[0, :, 0:GMLP_DIM] = (u * mix_s[...]).astype(BF16)
        yield

        gate = _gelu(q[:, 2 * GMLP_DIM:2 * GMLP_DIM + LRU_DIM])
        yield
        xr = q[:, 2 * GMLP_DIM + LRU_DIM:ODD_PROJ]
        hconv[8:8 + tt, :] = xr
        xc = xr * cw_ref[3:4, :] + cb_ref[...]
        for j in range(1, CONV_WIDTH):
            xc = xc + hconv[8 - j:8 - j + tt, :] * cw_ref[3 - j:4 - j, :]
        hconv[0:8, :] = hconv[tt:tt + 8, :]
        yield
        a, b = _lru_gates(xc, wx_ref[...], bx_ref[...], wa_ref[...], ba_ref[...], lam_ref[...])
        yield
        n_groups = tt // SCAN_GROUP
        a = a.reshape(n_groups, SCAN_GROUP, LRU_DIM)
        b = b.reshape(n_groups, SCAN_GROUP, LRU_DIM)
        in_group = lax.broadcasted_iota(jnp.int32, (1, SCAN_GROUP, 1), 1)
        dist = 1
        while dist < SCAN_GROUP:
            keep = in_group >= dist
            a_sh = jnp.where(keep, pltpu.roll(a, dist, 1), 1.0)
            b_sh = jnp.where(keep, pltpu.roll(b, dist, 1), 0.0)
            b = a * b_sh + b
            a = a * a_sh
            dist *= 2
            yield
        carry = hl[...]
        groups = []
        for gi in range(n_groups):
            hg = a[gi] * carry + b[gi]
            groups.append(hg)
            carry = hg[SCAN_GROUP - 1:SCAN_GROUP, :]
        h = jnp.concatenate(groups, axis=0)
        hl[...] = carry
        y_ref[0, :, GMLP_DIM:D_MODEL] = (h * gate).astype(BF16)
        yield

    _run_interleaved(mixers(), project_next())

    @pl.when(i == nt - 1)
    def _fin():
        oconv_ref[0] = hconv[5:8, :]
        olru_ref[0] = hl[...]


def _odd_prompt(x, g_in, w_in, st_conv, st_lru, prm):
    B, T, _ = x.shape
    tt = MIXER_TILE
    nt = T // tt
    bt = lambda b, i: (b, i, 0)
    bs3 = lambda b, i: (b, 0, 0)
    c2 = lambda b, i: (0, 0)
    c3 = lambda b, i: (0, 0, 0)
    vec = lambda n: pl.BlockSpec((1, n), c2)
    return pl.pallas_call(
        functools.partial(_odd_prompt_kernel, tt=tt),
        grid=(B, nt),
        in_specs=[pl.BlockSpec((1, tt, D_MODEL), lambda b, i: (0, 0, 0)),
                  pl.BlockSpec((1, tt, D_MODEL), functools.partial(_next_tile, nt=nt, n_tiles=B * nt)),
                  vec(D_MODEL),
                  pl.BlockSpec((D_MODEL, ODD_PROJ), c2, pipeline_mode=pl.Buffered(1)),
                  pl.BlockSpec((1, CONV_WIDTH - 1, LRU_DIM), bs3),
                  pl.BlockSpec((1, 1, LRU_DIM), bs3),
                  vec(GMLP_DIM), vec(GMLP_DIM),
                  pl.BlockSpec((GMLP_HEADS, CHUNK, CHUNK), c3),
                  pl.BlockSpec((CHUNK, GMLP_DIM), c2),
                  pl.BlockSpec((CONV_WIDTH, LRU_DIM), c2), vec(LRU_DIM),
                  pl.BlockSpec((LRU_DIM, LRU_DIM), c2), vec(LRU_DIM),
                  pl.BlockSpec((LRU_DIM, LRU_DIM), c2), vec(LRU_DIM), vec(LRU_DIM)],
        out_specs=[pl.BlockSpec((1, tt, D_MODEL), bt),
                   pl.BlockSpec((1, CONV_WIDTH - 1, LRU_DIM), bs3),
                   pl.BlockSpec((1, 1, LRU_DIM), bs3)],
        out_shape=[jax.ShapeDtypeStruct((B, T, D_MODEL), BF16),
                   jax.ShapeDtypeStruct((B, CONV_WIDTH - 1, LRU_DIM), F32),
                   jax.ShapeDtypeStruct((B, 1, LRU_DIM), F32)],
        scratch_shapes=[pltpu.VMEM((tt, ODD_PROJ), F32),
                        pltpu.VMEM((8 + tt, LRU_DIM), F32), pltpu.VMEM((1, LRU_DIM), F32),
                        pltpu.VMEM((tt, GMLP_DIM), F32)],
        compiler_params=pltpu.CompilerParams(dimension_semantics=("arbitrary", "arbitrary"),
                                             vmem_limit_bytes=VMEM_LIMIT),
        name="odd_prompt",
    )(x, x, g_in, w_in, st_conv, st_lru, *prm)


def _even_sample_pre_kernel(x_ref, gin_ref, win_ref, stp_ref, sts_ref,
                            mu_ref, w0_ref, wdec_ref, a0_ref, wa_ref, gw2_ref, kk_ref, ka_ref, rk_ref,
                            eseg_ref, poolw_ref, pools_ref,
                            r_ref, w_ref, kkn_ref, kka_ref, kp_ref, v_ref, g_ref, bonus_ref, ya_ref,
                            opool_ref, oshift_ref, *, T, B, start):
    prm = (mu_ref[...], w0_ref[...], wdec_ref[...], a0_ref[...], wa_ref[...], gw2_ref[...],
           kk_ref[...], ka_ref[...], rk_ref[...], eseg_ref[...])
    p = jnp.dot(_rmsnorm(x_ref[...], gin_ref[...]).astype(BF16), win_ref[...], preferred_element_type=F32)
    step = lambda x, t: x[t * B:(t + 1) * B]
    P = p[:, POOL_DIM:EVEN_PROJ]
    Pprev = jnp.concatenate([sts_ref[...], P[0:(T - 1) * B]], axis=0)
    r, kp, v, ld, kk, a, g, bonus = _rwkv_pointwise(P, Pprev, prm)
    w = jnp.exp(ld)
    kka = kk * a
    full = [stp_ref[s] for s in range(POOL_BUF)] + [step(p, t)[:, 0:POOL_DIM] for t in range(T)]
    wl = _pool_window_lanes()
    for t in range(T):
        r_ref[t] = jnp.transpose(step(r, t))
        w_ref[t] = jnp.transpose(step(w, t))
        kkn_ref[t] = jnp.transpose(step(kk, t))
        kka_ref[t] = jnp.transpose(step(kka, t))
        kp_ref[t] = jnp.transpose(step(kp, t))
        v_ref[t] = jnp.transpose(step(v, t))
        g_ref[t] = step(g, t)
        bonus_ref[t] = step(bonus, t)
        e = POOL_BUF + t
        s2 = full[e] + full[e - 1]
        s4 = s2 + full[e - 2] + full[e - 3]
        s8 = s4 + full[e - 4] + full[e - 5] + full[e - 6] + full[e - 7]
        s16 = s8
        for s in range(8, 16):
            s16 = s16 + full[e - s]
        sel = _pool_lane_select(s2, s4, s8, s16)
        cnt = jnp.minimum(wl, start + t + 1).astype(F32)
        d = sel / cnt - full[e]
        ya_ref[t] = _bdot(d, poolw_ref[...]) * pools_ref[...]
    for s in range(POOL_BUF):
        opool_ref[s] = full[T + s]
    oshift_ref[...] = step(P, T - 1)


def _even_sample_pre(x, g_in, w_in, st_pool, st_shift, prm, start):
    B = st_pool.shape[1]
    T = x.shape[0] // B
    cm = jax.ShapeDtypeStruct((T, RWKV_DIM, B), F32)
    bm = jax.ShapeDtypeStruct((T, B, RWKV_DIM), F32)
    return pl.pallas_call(
        functools.partial(_even_sample_pre_kernel, T=T, B=B, start=start),
        out_shape=[cm] * 6 + [bm] * 2 + [jax.ShapeDtypeStruct((T, B, POOL_DIM), F32),
                                   jax.ShapeDtypeStruct((POOL_BUF, B, POOL_DIM), F32),
                                   jax.ShapeDtypeStruct((B, RWKV_PROJ), F32)],
        compiler_params=pltpu.CompilerParams(vmem_limit_bytes=VMEM_LIMIT),
        name="even_sample_pre",
    )(x, g_in, w_in, st_pool, st_shift, *prm)


def _wkv_sample_kernel(r_ref, w_ref, kk_ref, kka_ref, kp_ref, v_ref, s_ref, o_ref, so_ref, *, T):
    group = range(WKV_SAMPLE_GROUP)

    def body(ib, carry):
        v0 = pl.multiple_of(ib * WKV_SAMPLE_GROUP, WKV_SAMPLE_GROUP)
        blk = pl.ds(v0, WKV_SAMPLE_GROUP)
        S = [s_ref[0, v0 + u] for u in group]
        for t in range(T):
            kk, w, kka, kp, r = kk_ref[t], w_ref[t], kka_ref[t], kp_ref[t], r_ref[t]
            vv = v_ref[t, blk, :]
            sk = [jnp.sum(S[u] * kk, axis=0, keepdims=True) for u in group]
            S = [S[u] * w - sk[u] * kka + vv[u:u + 1, :] * kp for u in group]
            o_ref[t, blk, :] = jnp.concatenate(
                [jnp.sum(S[u] * r, axis=0, keepdims=True) for u in group], axis=0)
        for u in group:
            so_ref[0, v0 + u] = S[u]
        return carry

    lax.fori_loop(0, HEAD_DIM // WKV_SAMPLE_GROUP, body, 0)


def _wkv_sample(r, w, kk, kka, kp, v, s):
    T, _, B = r.shape
    row_spec = pl.BlockSpec((T, HEAD_DIM, B), lambda h: (0, h, 0))
    st_spec = pl.BlockSpec((1, HEAD_DIM, HEAD_DIM, B), lambda h: (h, 0, 0, 0))
    return pl.pallas_call(
        functools.partial(_wkv_sample_kernel, T=T),
        grid=(RWKV_HEADS,),
        in_specs=[row_spec] * 6 + [st_spec],
        out_specs=[row_spec, st_spec],
        out_shape=[jax.ShapeDtypeStruct((T, RWKV_DIM, B), F32),
                   jax.ShapeDtypeStruct((RWKV_HEADS, HEAD_DIM, HEAD_DIM, B), F32)],
        compiler_params=pltpu.CompilerParams(dimension_semantics=("arbitrary",),
                                             vmem_limit_bytes=VMEM_LIMIT),
        name="wkv_sample",
    )(r, w, kk, kka, kp, v, s)


def _even_sample_post_kernel(o_ref, bonus_ref, g_ref, ya_ref, gng_ref, gnb_ref, eseg_ref, y_ref, *, T):
    for t in range(T):
        o = jnp.transpose(o_ref[t])
        yb = _rwkv_post(o, bonus_ref[t], g_ref[t], gng_ref[...], gnb_ref[...], eseg_ref[...])
        y_ref[t, :, 0:POOL_DIM] = ya_ref[t].astype(BF16)
        y_ref[t, :, POOL_DIM:D_MODEL] = yb.astype(BF16)


def _even_sample_post(o, bonus, g, ya, gn_g, gn_b, e_seg):
    T, _, B = o.shape
    return pl.pallas_call(
        functools.partial(_even_sample_post_kernel, T=T),
        out_shape=jax.ShapeDtypeStruct((T, B, D_MODEL), BF16),
        compiler_params=pltpu.CompilerParams(vmem_limit_bytes=VMEM_LIMIT),
        name="even_sample_post",
    )(o, bonus, g, ya, gn_g, gn_b, e_seg)


def _odd_sample_kernel(x_ref, gin_ref, win_ref, stc_ref, stl_ref,
                       lng_ref, lnb_ref, wsm_ref, bsm_ref, cw_ref, cb_ref, wx_ref, bx_ref, wa_ref,
                       ba_ref, lam_ref, y_ref, v_ref, oconv_ref, olru_ref, *, T, B):
    q = jnp.dot(_rmsnorm(x_ref[...], gin_ref[...]).astype(BF16), win_ref[...], preferred_element_type=F32)
    step = lambda x, t: x[t * B:(t + 1) * B]
    u_all, vn_all = _gmlp_pre(q[:, 0:2 * GMLP_DIM], lng_ref[...], lnb_ref[...])
    gate_all = _gelu(q[:, 2 * GMLP_DIM:2 * GMLP_DIM + LRU_DIM])
    us = [step(u_all, t) for t in range(T)]
    vns = [step(vn_all, t) for t in range(T)]
    for t in range(T):
        v_ref[t] = vns[t]
    full = [stc_ref[s] for s in range(CONV_WIDTH - 1)] + \
           [step(q, t)[:, 2 * GMLP_DIM + LRU_DIM:ODD_PROJ] for t in range(T)]
    h = stl_ref[...]
    for t in range(T):
        mix = bsm_ref[t:t + 1, :]
        for j in range(t + 1):
            mix = mix + wsm_ref[t * T + j:t * T + j + 1, :] * vns[j]
        y_ref[t, :, 0:GMLP_DIM] = (us[t] * mix).astype(BF16)
        xc = full[t + CONV_WIDTH - 1] * cw_ref[CONV_WIDTH - 1:CONV_WIDTH, :] + cb_ref[...]
        for j in range(CONV_WIDTH - 1):
            xc = xc + full[t + j] * cw_ref[j:j + 1, :]
        a, b = _lru_gates(xc, wx_ref[...], bx_ref[...], wa_ref[...], ba_ref[...], lam_ref[...])
        h = a * h + b
        y_ref[t, :, GMLP_DIM:D_MODEL] = (h * step(gate_all, t)).astype(BF16)
    for s in range(CONV_WIDTH - 1):
        oconv_ref[s] = full[T + s]
    olru_ref[...] = h


def _odd_sample(x, g_in, w_in, st_conv, st_lru, prm):
    B = st_lru.shape[0]
    T = x.shape[0] // B
    return pl.pallas_call(
        functools.partial(_odd_sample_kernel, T=T, B=B),
        out_shape=[jax.ShapeDtypeStruct((T, B, D_MODEL), BF16),
                   jax.ShapeDtypeStruct((T, B, GMLP_DIM), F32),
                   jax.ShapeDtypeStruct((CONV_WIDTH - 1, B, LRU_DIM), F32),
                   jax.ShapeDtypeStruct((B, LRU_DIM), F32)],
        compiler_params=pltpu.CompilerParams(vmem_limit_bytes=VMEM_LIMIT),
        name="odd_sample",
    )(x, g_in, w_in, st_conv, st_lru, *prm)


def _block_diag(w):
    n, c, d = w.shape
    eye = jnp.eye(n, dtype=w.dtype)
    return (eye[:, None, :, None] * w[:, :, None, :]).reshape(n * c, n * d)


def _row(x):
    return x.reshape(1, -1)


def kernel(x_prompt, x_sample, state_pool, state_shift, state_wkv, state_conv, state_lru, ev_norm_g, ev_w_in, pool_w, pool_scale, rwkv_mu, rwkv_w0, rwkv_w_w2, rwkv_a0, rwkv_a_w2, rwkv_g_w2, rwkv_k_k, rwkv_k_a, rwkv_r_k, rwkv_gn_g, rwkv_gn_b, ev_w_out, od_norm_g, od_w_in, gmlp_ln_g, gmlp_ln_b, gmlp_ws, gmlp_bs, lru_conv_w, lru_conv_b, lru_wx, lru_bx, lru_wa, lru_ba, lru_lam, od_w_out, ff_norm_g, ff_w1, ff_w2, final_norm_g):
    B, T, _ = x_prompt.shape
    DB, DT, _ = x_sample.shape

    seg_ids = jnp.arange(SEG_TILE) // HEAD_DIM
    e_seg = (seg_ids[:, None] == seg_ids[None, :]).astype(BF16)
    tri = (jnp.arange(WKV_CHUNK)[None, :] <= jnp.arange(WKV_CHUNK)[:, None]).astype(BF16)
    zlora = jnp.zeros((LORA_PAD // 2, RWKV_DIM), F32)

    ev_common = (_row(rwkv_mu[0]), _row(rwkv_w0[0]),
                 jnp.concatenate([rwkv_w_w2[0], zlora], 0).astype(BF16), _row(rwkv_a0[0]),
                 jnp.concatenate([zlora, rwkv_a_w2[0]], 0).astype(BF16), rwkv_g_w2[0].astype(BF16),
                 _row(rwkv_k_k[0]), _row(rwkv_k_a[0]), _row(rwkv_r_k[0]))
    gn_g, gn_b = _row(rwkv_gn_g[0]), _row(rwkv_gn_b[0])
    pool_bd = _block_diag(pool_w[0]).astype(BF16)
    pool_sc = _row(pool_scale[0])
    w_in0 = ev_w_in[0].astype(BF16)
    g_in0 = _row(ev_norm_g[0])

    xp = x_prompt.reshape(B * T, D_MODEL)
    xs = jnp.transpose(x_sample, (1, 0, 2)).reshape(DT * DB, D_MODEL)

    yp, p_pool, p_shift, p_wkv = _even_prompt(
        x_prompt, g_in0, w_in0,
        jnp.zeros((B, POOL_BUF, POOL_DIM), F32), jnp.zeros((B, 1, RWKV_PROJ), F32),
        jnp.zeros((B, RWKV_HEADS, HEAD_DIM, HEAD_DIM), F32),
        ev_common + (gn_g, gn_b, e_seg, pool_bd, pool_sc, tri), 0)

    pre = _even_sample_pre(xs, g_in0, w_in0, jnp.transpose(state_pool[0], (1, 0, 2)), state_shift[0],
                           ev_common + (e_seg, pool_bd, pool_sc), PAST_LEN)
    r_s, w_s, kk_s, kka_s, kp_s, v_s, g_s, bonus_s, ya_s, s_pool_tm, s_shift = pre
    o_s, s_wkv_bl = _wkv_sample(r_s, w_s, kk_s, kka_s, kp_s, v_s,
                                jnp.transpose(state_wkv[0], (1, 2, 3, 0)))
    ys = _even_sample_post(o_s, bonus_s, g_s, ya_s, gn_g, gn_b, e_seg)

    w_out0 = ev_w_out[0].astype(BF16)
    ffg = lambda l: _row(ff_norm_g[l])
    gfin = _row(final_norm_g)
    ff_w1_b, ff_w2_b = ff_w1.astype(BF16), ff_w2.astype(BF16)
    xp = _ffn(xp, yp.reshape(B * T, D_MODEL), w_out0, ffg(0), ff_w1_b, ff_w2_b, gfin, 0, False)
    xs = _ffn(xs, ys.reshape(DT * DB, D_MODEL), w_out0, ffg(0), ff_w1_b, ff_w2_b, gfin, 0, False)

    w_in1 = od_w_in[0].astype(BF16)
    g_in1 = _row(od_norm_g[0])
    lru_common = (lru_conv_w[0], _row(lru_conv_b[0]), _block_diag(lru_wx[0]).astype(BF16), _row(lru_bx[0]),
                  _block_diag(lru_wa[0]).astype(BF16), _row(lru_ba[0]), _row(lru_lam[0]))
    ln = (_row(gmlp_ln_g[0]), _row(gmlp_ln_b[0]))
    bias_full = jnp.repeat(jnp.transpose(gmlp_bs[0]), CHUNK, axis=1)
    yp, p_conv, p_lru = _odd_prompt(
        xp.reshape(B, T, D_MODEL), g_in1, w_in1,
        jnp.zeros((B, CONV_WIDTH - 1, LRU_DIM), F32), jnp.zeros((B, 1, LRU_DIM), F32),
        ln + (gmlp_ws[0], bias_full) + lru_common)

    ws_small = jnp.repeat(jnp.transpose(gmlp_ws[0][:, :DT, :DT], (1, 2, 0)).reshape(DT * DT, GMLP_HEADS),
                          CHUNK, axis=1)
    ys, s_v, s_conv_tm, s_lru = _odd_sample(
        xs, g_in1, w_in1, jnp.transpose(state_conv[0], (1, 0, 2)), state_lru[0],
        ln + (ws_small, bias_full[:DT]) + lru_common)

    w_out1 = od_w_out[0].astype(BF16)
    xp = _ffn(xp, yp.reshape(B * T, D_MODEL), w_out1, ffg(1), ff_w1_b, ff_w2_b, gfin, 1, True)
    xs = _ffn(xs, ys.reshape(DT * DB, D_MODEL), w_out1, ffg(1), ff_w1_b, ff_w2_b, gfin, 1, True)

    tm2bm = lambda t: jnp.transpose(t, (1, 0, 2))
    y_prompt = xp.reshape(B, T, D_MODEL)
    y_sample = tm2bm(xs.reshape(DT, DB, D_MODEL))
    return (y_prompt, y_sample,
            p_pool[None], p_shift.reshape(1, B, RWKV_PROJ), p_wkv[None],
            p_conv[None], p_lru.reshape(1, B, LRU_DIM),
            tm2bm(s_pool_tm)[None], s_shift[None],
            jnp.transpose(s_wkv_bl, (3, 0, 1, 2))[None],
            tm2bm(s_conv_tm)[None], s_lru[None], tm2bm(s_v)[None])
```
